```python
import math
import jax, jax.numpy as jnp
from jax import lax
import numpy as np

D_MODEL = 4096
BATCH = 2
SEQ = 8192
DEPTH = 1

SSD_D_INNER = 2048
SSD_HEAD_DIM = 64
SSD_HEADS = SSD_D_INNER // SSD_HEAD_DIM
SSD_GROUPS = 8
SSD_HPG = SSD_HEADS // SSD_GROUPS
SSD_STATE = 128
SSD_CONV = 4
SSD_CHUNK = 256
SSD_CONV_CH = SSD_D_INNER + 2 * SSD_GROUPS * SSD_STATE
DT_MIN = 0.001
DT_MAX = 0.1

NSA_HEADS = 16
NSA_KV_HEADS = 4
NSA_GQA = NSA_HEADS // NSA_KV_HEADS
NSA_HEAD_DIM = 128
NSA_WIDTH = NSA_HEADS * NSA_HEAD_DIM
NSA_KV_WIDTH = NSA_KV_HEADS * NSA_HEAD_DIM
CMP_BLOCK = 32
CMP_STRIDE = 16
CMP_HIDDEN = 256
SLC_BLOCK = 64
SLC_TOPK = 16
WINDOW = 512
NSA_Q_BLOCK = 64
D_MIX = SSD_D_INNER + NSA_WIDTH

IN_SPLITS = (SSD_D_INNER,
             SSD_D_INNER + SSD_CONV_CH,
             SSD_D_INNER + SSD_CONV_CH + SSD_HEADS,
             SSD_D_INNER + SSD_CONV_CH + SSD_HEADS + NSA_WIDTH,
             SSD_D_INNER + SSD_CONV_CH + SSD_HEADS + NSA_WIDTH + 6 * NSA_KV_WIDTH)
D_IN_PROJ = SSD_D_INNER + SSD_CONV_CH + SSD_HEADS + NSA_WIDTH + 6 * NSA_KV_WIDTH + 3 * NSA_HEADS

N_EXPERTS = 32
TOP_K = 4
D_EXPERT = 1536
SWIGLU_LIMIT = 7.0
SWIGLU_ALPHA = 1.702
MOE_BLOCK = 256

EPS = 1e-5
NEG = -1e30

kernel_name = "hymba_ssd_nsa_moe_layer"


def rms_norm(x, g):
    xf = x.astype(jnp.float32)
    y = xf * lax.rsqrt(jnp.mean(xf * xf, axis=-1, keepdims=True) + EPS)
    return (y * g.astype(jnp.float32)).astype(x.dtype)


def masked_softmax(s, mask):
    s = jnp.where(mask, s, NEG)
    p = jnp.exp(s - jnp.max(s, axis=-1, keepdims=True)) * mask
    return p / jnp.maximum(jnp.sum(p, axis=-1, keepdims=True), 1e-30)


def causal_depthwise_conv(u, w, b):
    y = lax.conv_general_dilated(u, w[:, None, :].astype(u.dtype), window_strides=(1,),
                                 padding=[(SSD_CONV - 1, 0)],
                                 dimension_numbers=('NWC', 'WIO', 'NWC'),
                                 feature_group_count=u.shape[-1])
    return y + b.astype(u.dtype)


def ssd_mixer(z, xbc, dt_raw, conv_w, conv_b, dt_bias, a_log, d_skip, norm_g):
    f32 = jnp.float32
    bsz, t_len, _ = xbc.shape
    xbc = jax.nn.silu(causal_depthwise_conv(xbc, conv_w, conv_b))
    xs, bm, cm = jnp.split(xbc, [SSD_D_INNER, SSD_D_INNER + SSD_GROUPS * SSD_STATE], axis=-1)
    dt = jax.nn.softplus(dt_raw.astype(f32) + dt_bias.astype(f32))
    a = -jnp.exp(a_log.astype(f32)).reshape(SSD_GROUPS, SSD_HPG)
    pad = (-t_len) % SSD_CHUNK
    nc = (t_len + pad) // SSD_CHUNK

    def chunked(u, tail):
        u = jnp.pad(u.astype(f32), [(0, 0), (0, pad)] + [(0, 0)] * (u.ndim - 2))
        return u.reshape((bsz, nc, SSD_CHUNK) + tail)

    x_c = chunked(xs, (SSD_GROUPS, SSD_HPG, SSD_HEAD_DIM))
    b_c = chunked(bm, (SSD_GROUPS, SSD_STATE))
    c_c = chunked(cm, (SSD_GROUPS, SSD_STATE))
    dt_c = chunked(dt, (SSD_GROUPS, SSD_HPG))
    a_cs = jnp.cumsum(dt_c * a, axis=2)
    xdt = x_c * dt_c[..., None]

    causal = jnp.tril(jnp.ones((SSD_CHUNK, SSD_CHUNK), bool))[None, None, :, :, None, None]
    cb = jnp.einsum('bclgn,bcsgn->bclsg', c_c, b_c)
    w = cb[..., None] * jnp.exp(jnp.where(causal, a_cs[:, :, :, None] - a_cs[:, :, None], NEG))
    y_diag = jnp.einsum('bclsgj,bcsgjp->bclgjp', w, xdt)

    xw = xdt * jnp.exp(a_cs[:, :, -1:] - a_cs)[..., None]
    states = jnp.einsum('bclgn,bclgjp->bcgjpn', b_c, xw)
    chunk_decay = jnp.exp(a_cs[:, :, -1])

    def step(h, inp):
        dec, st = inp
        return h * dec[..., None, None] + st, h

    h0 = jnp.zeros((bsz, SSD_GROUPS, SSD_HPG, SSD_HEAD_DIM, SSD_STATE), f32)
    _, prev = lax.scan(step, h0, (jnp.moveaxis(chunk_decay, 1, 0), jnp.moveaxis(states, 1, 0)))
    prev = jnp.moveaxis(prev, 0, 1)
    y_off = jnp.einsum('bclgn,bcgjpn->bclgjp', c_c, prev) * jnp.exp(a_cs)[..., None]

    y = y_diag + y_off + x_c * d_skip.astype(f32).reshape(SSD_GROUPS, SSD_HPG)[:, :, None]
    y = y.reshape(bsz, nc * SSD_CHUNK, SSD_D_INNER)[:, :t_len]
    y = (y * jax.nn.silu(z.astype(f32))).reshape(bsz, t_len, SSD_GROUPS, -1)
    y = y * lax.rsqrt(jnp.mean(y * y, axis=-1, keepdims=True) + EPS)
    y = y.reshape(bsz, t_len, SSD_D_INNER) * norm_g.astype(f32)
    return y.astype(z.dtype)


def nsa_mixer(q, kv, gate_logits, q_norm_g, k_norm_g, cmp_pe, cmp_w1, cmp_w2):
    f32 = jnp.float32
    bsz, t_len, _ = q.shape
    dh = NSA_HEAD_DIM
    q = rms_norm(q.reshape(bsz, t_len, NSA_HEADS, dh), q_norm_g)
    k_cmp, v_cmp, k_slc, v_slc, k_win, v_win = [
        u.reshape(bsz, t_len, NSA_KV_HEADS, dh) for u in jnp.split(kv, 6, axis=-1)]

    half = CMP_STRIDE

    def compress(u, pe, w1, w2):
        ch = u.reshape(bsz, t_len // CMP_STRIDE, CMP_STRIDE, NSA_KV_HEADS, dh)
        hid = (jnp.einsum('bcshd,sdf->bchf', ch[:, :-1] + pe[:half][:, None, :], w1[:half]) +
               jnp.einsum('bcshd,sdf->bchf', ch[:, 1:] + pe[half:][:, None, :], w1[half:]))
        return jnp.einsum('bchf,fd->bchd', jax.nn.silu(hid), w2)

    kc = rms_norm(compress(k_cmp, cmp_pe[0], cmp_w1[0], cmp_w2[0]), k_norm_g[0])
    vc = compress(v_cmp, cmp_pe[1], cmp_w1[1], cmp_w2[1])
    ks = rms_norm(k_slc, k_norm_g[1])
    kw = rms_norm(k_win, k_norm_g[2])

    n_cmp = t_len // CMP_STRIDE - 1
    cmp_end = jnp.arange(n_cmp) * CMP_STRIDE + CMP_BLOCK - 1
    n_slc = t_len // SLC_BLOCK
    topk = min(SLC_TOPK, n_slc)
    ks_blocks = ks.reshape(bsz, n_slc, SLC_BLOCK, NSA_KV_HEADS, dh).transpose(0, 3, 1, 2, 4)
    vs_blocks = v_slc.reshape(bsz, n_slc, SLC_BLOCK, NSA_KV_HEADS, dh).transpose(0, 3, 1, 2, 4)
    kw_pad = jnp.pad(kw, ((0, 0), (WINDOW, 0), (0, 0), (0, 0)))
    vw_pad = jnp.pad(v_win, ((0, 0), (WINDOW, 0), (0, 0), (0, 0)))
    gates = jax.nn.sigmoid(gate_logits.astype(f32)).reshape(bsz, t_len, NSA_KV_HEADS, NSA_GQA, 3)

    nq = t_len // NSA_Q_BLOCK
    qb = q.reshape(bsz, nq, NSA_Q_BLOCK, NSA_KV_HEADS, NSA_GQA, dh).transpose(1, 0, 2, 3, 4, 5)
    gb = gates.reshape(bsz, nq, NSA_Q_BLOCK, NSA_KV_HEADS, NSA_GQA, 3).transpose(1, 0, 2, 3, 4, 5)
    scale = dh ** -0.5
    b_ix = jnp.arange(bsz)[:, None, None, None]
    h_ix = jnp.arange(NSA_KV_HEADS)[None, :, None, None]
    j_slc = jnp.arange(n_slc)

    def query_block(args):
        i, qi, gi = args
        t = i * NSA_Q_BLOCK + jnp.arange(NSA_Q_BLOCK)
        s_c = jnp.einsum('bqhgd,bchd->bhgqc', qi, kc).astype(f32) * scale
        p_c = masked_softmax(s_c, cmp_end[None, :] <= t[:, None])
        o_c = jnp.einsum('bhgqc,bchd->bqhgd', p_c.astype(vc.dtype), vc)
        imp = jnp.sum(p_c, axis=2)
        sub = (jnp.pad(imp, ((0, 0), (0, 0), (0, 0), (1, 0))) +
               jnp.pad(imp, ((0, 0), (0, 0), (0, 0), (0, 1))))
        imp = sub.reshape(bsz, NSA_KV_HEADS, NSA_Q_BLOCK, n_slc, SLC_BLOCK // CMP_STRIDE).sum(-1)
        cur = (t // SLC_BLOCK)[:, None]
        forced = (j_slc[None, :] == 0) | (j_slc[None, :] == cur) | (j_slc[None, :] == cur - 1)
        imp = jnp.where(forced, jnp.inf, jnp.where(j_slc[None, :] <= cur, imp, -jnp.inf))
        _, idx = lax.top_k(imp, topk)
        kg = ks_blocks[b_ix, h_ix, idx]
        vg = vs_blocks[b_ix, h_ix, idx]
        s_s = jnp.einsum('bqhgd,bhqksd->bhgqks', qi, kg).astype(f32) * scale
        pos = idx[..., None] * SLC_BLOCK + jnp.arange(SLC_BLOCK)
        m_s = (pos <= t[None, None, :, None, None]).reshape(bsz, NSA_KV_HEADS, 1, NSA_Q_BLOCK, -1)
        p_s = masked_softmax(s_s.reshape(bsz, NSA_KV_HEADS, NSA_GQA, NSA_Q_BLOCK, -1), m_s)
        p_s = p_s.reshape(bsz, NSA_KV_HEADS, NSA_GQA, NSA_Q_BLOCK, topk, SLC_BLOCK)
        o_s = jnp.einsum('bhgqks,bhqksd->bqhgd', p_s.astype(vg.dtype), vg)
        kwi = lax.dynamic_slice_in_dim(kw_pad, i * NSA_Q_BLOCK, NSA_Q_BLOCK + WINDOW, axis=1)
        vwi = lax.dynamic_slice_in_dim(vw_pad, i * NSA_Q_BLOCK, NSA_Q_BLOCK + WINDOW, axis=1)
        kpos = i * NSA_Q_BLOCK - WINDOW + jnp.arange(NSA_Q_BLOCK + WINDOW)
        m_w = ((kpos[None, :] <= t[:, None]) & (kpos[None, :] > t[:, None] - WINDOW)
               & (kpos[None, :] >= 0))
        s_w = jnp.einsum('bqhgd,bkhd->bhgqk', qi, kwi).astype(f32) * scale
        p_w = masked_softmax(s_w, m_w)
        o_w = jnp.einsum('bhgqk,bkhd->bqhgd', p_w.astype(vwi.dtype), vwi)
        out = (gi[..., 0:1] * o_c.astype(f32) + gi[..., 1:2] * o_s.astype(f32)
               + gi[..., 2:3] * o_w.astype(f32))
        return out.reshape(bsz, NSA_Q_BLOCK, NSA_WIDTH).astype(qi.dtype)

    out = lax.map(query_block, (jnp.arange(nq), qb, gb))
    return out.transpose(1, 0, 2, 3).reshape(bsz, t_len, NSA_WIDTH)


def moe_ffn(h, w_router, b_router, w_up, b_up, w_down, b_down):
    f32 = jnp.float32
    bsz, t_len, d = h.shape
    xt = h.reshape(-1, d)
    logits = (xt @ w_router + b_router).astype(f32)
    top_val, top_idx = lax.top_k(logits, TOP_K)
    top_w = jax.nn.softmax(top_val, axis=-1)
    n_assign = xt.shape[0] * TOP_K
    e_flat = top_idx.reshape(-1)
    w_flat = top_w.reshape(-1)
    tok_flat = jnp.arange(n_assign) // TOP_K
    order = jnp.argsort(e_flat)
    e_sorted = e_flat[order]
    counts = jnp.zeros((N_EXPERTS,), jnp.int32).at[e_flat].add(1)
    padded = (counts + MOE_BLOCK - 1) // MOE_BLOCK * MOE_BLOCK
    start = jnp.cumsum(counts) - counts
    pend = jnp.cumsum(padded)
    pstart = pend - padded
    dest = pstart[e_sorted] + (jnp.arange(n_assign) - start[e_sorted])
    n_blocks = -(-n_assign // MOE_BLOCK) + N_EXPERTS
    n_rows = n_blocks * MOE_BLOCK
    row_tok = jnp.zeros((n_rows,), jnp.int32).at[dest].set(tok_flat[order])
    row_w = jnp.zeros((n_rows,), f32).at[dest].set(w_flat[order])
    blk_expert = jnp.minimum(
        jnp.searchsorted(pend, jnp.arange(n_blocks) * MOE_BLOCK, side='right'), N_EXPERTS - 1)

    def expert_block(args):
        e, tok, wgt = args
        xb = xt[tok]
        u = (xb @ w_up[e] + b_up[e]).astype(f32)
        glu, lin = jnp.split(u, 2, axis=-1)
        glu = jnp.minimum(glu, SWIGLU_LIMIT)
        lin = jnp.clip(lin, -SWIGLU_LIMIT, SWIGLU_LIMIT)
        act = glu * jax.nn.sigmoid(SWIGLU_ALPHA * glu) * (lin + 1.0)
        y = act.astype(xb.dtype) @ w_down[e] + b_down[e]
        return y * wgt[:, None].astype(y.dtype)

    ys = lax.map(expert_block, (blk_expert, row_tok.reshape(n_blocks, MOE_BLOCK),
                                row_w.reshape(n_blocks, MOE_BLOCK)))
    out = jnp.zeros_like(xt).at[row_tok].add(ys.reshape(n_rows, d))
    return out.reshape(bsz, t_len, d)


def setup_inputs(seed: int = 0) -> dict:
    key = jax.random.key(seed)
    ks = jax.random.split(key, 24)
    f32 = jnp.float32
    L = DEPTH

    def nrm(k, shape, scale):
        return jax.random.normal(k, shape, f32) * scale

    dt = jnp.exp(jax.random.uniform(ks[5], (L, SSD_HEADS), f32)
                 * (math.log(DT_MAX) - math.log(DT_MIN)) + math.log(DT_MIN))
    dt_bias = dt + jnp.log(-jnp.expm1(-dt))
    a_log = jnp.log(jax.random.uniform(ks[6], (L, SSD_HEADS), f32, 1.0, 16.0))
    return {
        "x": nrm(ks[0], (BATCH, SEQ, D_MODEL), 1.0),
        "ln1_g": 1.0 + nrm(ks[1], (L, D_MODEL), 0.02),
        "w_in": nrm(ks[2], (L, D_MODEL, D_IN_PROJ), D_MODEL ** -0.5),
        "conv_w": nrm(ks[3], (L, SSD_CONV, SSD_CONV_CH), SSD_CONV ** -0.5),
        "conv_b": nrm(ks[4], (L, SSD_CONV_CH), 0.02),
        "dt_bias": dt_bias,
        "a_log": a_log,
        "d_skip": 1.0 + nrm(ks[7], (L, SSD_HEADS), 0.1),
        "ssd_norm_g": 1.0 + nrm(ks[8], (L, SSD_D_INNER), 0.02),
        "q_norm_g": 1.0 + nrm(ks[9], (L, NSA_HEAD_DIM), 0.02),
        "k_norm_g": 1.0 + nrm(ks[10], (L, 3, NSA_HEAD_DIM), 0.02),
        "cmp_pe": nrm(ks[11], (L, 2, CMP_BLOCK, NSA_HEAD_DIM), 0.1),
        "cmp_w1": nrm(ks[12], (L, 2, CMP_BLOCK, NSA_HEAD_DIM, CMP_HIDDEN), (CMP_BLOCK * NSA_HEAD_DIM) ** -0.5),
        "cmp_w2": nrm(ks[13], (L, 2, CMP_HIDDEN, NSA_HEAD_DIM), CMP_HIDDEN ** -0.5),
        "w_out": nrm(ks[14], (L, D_MIX, D_MODEL), D_MIX ** -0.5),
        "ln2_g": 1.0 + nrm(ks[15], (L, D_MODEL), 0.02),
        "w_router": nrm(ks[16], (L, D_MODEL, N_EXPERTS), D_MODEL ** -0.5),
        "b_router": nrm(ks[17], (L, N_EXPERTS), 0.01),
        "w_up": nrm(ks[18], (L, N_EXPERTS, D_MODEL, 2 * D_EXPERT), D_MODEL ** -0.5),
        "b_up": nrm(ks[19], (L, N_EXPERTS, 2 * D_EXPERT), 0.02),
        "w_down": nrm(ks[20], (L, N_EXPERTS, D_EXPERT, D_MODEL), D_EXPERT ** -0.5),
        "b_down": nrm(ks[21], (L, N_EXPERTS, D_MODEL), 0.02),
    }


def reference(x, ln1_g, w_in, conv_w, conv_b, dt_bias, a_log, d_skip, ssd_norm_g, q_norm_g,
              k_norm_g, cmp_pe, cmp_w1, cmp_w2, w_out, ln2_g, w_router, b_router, w_up, b_up,
              w_down, b_down):
    for l in range(DEPTH):
        h = rms_norm(x, ln1_g[l])
        proj = h @ w_in[l]
        z, xbc, dt_raw, q, kv, gate_logits = jnp.split(proj, list(IN_SPLITS), axis=-1)
        y_ssd = ssd_mixer(z, xbc, dt_raw, conv_w[l], conv_b[l], dt_bias[l], a_log[l],
                          d_skip[l], ssd_norm_g[l])
        y_nsa = nsa_mixer(q, kv, gate_logits, q_norm_g[l], k_norm_g[l], cmp_pe[l],
                          cmp_w1[l], cmp_w2[l])
        x = x + jnp.concatenate([y_ssd, y_nsa], axis=-1) @ w_out[l]
        x = x + moe_ffn(rms_norm(x, ln2_g[l]), w_router[l], b_router[l], w_up[l], b_up[l],
                        w_down[l], b_down[l])
    return x
```

```python
import functools
import math

import jax
import jax.numpy as jnp
from jax import lax
from jax.experimental import pallas as pl
from jax.experimental.pallas import tpu as pltpu

f32 = jnp.float32
bf16 = jnp.bfloat16
i32 = jnp.int32

D_MODEL = 4096
SSD_D_INNER = 2048
SSD_HEAD_DIM = 64
SSD_HEADS = 32
SSD_GROUPS = 8
SSD_HPG = 4
SSD_STATE = 128
SSD_CONV = 4
SSD_CHUNK = 256
SSD_GN = SSD_GROUPS * SSD_STATE
SSD_GW = SSD_HPG * SSD_HEAD_DIM
NSA_HEADS = 16
NSA_KV_HEADS = 4
NSA_GQA = 4
DH = 128
NSA_WIDTH = 2048
NSA_KV_WIDTH = 512
CMP_BLOCK = 32
CMP_STRIDE = 16
CMP_HIDDEN = 256
SLC_BLOCK = 64
SLC_TOPK = 16
WINDOW = 512
N_EXPERTS = 32
TOP_K = 4
D_EXPERT = 1536
SWIGLU_LIMIT = 7.0
SWIGLU_ALPHA = 1.702
EPS = 1e-5
NEG = -1e30

COL_Z = 0
COL_X = 2048
COL_B = 4096
COL_C = 5120
COL_Q = 6144
COL_KV = 8192
COL_DT = 11264
GATE_OFF = 32
NP_PROJ = 11520

LANE = 128
VMEM_LIMIT = 56 * 1024 * 1024


def _cparams(n_axes):
    return pltpu.CompilerParams(dimension_semantics=("arbitrary",) * n_axes,
                                vmem_limit_bytes=VMEM_LIMIT)


def _silu(v):
    return v * (1.0 / (1.0 + jnp.exp(-v)))


def _inproj_body(x_ref, g_ref, w_ref, o_ref, h_ref):
    @pl.when(pl.program_id(1) == 0)
    def _():
        x = x_ref[...]
        ms = jnp.mean(x * x, axis=-1, keepdims=True)
        h_ref[...] = (x * lax.rsqrt(ms + EPS) * g_ref[...]).astype(bf16)

    o_ref[...] = jnp.dot(h_ref[...], w_ref[...], preferred_element_type=f32)


def _inproj(x2, g, w, tm=512, tn=768):
    n, d = x2.shape
    npj = w.shape[1]
    return pl.pallas_call(
        _inproj_body,
        grid=(n // tm, npj // tn),
        in_specs=[pl.BlockSpec((tm, d), lambda i, j: (i, 0)),
                  pl.BlockSpec((1, d), lambda i, j: (0, 0)),
                  pl.BlockSpec((d, tn), lambda i, j: (0, j))],
        out_specs=pl.BlockSpec((tm, tn), lambda i, j: (i, j)),
        out_shape=jax.ShapeDtypeStruct((n, npj), f32),
        scratch_shapes=[pltpu.VMEM((tm, d), bf16)],
        compiler_params=_cparams(2),
        name="inproj",
    )(x2, g, w)


def _lane_expand(cols, width):
    L = cols[0].shape[0]
    n = len(cols)
    lane = lax.broadcasted_iota(i32, (L, n * width), 1)
    out = jnp.broadcast_to(cols[n - 1], (L, n * width))
    for j in range(n - 2, -1, -1):
        out = jnp.where(lane < (j + 1) * width, jnp.broadcast_to(cols[j], (L, n * width)), out)
    return out


def _ssd_body(z_ref, x_ref, b_ref, c_ref, dt_ref, cw_ref, cb_ref, dtb_ref, alog_ref,
              dskip_ref, ng_ref, y_ref, buf_ref, xs_ref, bt_ref, cs_ref, h_ref):
    L = SSD_CHUNK
    c_idx = pl.program_id(1)

    @pl.when(c_idx == 0)
    def _():
        buf_ref[0:8, :] = jnp.zeros((8, buf_ref.shape[1]), f32)
        h_ref[...] = jnp.zeros(h_ref.shape, f32)

    buf_ref[8:8 + L, 0:SSD_D_INNER] = x_ref[...]
    buf_ref[8:8 + L, SSD_D_INNER:SSD_D_INNER + SSD_GN] = b_ref[...]
    buf_ref[8:8 + L, SSD_D_INNER + SSD_GN:] = c_ref[...]

    cw = 512
    n_ch = buf_ref.shape[1]
    for c0 in range(0, n_ch, cw):
        acc = jnp.broadcast_to(cb_ref[:, c0:c0 + cw], (L, cw))
        for k in range(SSD_CONV):
            acc = acc + cw_ref[k:k + 1, c0:c0 + cw] * buf_ref[5 + k:5 + k + L, c0:c0 + cw]
        v = _silu(acc)
        if c0 < SSD_D_INNER:
            xs_ref[:, c0:c0 + cw] = v
        elif c0 < SSD_D_INNER + SSD_GN:
            bt_ref[c0 - SSD_D_INNER:c0 - SSD_D_INNER + cw, :] = v.T.astype(bf16)
        else:
            o = c0 - SSD_D_INNER - SSD_GN
            cs_ref[:, o:o + cw] = v.astype(bf16)
    buf_ref[0:8, :] = buf_ref[L:L + 8, :]

    dt = dt_ref[...] + dtb_ref[...]
    dt = jnp.maximum(dt, 0.0) + jnp.log(1.0 + jnp.exp(-jnp.abs(dt)))
    da = dt * -jnp.exp(alog_ref[...])
    r = lax.broadcasted_iota(i32, (L, L), 0)
    s = lax.broadcasted_iota(i32, (L, L), 1)
    tri = (s <= r).astype(f32)
    acs = jnp.dot(tri, da, preferred_element_type=f32,
                  precision=lax.Precision.HIGHEST)
    acs_t = acs.T
    causal = s <= r
    a_last = acs[L - 1:L, :]
    e_acs = jnp.exp(acs)
    e_rem = jnp.exp(a_last - acs)
    e_last = jnp.exp(a_last)
    lane_gw = lax.broadcasted_iota(i32, (L, SSD_GW), 1)

    for g in range(SSD_GROUPS):
        hs = [g * SSD_HPG + j for j in range(SSD_HPG)]
        xg = xs_ref[:, g * SSD_GW:(g + 1) * SSD_GW]
        bgt = bt_ref[g * SSD_STATE:(g + 1) * SSD_STATE, :]
        cg = cs_ref[:, g * SSD_STATE:(g + 1) * SSD_STATE]
        dt_x = _lane_expand([dt[:, h:h + 1] for h in hs], SSD_HEAD_DIM)
        xdt = xg * dt_x
        cbm = jnp.dot(cg, bgt, preferred_element_type=f32)
        y = jnp.zeros((L, SSD_GW), f32)
        for j, h in enumerate(hs):
            seg = jnp.where(causal, acs[:, h:h + 1] - acs_t[h:h + 1, :], NEG)
            w = (cbm * jnp.exp(seg)).astype(bf16)
            band = (lane_gw >= j * SSD_HEAD_DIM) & (lane_gw < (j + 1) * SSD_HEAD_DIM)
            xm = jnp.where(band, xdt, 0.0).astype(bf16)
            y = y + jnp.dot(w, xm, preferred_element_type=f32)
        hprev = h_ref[g]
        y_off = jnp.dot(cg, hprev.astype(bf16), preferred_element_type=f32)
        y = y + y_off * _lane_expand([e_acs[:, h:h + 1] for h in hs], SSD_HEAD_DIM)
        xw = (xdt * _lane_expand([e_rem[:, h:h + 1] for h in hs], SSD_HEAD_DIM)).astype(bf16)
        st = jnp.dot(bgt, xw, preferred_element_type=f32)
        dec = _lane_expand([e_last[:, h:h + 1] for h in hs], SSD_HEAD_DIM)
        h_ref[g] = hprev * dec + st
        y = y + xg * dskip_ref[:, g * SSD_GW:(g + 1) * SSD_GW]
        y = y * _silu(z_ref[:, g * SSD_GW:(g + 1) * SSD_GW])
        y = y * lax.rsqrt(jnp.mean(y * y, axis=-1, keepdims=True) + EPS)
        y_ref[:, g * SSD_GW:(g + 1) * SSD_GW] = (y * ng_ref[:, g * SSD_GW:(g + 1) * SSD_GW]).astype(bf16)


def _ssd(proj, bsz, t_len, conv_w, conv_b, dt_bias, a_log, dskip_x, norm_g):
    L = SSD_CHUNK
    nc = t_len // L
    n = bsz * t_len
    nch = SSD_D_INNER + 2 * SSD_GN
    row = lambda b, c: b * nc + c
    full = lambda shape: pl.BlockSpec(shape, lambda b, c: (0,) * len(shape))
    return pl.pallas_call(
        _ssd_body,
        grid=(bsz, nc),
        in_specs=[pl.BlockSpec((L, SSD_D_INNER), lambda b, c: (row(b, c), COL_Z // SSD_D_INNER)),
                  pl.BlockSpec((L, SSD_D_INNER), lambda b, c: (row(b, c), COL_X // SSD_D_INNER)),
                  pl.BlockSpec((L, SSD_GN), lambda b, c: (row(b, c), COL_B // SSD_GN)),
                  pl.BlockSpec((L, SSD_GN), lambda b, c: (row(b, c), COL_C // SSD_GN)),
                  pl.BlockSpec((L, LANE), lambda b, c: (row(b, c), COL_DT // LANE)),
                  full((SSD_CONV, nch)), full((1, nch)), full((1, LANE)),
                  full((1, LANE)), full((1, SSD_D_INNER)), full((1, SSD_D_INNER))],
        out_specs=pl.BlockSpec((L, SSD_D_INNER), lambda b, c: (row(b, c), 0)),
        out_shape=jax.ShapeDtypeStruct((n, SSD_D_INNER), bf16),
        scratch_shapes=[pltpu.VMEM((L + 8, nch), f32),
                        pltpu.VMEM((L, SSD_D_INNER), f32),
                        pltpu.VMEM((SSD_GN, L), bf16),
                        pltpu.VMEM((L, SSD_GN), bf16),
                        pltpu.VMEM((SSD_GROUPS, SSD_STATE, SSD_GW), f32)],
        compiler_params=_cparams(2),
        name="ssd",
    )(proj, proj, proj, proj, proj, conv_w, conv_b, dt_bias, a_log, dskip_x, norm_g)


def _cmp_body(u_ref, pe_ref, w1_ref, w2_ref, g_ref, o_ref):
    nchunk = u_ref.shape[0]
    hid_a = jnp.zeros((nchunk, CMP_HIDDEN), f32)
    hid_b = jnp.zeros((nchunk, CMP_HIDDEN), f32)
    for s in range(CMP_STRIDE):
        us = u_ref[:, s, :]
        hid_a = hid_a + jnp.dot((us + pe_ref[0, s:s + 1, :]).astype(bf16), w1_ref[0, s],
                                preferred_element_type=f32)
        hid_b = hid_b + jnp.dot((us + pe_ref[0, CMP_STRIDE + s:CMP_STRIDE + s + 1, :]).astype(bf16),
                                w1_ref[0, CMP_STRIDE + s], preferred_element_type=f32)
    hid = hid_a + pltpu.roll(hid_b, nchunk - 1, 0)
    out = jnp.dot(_silu(hid).astype(bf16), w2_ref[0], preferred_element_type=f32)
    normed = out * lax.rsqrt(jnp.mean(out * out, axis=-1, keepdims=True) + EPS) * g_ref[...]
    out = jnp.where(pl.program_id(1) == 0, normed, out)
    rowi = lax.broadcasted_iota(i32, out.shape, 0)
    o_ref[0, 0, 0] = jnp.where(rowi == nchunk - 1, 0.0, out).astype(bf16)


def _compress(proj, bsz, t_len, pe, w1, w2, kg):
    nchunk = t_len // CMP_STRIDE
    u3 = proj.reshape(bsz * nchunk, CMP_STRIDE, proj.shape[1])
    return pl.pallas_call(
        _cmp_body,
        grid=(bsz, 2, NSA_KV_HEADS),
        in_specs=[pl.BlockSpec((nchunk, CMP_STRIDE, DH),
                               lambda b, kv, h: (b, 0, COL_KV // DH + kv * NSA_KV_HEADS + h)),
                  pl.BlockSpec((1, CMP_BLOCK, DH), lambda b, kv, h: (kv, 0, 0)),
                  pl.BlockSpec((1, CMP_BLOCK, DH, CMP_HIDDEN), lambda b, kv, h: (kv, 0, 0, 0)),
                  pl.BlockSpec((1, CMP_HIDDEN, DH), lambda b, kv, h: (kv, 0, 0)),
                  pl.BlockSpec((1, DH), lambda b, kv, h: (0, 0))],
        out_specs=pl.BlockSpec((1, 1, 1, nchunk, DH), lambda b, kv, h: (b, kv, h, 0, 0)),
        out_shape=jax.ShapeDtypeStruct((bsz, 2, NSA_KV_HEADS, nchunk, DH), bf16),
        compiler_params=_cparams(3),
        name="nsa_compress",
    )(u3, pe, w1, w2, kg)


def _qk_prep_body(q_ref, s_ref, w_ref, qg_ref, kg_ref, qo_ref, kvo_ref):
    scale = DH ** -0.5

    def hnorm(v, g):
        return v * lax.rsqrt(jnp.mean(v * v, axis=-1, keepdims=True) + EPS) * g

    for h in range(NSA_HEADS):
        sl = slice(h * DH, (h + 1) * DH)
        qo_ref[:, sl] = (hnorm(q_ref[:, sl], qg_ref[...]) * scale).astype(bf16)
    for h in range(NSA_KV_HEADS):
        sl = slice(h * DH, (h + 1) * DH)
        sv = slice(NSA_KV_WIDTH + h * DH, NSA_KV_WIDTH + (h + 1) * DH)
        kvo_ref[:, sl] = hnorm(s_ref[:, sl], kg_ref[1:2, :]).astype(bf16)
        kvo_ref[:, sv] = s_ref[:, sv].astype(bf16)
        kvo_ref[:, 2 * NSA_KV_WIDTH + h * DH:2 * NSA_KV_WIDTH + (h + 1) * DH] = (
            hnorm(w_ref[:, sl], kg_ref[2:3, :]).astype(bf16))
        kvo_ref[:, 3 * NSA_KV_WIDTH + h * DH:3 * NSA_KV_WIDTH + (h + 1) * DH] = w_ref[:, sv].astype(bf16)


def _qk_prep(proj, qg, kg, tm=512):
    n = proj.shape[0]
    w2 = 2 * NSA_KV_WIDTH
    return pl.pallas_call(
        _qk_prep_body,
        grid=(n // tm,),
        in_specs=[pl.BlockSpec((tm, NSA_WIDTH), lambda i: (i, COL_Q // NSA_WIDTH)),
                  pl.BlockSpec((tm, w2), lambda i: (i, (COL_KV + w2) // w2)),
                  pl.BlockSpec((tm, w2), lambda i: (i, (COL_KV + 2 * w2) // w2)),
                  pl.BlockSpec((1, DH), lambda i: (0, 0)),
                  pl.BlockSpec((3, DH), lambda i: (0, 0))],
        out_specs=[pl.BlockSpec((tm, NSA_WIDTH), lambda i: (i, 0)),
                   pl.BlockSpec((tm, 2 * w2), lambda i: (i, 0))],
        out_shape=[jax.ShapeDtypeStruct((n, NSA_WIDTH), bf16),
                   jax.ShapeDtypeStruct((n, 2 * w2), bf16)],
        compiler_params=_cparams(1),
        name="nsa_qk_prep",
    )(proj, proj, proj, qg, kg)


def _nsa_body(q_ref, kc_ref, vc_ref, ks_ref, vs_ref, kw_ref, vw_ref, gt_ref, o_ref,
              m_ref, l_ref, acc_ref, *, tq):
    i = pl.program_id(2)
    G = NSA_GQA
    R = G * tq
    n_cmp_pad = kc_ref.shape[3]
    n_slc = ks_ref.shape[0] // SLC_BLOCK
    nt = (((1,), (1,)), ((), ()))

    qv = q_ref[...]
    qs = jnp.concatenate([qv[:, g * DH:(g + 1) * DH] for g in range(G)], axis=0)
    t_q = i * tq + lax.broadcasted_iota(i32, (tq, 1), 0)
    t_r = jnp.concatenate([t_q] * G, axis=0)

    s_c = lax.dot_general(qs, kc_ref[0, 0, 0], nt, preferred_element_type=f32)
    cend = lax.broadcasted_iota(i32, (1, n_cmp_pad), 1) * CMP_STRIDE + (CMP_BLOCK - 1)
    m_c = cend <= t_r
    s_c = jnp.where(m_c, s_c, NEG)
    p_c = jnp.where(m_c, jnp.exp(s_c - jnp.max(s_c, axis=-1, keepdims=True)), 0.0)
    p_c = p_c / jnp.maximum(jnp.sum(p_c, axis=-1, keepdims=True), 1e-30)
    o_c = jnp.dot(p_c.astype(bf16), vc_ref[0, 0, 0], preferred_element_type=f32)

    imp = p_c[0:tq]
    for g in range(1, G):
        imp = imp + p_c[g * tq:(g + 1) * tq]
    per = SLC_BLOCK // CMP_STRIDE
    ci = lax.broadcasted_iota(i32, (n_cmp_pad, n_slc), 0)
    ni = lax.broadcasted_iota(i32, (n_cmp_pad, n_slc), 1)
    fold = ((ci // per == ni).astype(f32) + ((ci + 1) // per == ni).astype(f32))
    imp_b = jnp.dot(imp, fold, preferred_element_type=f32, precision=lax.Precision.HIGHEST)
    jb = lax.broadcasted_iota(i32, (tq, n_slc), 1)
    cur = t_q // SLC_BLOCK
    forced = (jb == 0) | (jb == cur) | (jb == cur - 1)
    val = jnp.where(forced, 1e30, jnp.where(jb <= cur, imp_b, -1.0))
    sel = jnp.zeros((tq, n_slc), f32)
    for _ in range(SLC_TOPK):
        mx = jnp.max(val, axis=-1, keepdims=True)
        first = jnp.min(jnp.where(val == mx, jb, n_slc), axis=-1, keepdims=True)
        hit = (jb == first) & (mx >= 0.0)
        sel = jnp.where(hit, 1.0, sel)
        val = jnp.where(jb == first, -1.0, val)
    sel_b = sel.astype(bf16)

    def online(k, v, bias, slot):
        s_ = lax.dot_general(qs, k, nt, preferred_element_type=f32)
        s_ = s_ + jnp.concatenate([bias] * G, axis=0)
        m_old = m_ref[slot]
        m_new = jnp.maximum(m_old, jnp.max(s_, axis=-1, keepdims=True))
        alpha = jnp.exp(m_old - m_new)
        p = jnp.exp(s_ - m_new)
        l_ref[slot] = alpha * l_ref[slot] + jnp.sum(p, axis=-1, keepdims=True)
        acc_ref[slot] = alpha * acc_ref[slot] + jnp.dot(p.astype(bf16), v, preferred_element_type=f32)
        m_ref[slot] = m_new

    m_ref[...] = jnp.full(m_ref.shape, NEG, f32)
    l_ref[...] = jnp.zeros(l_ref.shape, f32)
    acc_ref[...] = jnp.zeros(acc_ref.shape, f32)
    kcol = lax.broadcasted_iota(i32, (1, tq), 1)
    blk_r = lax.broadcasted_iota(i32, (n_slc, tq), 0)
    blk_c = lax.broadcasted_iota(i32, (n_slc, tq), 1)

    def slc_step(kt, carry):
        k0 = pl.multiple_of(kt * tq, tq)
        key = k0 + kcol
        expand = (blk_r == (k0 + blk_c) // SLC_BLOCK).astype(bf16)
        picked = jnp.dot(sel_b, expand, preferred_element_type=f32)
        ok = (picked > 0.5) & (key <= t_q)
        online(ks_ref[pl.ds(k0, tq), :], vs_ref[pl.ds(k0, tq), :], jnp.where(ok, 0.0, NEG), 0)
        return carry

    lax.fori_loop(0, i + 1, slc_step, 0)

    def win_step(kt, carry):
        k0 = pl.multiple_of(kt * tq, tq)
        key = k0 + kcol
        ok = (key <= t_q) & (key > t_q - WINDOW)
        online(kw_ref[pl.ds(k0, tq), :], vw_ref[pl.ds(k0, tq), :], jnp.where(ok, 0.0, NEG), 1)
        return carry

    lax.fori_loop(jnp.maximum(i - WINDOW // tq, 0), i + 1, win_step, 0)

    o_s = acc_ref[0] / l_ref[0]
    o_w = acc_ref[1] / l_ref[1]
    gts = 1.0 / (1.0 + jnp.exp(-gt_ref[0]))
    for g in range(G):
        rs = slice(g * tq, (g + 1) * tq)
        out = (gts[:, 3 * g:3 * g + 1] * o_c[rs] + gts[:, 3 * g + 1:3 * g + 2] * o_s[rs]
               + gts[:, 3 * g + 2:3 * g + 3] * o_w[rs])
        o_ref[:, g * DH:(g + 1) * DH] = out.astype(bf16)


def _nsa_attention(qn, kvc, kvs, gates, bsz, t_len, tq=128):
    n = bsz * t_len
    nq = t_len // tq
    nchunk = t_len // CMP_STRIDE
    R = NSA_GQA * tq
    H = NSA_KV_HEADS
    seq = lambda off: pl.BlockSpec((t_len, DH), lambda b, h, i: (b, off + h))
    return pl.pallas_call(
        functools.partial(_nsa_body, tq=tq),
        grid=(bsz, H, nq),
        in_specs=[pl.BlockSpec((tq, NSA_GQA * DH), lambda b, h, i: (b * nq + i, h)),
                  pl.BlockSpec((1, 1, 1, nchunk, DH), lambda b, h, i: (b, 0, h, 0, 0)),
                  pl.BlockSpec((1, 1, 1, nchunk, DH), lambda b, h, i: (b, 1, h, 0, 0)),
                  seq(0), seq(H), seq(2 * H), seq(3 * H),
                  pl.BlockSpec((1, tq, 3 * NSA_GQA), lambda b, h, i: (h, b * nq + i, 0))],
        out_specs=pl.BlockSpec((tq, NSA_GQA * DH), lambda b, h, i: (b * nq + i, h)),
        out_shape=jax.ShapeDtypeStruct((n, NSA_WIDTH), bf16),
        scratch_shapes=[pltpu.VMEM((2, R, 1), f32), pltpu.VMEM((2, R, 1), f32),
                        pltpu.VMEM((2, R, DH), f32)],
        compiler_params=_cparams(3),
        name="nsa_attention",
    )(qn, kvc, kvc, kvs, kvs, kvs, kvs, gates)


def _outproj_body(ya_ref, yb_ref, wa_ref, wb_ref, x_ref, g_ref, wr_ref, br_ref,
                  x1_ref, h_ref, ti_ref, tw_ref, row_ref, *, tn):
    j = pl.program_id(1)
    nj = pl.num_programs(1)
    acc = x_ref[...] + jnp.dot(ya_ref[...], wa_ref[...], preferred_element_type=f32)
    acc = acc + jnp.dot(yb_ref[...], wb_ref[...], preferred_element_type=f32)
    x1_ref[...] = acc
    row_ref[j] = acc

    @pl.when(j == nj - 1)
    def _():
        tm = x1_ref.shape[0]
        ssq = jnp.zeros((tm, 1), f32)
        for jj in range(row_ref.shape[0]):
            v = row_ref[jj]
            ssq = ssq + jnp.sum(v * v, axis=-1, keepdims=True)
        rinv = lax.rsqrt(ssq / (row_ref.shape[0] * tn) + EPS)
        logits = jnp.broadcast_to(br_ref[...], (tm, LANE))
        for jj in range(row_ref.shape[0]):
            hh = row_ref[jj] * rinv * g_ref[:, jj * tn:(jj + 1) * tn]
            h_ref[:, jj * tn:(jj + 1) * tn] = hh
            logits = logits + jnp.dot(hh, wr_ref[jj * tn:(jj + 1) * tn, :],
                                      preferred_element_type=f32, precision=lax.Precision.HIGHEST)
        lane = lax.broadcasted_iota(i32, (tm, LANE), 1)
        val = jnp.where(lane < N_EXPERTS, logits, -jnp.inf)
        idxs = jnp.zeros((tm, LANE), i32)
        vals = jnp.full((tm, LANE), -jnp.inf, f32)
        for k in range(TOP_K):
            mx = jnp.max(val, axis=-1, keepdims=True)
            first = jnp.min(jnp.where(val == mx, lane, LANE), axis=-1, keepdims=True)
            idxs = jnp.where(lane == k, first, idxs)
            vals = jnp.where(lane == k, mx, vals)
            val = jnp.where(lane == first, -jnp.inf, val)
        e = jnp.exp(vals - jnp.max(vals, axis=-1, keepdims=True))
        tw_ref[...] = e / jnp.sum(e, axis=-1, keepdims=True)
        ti_ref[...] = idxs


def _outproj_router(ya, yb, wa, wb, x2, g2, wr, br, tm=256, tn=512):
    n, d = x2.shape
    nj = d // tn
    ka = ya.shape[1]
    kb = yb.shape[1]
    return pl.pallas_call(
        functools.partial(_outproj_body, tn=tn),
        grid=(n // tm, nj),
        in_specs=[pl.BlockSpec((tm, ka), lambda i, j: (i, 0)),
                  pl.BlockSpec((tm, kb), lambda i, j: (i, 0)),
                  pl.BlockSpec((ka, tn), lambda i, j: (0, j)),
                  pl.BlockSpec((kb, tn), lambda i, j: (0, j)),
                  pl.BlockSpec((tm, tn), lambda i, j: (i, j)),
                  pl.BlockSpec((1, d), lambda i, j: (0, 0)),
                  pl.BlockSpec((d, LANE), lambda i, j: (0, 0)),
                  pl.BlockSpec((1, LANE), lambda i, j: (0, 0))],
        out_specs=[pl.BlockSpec((tm, tn), lambda i, j: (i, j)),
                   pl.BlockSpec((tm, d), lambda i, j: (i, 0)),
                   pl.BlockSpec((tm, LANE), lambda i, j: (i, 0)),
                   pl.BlockSpec((tm, LANE), lambda i, j: (i, 0))],
        out_shape=[jax.ShapeDtypeStruct((n, d), f32),
                   jax.ShapeDtypeStruct((n, d), f32),
                   jax.ShapeDtypeStruct((n, LANE), i32),
                   jax.ShapeDtypeStruct((n, LANE), f32)],
        scratch_shapes=[pltpu.VMEM((nj, tm, tn), f32)],
        compiler_params=_cparams(2),
        name="outproj_router",
    )(ya, yb, wa, wb, x2, g2, wr, br)


MOE_TM = 256


def _row_gather(idx_ref, src_ref, dst_of, sem, count):
    def copy(r):
        return pltpu.make_async_copy(src_ref.at[pl.ds(idx_ref[0, 0, r], 1), :], dst_of(r), sem)

    def start(r, c):
        copy(r).start()
        return c

    def wait(r, c):
        copy(r).wait()
        return c

    lax.fori_loop(0, count, start, 0)
    lax.fori_loop(0, count, wait, 0)


def _gather_body(tok_ref, src_ref, o_ref, buf_ref, sem):
    tmr = o_ref.shape[0]
    _row_gather(tok_ref, src_ref, lambda r: buf_ref.at[pl.ds(r, 1), :], sem, tmr)
    o_ref[...] = buf_ref[...].astype(o_ref.dtype)


def _gather_tokens(src, idx, tmr):
    n_rows = idx.shape[0]
    d = src.shape[1]
    nt = n_rows // tmr
    return pl.pallas_call(
        _gather_body,
        grid=(nt,),
        in_specs=[pl.BlockSpec((1, 1, tmr), lambda i: (i, 0, 0), memory_space=pltpu.SMEM),
                  pl.BlockSpec(memory_space=pl.ANY)],
        out_specs=pl.BlockSpec((tmr, d), lambda i: (i, 0)),
        out_shape=jax.ShapeDtypeStruct((n_rows, d), bf16),
        scratch_shapes=[pltpu.VMEM((tmr, d), src.dtype), pltpu.SemaphoreType.DMA(())],
        compiler_params=_cparams(1),
        name="moe_gather",
    )(idx.reshape(nt, 1, tmr), src)


def _up_body(be_ref, x_ref, wg_ref, wl_ref, bg_ref, bl_ref, o_ref, wgs_ref, wls_ref):
    i = pl.program_id(1)
    e = be_ref[i]
    prev = be_ref[jnp.maximum(i - 1, 0)]

    @pl.when((i == 0) | (e != prev))
    def _():
        wgs_ref[...] = wg_ref[0].astype(bf16)
        wls_ref[...] = wl_ref[0].astype(bf16)

    x = x_ref[...]
    glu = jnp.dot(x, wgs_ref[...], preferred_element_type=f32) + bg_ref[0]
    lin = jnp.dot(x, wls_ref[...], preferred_element_type=f32) + bl_ref[0]
    glu = jnp.minimum(glu, SWIGLU_LIMIT)
    lin = jnp.clip(lin, -SWIGLU_LIMIT, SWIGLU_LIMIT)
    act = glu * (1.0 / (1.0 + jnp.exp(-SWIGLU_ALPHA * glu))) * (lin + 1.0)
    o_ref[...] = act.astype(bf16)


def _moe_up(blk_e, xs, w_up, b_up3, tn=384):
    n_rows, d = xs.shape
    nb = n_rows // MOE_TM
    nj = D_EXPERT // tn
    gs = pltpu.PrefetchScalarGridSpec(
        num_scalar_prefetch=1,
        grid=(nj, nb),
        in_specs=[pl.BlockSpec((MOE_TM, d), lambda j, i, be: (i, 0)),
                  pl.BlockSpec((1, d, tn), lambda j, i, be: (be[i], 0, j)),
                  pl.BlockSpec((1, d, tn), lambda j, i, be: (be[i], 0, nj + j)),
                  pl.BlockSpec((1, 1, tn), lambda j, i, be: (be[i], 0, j)),
                  pl.BlockSpec((1, 1, tn), lambda j, i, be: (be[i], 0, nj + j))],
        out_specs=pl.BlockSpec((MOE_TM, tn), lambda j, i, be: (i, j)),
        scratch_shapes=[pltpu.VMEM((d, tn), bf16), pltpu.VMEM((d, tn), bf16)])
    return pl.pallas_call(
        _up_body, grid_spec=gs,
        out_shape=jax.ShapeDtypeStruct((n_rows, D_EXPERT), bf16),
        compiler_params=_cparams(2),
        name="moe_up",
    )(blk_e, xs, w_up, w_up, b_up3, b_up3)


def _down_body(be_ref, a_ref, w_ref, b_ref, rw_ref, o_ref, ws_ref):
    i = pl.program_id(1)
    e = be_ref[i]
    prev = be_ref[jnp.maximum(i - 1, 0)]

    @pl.when((i == 0) | (e != prev))
    def _():
        ws_ref[...] = w_ref[0].astype(bf16)

    y = jnp.dot(a_ref[...], ws_ref[...], preferred_element_type=f32) + b_ref[0]
    o_ref[...] = y * rw_ref[...]


def _moe_down(blk_e, act, w_down, b_down3, row_w, tn=1024):
    n_rows, de = act.shape
    d = w_down.shape[2]
    nb = n_rows // MOE_TM
    nj = d // tn
    gs = pltpu.PrefetchScalarGridSpec(
        num_scalar_prefetch=1,
        grid=(nj, nb),
        in_specs=[pl.BlockSpec((MOE_TM, de), lambda j, i, be: (i, 0)),
                  pl.BlockSpec((1, de, tn), lambda j, i, be: (be[i], 0, j)),
                  pl.BlockSpec((1, 1, tn), lambda j, i, be: (be[i], 0, j)),
                  pl.BlockSpec((MOE_TM, 1), lambda j, i, be: (i, 0))],
        out_specs=pl.BlockSpec((MOE_TM, tn), lambda j, i, be: (i, j)),
        scratch_shapes=[pltpu.VMEM((de, tn), bf16)])
    return pl.pallas_call(
        _down_body, grid_spec=gs,
        out_shape=jax.ShapeDtypeStruct((n_rows, d), f32),
        compiler_params=_cparams(2),
        name="moe_down",
    )(blk_e, act, w_down, b_down3, row_w)


def _combine_body(pos_ref, ys_ref, x_ref, o_ref, buf_ref, sem):
    tm = o_ref.shape[0]
    _row_gather(pos_ref, ys_ref,
                lambda a: buf_ref.at[a % TOP_K, pl.ds(a // TOP_K, 1), :], sem, tm * TOP_K)
    acc = x_ref[...]
    for k in range(TOP_K):
        acc = acc + buf_ref[k]
    o_ref[...] = acc


def _combine(ys, pos, x1, tm=128):
    n, d = x1.shape
    nt = n // tm
    return pl.pallas_call(
        _combine_body,
        grid=(nt,),
        in_specs=[pl.BlockSpec((1, 1, tm * TOP_K), lambda i: (i, 0, 0), memory_space=pltpu.SMEM),
                  pl.BlockSpec(memory_space=pl.ANY),
                  pl.BlockSpec((tm, d), lambda i: (i, 0))],
        out_specs=pl.BlockSpec((tm, d), lambda i: (i, 0)),
        out_shape=jax.ShapeDtypeStruct((n, d), f32),
        scratch_shapes=[pltpu.VMEM((TOP_K, tm, d), f32), pltpu.SemaphoreType.DMA(())],
        compiler_params=_cparams(1),
        name="moe_combine",
    )(pos.reshape(nt, 1, tm * TOP_K), ys, x1)


def _route(top_idx, top_w):
    n = top_idx.shape[0]
    n_assign = n * TOP_K
    e_flat = top_idx.reshape(-1)
    w_flat = top_w.reshape(-1)
    order = jnp.argsort(e_flat)
    e_sorted = e_flat[order]
    counts = jnp.zeros((N_EXPERTS,), i32).at[e_flat].add(1)
    padded = (counts + MOE_TM - 1) // MOE_TM * MOE_TM
    start = jnp.cumsum(counts) - counts
    pend = jnp.cumsum(padded)
    pstart = pend - padded
    dest = pstart[e_sorted] + (jnp.arange(n_assign, dtype=i32) - start[e_sorted])
    n_blocks = -(-n_assign // MOE_TM) + N_EXPERTS
    n_rows = n_blocks * MOE_TM
    row_tok = jnp.zeros((n_rows,), i32).at[dest].set((order // TOP_K).astype(i32))
    row_w = jnp.zeros((n_rows,), f32).at[dest].set(w_flat[order])
    blk_e = jnp.minimum(jnp.searchsorted(pend, jnp.arange(n_blocks, dtype=i32) * MOE_TM, side='right'),
                        N_EXPERTS - 1).astype(i32)
    pos = jnp.zeros((n_assign,), i32).at[order].set(dest.astype(i32))
    return row_tok, row_w, blk_e, pos


def _layer(x, ln1_g, w_in, conv_w, conv_b, dt_bias, a_log, d_skip, ssd_norm_g, q_norm_g,
           k_norm_g, cmp_pe, cmp_w1, cmp_w2, w_out, ln2_g, w_router, b_router, w_up, b_up,
           w_down, b_down):
    bsz, t_len, d = x.shape
    n = bsz * t_len
    x2 = x.reshape(n, d)

    o_dt = SSD_D_INNER + SSD_D_INNER + 2 * SSD_GN
    o_q = o_dt + SSD_HEADS
    o_kv = o_q + NSA_WIDTH
    o_g = o_kv + 6 * NSA_KV_WIDTH
    n_in = o_g + 3 * NSA_HEADS
    w_perm = jnp.concatenate(
        [w_in[:, :o_dt], w_in[:, o_q:o_g], w_in[:, o_dt:o_q], w_in[:, o_g:n_in],
         jnp.zeros((d, NP_PROJ - n_in), w_in.dtype)], axis=1).astype(bf16)

    proj = _inproj(x2, ln1_g.reshape(1, d), w_perm)

    lane_pad = lambda v: jnp.concatenate([v, jnp.zeros((LANE - v.shape[0],), f32)]).reshape(1, LANE)
    y_ssd = _ssd(proj, bsz, t_len, conv_w, conv_b.reshape(1, -1), lane_pad(dt_bias),
                 lane_pad(a_log), jnp.repeat(d_skip, SSD_HEAD_DIM).reshape(1, -1),
                 ssd_norm_g.reshape(1, -1))

    kvc = _compress(proj, bsz, t_len, cmp_pe, cmp_w1.astype(bf16), cmp_w2.astype(bf16),
                    k_norm_g[0:1])
    qn, kvs = _qk_prep(proj, q_norm_g.reshape(1, DH), k_norm_g)
    gates = proj[:, COL_DT + GATE_OFF:COL_DT + GATE_OFF + 3 * NSA_HEADS]
    gates = gates.reshape(n, NSA_KV_HEADS, 3 * NSA_GQA).transpose(1, 0, 2)
    y_nsa = _nsa_attention(qn, kvc, kvs, gates, bsz, t_len)

    wo = w_out.astype(bf16)
    wr = jnp.concatenate([w_router, jnp.zeros((d, LANE - N_EXPERTS), f32)], axis=1)
    br = jnp.concatenate([b_router, jnp.zeros((LANE - N_EXPERTS,), f32)]).reshape(1, LANE)
    x1, h2, ti, tw = _outproj_router(y_ssd, y_nsa, wo[:SSD_D_INNER], wo[SSD_D_INNER:], x2,
                                     ln2_g.reshape(1, d), wr, br)

    row_tok, row_w, blk_e, pos = _route(ti[:, :TOP_K], tw[:, :TOP_K])
    xs = _gather_tokens(h2, row_tok, MOE_TM)
    act = _moe_up(blk_e, xs, w_up, b_up.reshape(N_EXPERTS, 1, -1))
    ys = _moe_down(blk_e, act, w_down, b_down.reshape(N_EXPERTS, 1, -1), row_w.reshape(-1, 1))
    out = _combine(ys, pos, x1)
    return out.reshape(bsz, t_len, d)


def kernel(x, ln1_g, w_in, conv_w, conv_b, dt_bias, a_log, d_skip, ssd_norm_g, q_norm_g,
           k_norm_g, cmp_pe, cmp_w1, cmp_w2, w_out, ln2_g, w_router, b_router, w_up, b_up,
           w_down, b_down):
    return _layer(x, ln1_g[0], w_in[0], conv_w[0], conv_b[0], dt_bias[0], a_log[0], d_skip[0],
                  ssd_norm_g[0], q_norm_g[0], k_norm_g[0], cmp_pe[0], cmp_w1[0], cmp_w2[0],
                  w_out[0], ln2_g[0], w_router[0], b_router[0], w_up[0], b_up[0], w_down[0],
                  b_down[0])
```

```python
import functools
import math

import jax
import jax.numpy as jnp
from jax import lax
from jax.experimental import pallas as pl
from jax.experimental.pallas import tpu as pltpu

f32 = jnp.float32
bf16 = jnp.bfloat16
i32 = jnp.int32

D_MODEL = 4096
SSD_D_INNER = 2048
SSD_HEAD_DIM = 64
SSD_HEADS = 32
SSD_GROUPS = 8
SSD_HPG = 4
SSD_STATE = 128
SSD_CONV = 4
SSD_CHUNK = 256
SSD_GN = SSD_GROUPS * SSD_STATE
SSD_GW = SSD_HPG * SSD_HEAD_DIM
NSA_HEADS = 16
NSA_KV_HEADS = 4
NSA_GQA = 4
DH = 128
NSA_WIDTH = 2048
NSA_KV_WIDTH = 512
CMP_BLOCK = 32
CMP_STRIDE = 16
CMP_HIDDEN = 256
SLC_BLOCK = 64
SLC_TOPK = 16
WINDOW = 512
N_EXPERTS = 32
TOP_K = 4
D_EXPERT = 1536
SWIGLU_LIMIT = 7.0
SWIGLU_ALPHA = 1.702
EPS = 1e-5
NEG = -1e30

COL_Z = 0
COL_X = 2048
COL_B = 4096
COL_C = 5120
COL_Q = 6144
COL_KV = 8192
COL_DT = 11264
GATE_OFF = 32
NP_PROJ = 11520

LANE = 128
VMEM_LIMIT = 56 * 1024 * 1024


def _cparams(n_axes):
    return pltpu.CompilerParams(dimension_semantics=("arbitrary",) * n_axes,
                                vmem_limit_bytes=VMEM_LIMIT)


def _silu(v):
    return v * (1.0 / (1.0 + jnp.exp(-v)))


def _inproj_body(x_ref, g_ref, w_ref, o_ref, h_ref):
    @pl.when(pl.program_id(1) == 0)
    def _():
        x = x_ref[...]
        ms = jnp.mean(x * x, axis=-1, keepdims=True)
        h_ref[...] = (x * lax.rsqrt(ms + EPS) * g_ref[...]).astype(bf16)

    o_ref[...] = jnp.dot(h_ref[...], w_ref[...], preferred_element_type=f32)


def _inproj(x2, g, w, tm=512, tn=768):
    n, d = x2.shape
    npj = w.shape[1]
    return pl.pallas_call(
        _inproj_body,
        grid=(n // tm, npj // tn),
        in_specs=[pl.BlockSpec((tm, d), lambda i, j: (i, 0)),
                  pl.BlockSpec((1, d), lambda i, j: (0, 0)),
                  pl.BlockSpec((d, tn), lambda i, j: (0, j))],
        out_specs=pl.BlockSpec((tm, tn), lambda i, j: (i, j)),
        out_shape=jax.ShapeDtypeStruct((n, npj), f32),
        scratch_shapes=[pltpu.VMEM((tm, d), bf16)],
        compiler_params=_cparams(2),
        name="inproj",
    )(x2, g, w)


def _lane_expand(cols, width):
    L = cols[0].shape[0]
    n = len(cols)
    lane = lax.broadcasted_iota(i32, (L, n * width), 1)
    out = jnp.broadcast_to(cols[n - 1], (L, n * width))
    for j in range(n - 2, -1, -1):
        out = jnp.where(lane < (j + 1) * width, jnp.broadcast_to(cols[j], (L, n * width)), out)
    return out


def _ssd_body(z_ref, x_ref, b_ref, c_ref, dt_ref, cw_ref, cb_ref, dtb_ref, alog_ref,
              dskip_ref, ng_ref, y_ref, buf_ref, xs_ref, bt_ref, cs_ref, h_ref):
    L = SSD_CHUNK
    c_idx = pl.program_id(1)

    @pl.when(c_idx == 0)
    def _():
        buf_ref[0:8, :] = jnp.zeros((8, buf_ref.shape[1]), f32)
        h_ref[...] = jnp.zeros(h_ref.shape, f32)

    buf_ref[8:8 + L, 0:SSD_D_INNER] = x_ref[...]
    buf_ref[8:8 + L, SSD_D_INNER:SSD_D_INNER + SSD_GN] = b_ref[...]
    buf_ref[8:8 + L, SSD_D_INNER + SSD_GN:] = c_ref[...]

    cw = 512
    n_ch = buf_ref.shape[1]
    for c0 in range(0, n_ch, cw):
        acc = jnp.broadcast_to(cb_ref[:, c0:c0 + cw], (L, cw))
        for k in range(SSD_CONV):
            acc = acc + cw_ref[k:k + 1, c0:c0 + cw] * buf_ref[5 + k:5 + k + L, c0:c0 + cw]
        v = _silu(acc)
        if c0 < SSD_D_INNER:
            xs_ref[:, c0:c0 + cw] = v
        elif c0 < SSD_D_INNER + SSD_GN:
            bt_ref[c0 - SSD_D_INNER:c0 - SSD_D_INNER + cw, :] = v.T.astype(bf16)
        else:
            o = c0 - SSD_D_INNER - SSD_GN
            cs_ref[:, o:o + cw] = v.astype(bf16)
    buf_ref[0:8, :] = buf_ref[L:L + 8, :]

    dt = dt_ref[...] + dtb_ref[...]
    dt = jnp.maximum(dt, 0.0) + jnp.log(1.0 + jnp.exp(-jnp.abs(dt)))
    da = dt * -jnp.exp(alog_ref[...])
    r = lax.broadcasted_iota(i32, (L, L), 0)
    s = lax.broadcasted_iota(i32, (L, L), 1)
    tri = (s <= r).astype(f32)
    acs = jnp.dot(tri, da, preferred_element_type=f32,
                  precision=lax.Precision.HIGHEST)
    acs_t = acs.T
    causal = s <= r
    a_last = acs[L - 1:L, :]
    e_acs = jnp.exp(acs)
    e_rem = jnp.exp(a_last - acs)
    e_last = jnp.exp(a_last)
    lane_gw = lax.broadcasted_iota(i32, (L, SSD_GW), 1)

    for g in range(SSD_GROUPS):
        hs = [g * SSD_HPG + j for j in range(SSD_HPG)]
        xg = xs_ref[:, g * SSD_GW:(g + 1) * SSD_GW]
        bgt = bt_ref[g * SSD_STATE:(g + 1) * SSD_STATE, :]
        cg = cs_ref[:, g * SSD_STATE:(g + 1) * SSD_STATE]
        dt_x = _lane_expand([dt[:, h:h + 1] for h in hs], SSD_HEAD_DIM)
        xdt = xg * dt_x
        cbm = jnp.dot(cg, bgt, preferred_element_type=f32)
        y = jnp.zeros((L, SSD_GW), f32)
        for j, h in enumerate(hs):
            seg = jnp.where(causal, acs[:, h:h + 1] - acs_t[h:h + 1, :], NEG)
            w = (cbm * jnp.exp(seg)).astype(bf16)
            band = (lane_gw >= j * SSD_HEAD_DIM) & (lane_gw < (j + 1) * SSD_HEAD_DIM)
            xm = jnp.where(band, xdt, 0.0).astype(bf16)
            y = y + jnp.dot(w, xm, preferred_element_type=f32)
        hprev = h_ref[g]
        y_off = jnp.dot(cg, hprev.astype(bf16), preferred_element_type=f32)
        y = y + y_off * _lane_expand([e_acs[:, h:h + 1] for h in hs], SSD_HEAD_DIM)
        xw = (xdt * _lane_expand([e_rem[:, h:h + 1] for h in hs], SSD_HEAD_DIM)).astype(bf16)
        st = jnp.dot(bgt, xw, preferred_element_type=f32)
        dec = _lane_expand([e_last[:, h:h + 1] for h in hs], SSD_HEAD_DIM)
        h_ref[g] = hprev * dec + st
        y = y + xg * dskip_ref[:, g * SSD_GW:(g + 1) * SSD_GW]
        y = y * _silu(z_ref[:, g * SSD_GW:(g + 1) * SSD_GW])
        y = y * lax.rsqrt(jnp.mean(y * y, axis=-1, keepdims=True) + EPS)
        y_ref[:, g * SSD_GW:(g + 1) * SSD_GW] = (y * ng_ref[:, g * SSD_GW:(g + 1) * SSD_GW]).astype(bf16)


def _ssd(proj, bsz, t_len, conv_w, conv_b, dt_bias, a_log, dskip_x, norm_g):
    L = SSD_CHUNK
    nc = t_len // L
    n = bsz * t_len
    nch = SSD_D_INNER + 2 * SSD_GN
    row = lambda b, c: b * nc + c
    full = lambda shape: pl.BlockSpec(shape, lambda b, c: (0,) * len(shape))
    return pl.pallas_call(
        _ssd_body,
        grid=(bsz, nc),
        in_specs=[pl.BlockSpec((L, SSD_D_INNER), lambda b, c: (row(b, c), COL_Z // SSD_D_INNER)),
                  pl.BlockSpec((L, SSD_D_INNER), lambda b, c: (row(b, c), COL_X // SSD_D_INNER)),
                  pl.BlockSpec((L, SSD_GN), lambda b, c: (row(b, c), COL_B // SSD_GN)),
                  pl.BlockSpec((L, SSD_GN), lambda b, c: (row(b, c), COL_C // SSD_GN)),
                  pl.BlockSpec((L, LANE), lambda b, c: (row(b, c), COL_DT // LANE)),
                  full((SSD_CONV, nch)), full((1, nch)), full((1, LANE)),
                  full((1, LANE)), full((1, SSD_D_INNER)), full((1, SSD_D_INNER))],
        out_specs=pl.BlockSpec((L, SSD_D_INNER), lambda b, c: (row(b, c), 0)),
        out_shape=jax.ShapeDtypeStruct((n, SSD_D_INNER), bf16),
        scratch_shapes=[pltpu.VMEM((L + 8, nch), f32),
                        pltpu.VMEM((L, SSD_D_INNER), f32),
                        pltpu.VMEM((SSD_GN, L), bf16),
                        pltpu.VMEM((L, SSD_GN), bf16),
                        pltpu.VMEM((SSD_GROUPS, SSD_STATE, SSD_GW), f32)],
        compiler_params=_cparams(2),
        name="ssd",
    )(proj, proj, proj, proj, proj, conv_w, conv_b, dt_bias, a_log, dskip_x, norm_g)


def _cmp_body(uk_ref, uv_ref, pe_ref, w1_ref, w2_ref, g_ref, kc_ref, vct_ref):
    nchunk = uk_ref.shape[0]

    def branch(u_ref, kv):
        hid_a = jnp.zeros((nchunk, CMP_HIDDEN), f32)
        hid_b = jnp.zeros((nchunk, CMP_HIDDEN), f32)
        ut = pltpu.einshape("csd->scd", u_ref[...])
        for s in range(CMP_STRIDE):
            us = ut[s]
            hid_a = hid_a + jnp.dot((us + pe_ref[kv, s:s + 1, :]).astype(bf16), w1_ref[kv, s],
                                    preferred_element_type=f32)
            hid_b = hid_b + jnp.dot(
                (us + pe_ref[kv, CMP_STRIDE + s:CMP_STRIDE + s + 1, :]).astype(bf16),
                w1_ref[kv, CMP_STRIDE + s], preferred_element_type=f32)
        hid = hid_a + pltpu.roll(hid_b, nchunk - 1, 0)
        out = jnp.dot(_silu(hid).astype(bf16), w2_ref[kv], preferred_element_type=f32)
        rowi = lax.broadcasted_iota(i32, out.shape, 0)
        return jnp.where(rowi == nchunk - 1, 0.0, out)

    k = branch(uk_ref, 0)
    k = k * lax.rsqrt(jnp.mean(k * k, axis=-1, keepdims=True) + EPS) * g_ref[...]
    kc_ref[0, 0] = k.astype(bf16)
    vct_ref[0, 0] = branch(uv_ref, 1).T.astype(bf16)


def _compress(proj, bsz, t_len, pe, w1, w2, kg):
    nchunk = t_len // CMP_STRIDE
    H = NSA_KV_HEADS
    u3 = proj.reshape(bsz * nchunk, CMP_STRIDE, proj.shape[1])
    useg = lambda kv: pl.BlockSpec((nchunk, CMP_STRIDE, DH),
                                   lambda b, h: (b, 0, COL_KV // DH + kv * H + h))
    full = lambda shape: pl.BlockSpec(shape, lambda b, h: (0,) * len(shape))
    return pl.pallas_call(
        _cmp_body,
        grid=(bsz, H),
        in_specs=[useg(0), useg(1), full((2, CMP_BLOCK, DH)), full((2, CMP_BLOCK, DH, CMP_HIDDEN)),
                  full((2, CMP_HIDDEN, DH)), full((1, DH))],
        out_specs=[pl.BlockSpec((1, 1, nchunk, DH), lambda b, h: (b, h, 0, 0)),
                   pl.BlockSpec((1, 1, DH, nchunk), lambda b, h: (b, h, 0, 0))],
        out_shape=[jax.ShapeDtypeStruct((bsz, H, nchunk, DH), bf16),
                   jax.ShapeDtypeStruct((bsz, H, DH, nchunk), bf16)],
        compiler_params=_cparams(2),
        name="nsa_compress",
    )(u3, u3, pe, w1, w2, kg)


def _qk_prep_body(q_ref, s_ref, w_ref, qg_ref, kg_ref, qt_ref, ks_ref, vst_ref, kw_ref, vwt_ref):
    scale = DH ** -0.5
    tq = q_ref.shape[0]

    def hnorm(v, g):
        return v * lax.rsqrt(jnp.mean(v * v, axis=-1, keepdims=True) + EPS) * g

    for h in range(NSA_KV_HEADS):
        for g in range(NSA_GQA):
            sl = slice((h * NSA_GQA + g) * DH, (h * NSA_GQA + g + 1) * DH)
            qn = hnorm(q_ref[:, sl], qg_ref[...]) * scale
            qt_ref[0, h, 0, :, g * tq:(g + 1) * tq] = qn.T.astype(bf16)
        sl = slice(h * DH, (h + 1) * DH)
        sv = slice(NSA_KV_WIDTH + h * DH, NSA_KV_WIDTH + (h + 1) * DH)
        ks_ref[:, sl] = hnorm(s_ref[:, sl], kg_ref[1:2, :]).astype(bf16)
        kw_ref[:, sl] = hnorm(w_ref[:, sl], kg_ref[2:3, :]).astype(bf16)
        vst_ref[0, h, 0] = s_ref[:, sv].T.astype(bf16)
        vwt_ref[0, h, 0] = w_ref[:, sv].T.astype(bf16)


def _qk_prep(proj, qg, kg, bsz, t_len, tq):
    n = proj.shape[0]
    nq = t_len // tq
    H = NSA_KV_HEADS
    w2 = 2 * NSA_KV_WIDTH
    row = lambda b, i: b * nq + i
    vt_spec = pl.BlockSpec((1, H, 1, DH, tq), lambda b, i: (b, 0, i, 0, 0))
    vt_shape = jax.ShapeDtypeStruct((bsz, H, nq, DH, tq), bf16)
    return pl.pallas_call(
        _qk_prep_body,
        grid=(bsz, nq),
        in_specs=[pl.BlockSpec((tq, NSA_WIDTH), lambda b, i: (row(b, i), COL_Q // NSA_WIDTH)),
                  pl.BlockSpec((tq, w2), lambda b, i: (row(b, i), (COL_KV + w2) // w2)),
                  pl.BlockSpec((tq, w2), lambda b, i: (row(b, i), (COL_KV + 2 * w2) // w2)),
                  pl.BlockSpec((1, DH), lambda b, i: (0, 0)),
                  pl.BlockSpec((3, DH), lambda b, i: (0, 0))],
        out_specs=[pl.BlockSpec((1, H, 1, DH, NSA_GQA * tq), lambda b, i: (b, 0, i, 0, 0)),
                   pl.BlockSpec((tq, NSA_KV_WIDTH), lambda b, i: (row(b, i), 0)),
                   vt_spec,
                   pl.BlockSpec((tq, NSA_KV_WIDTH), lambda b, i: (row(b, i), 0)),
                   vt_spec],
        out_shape=[jax.ShapeDtypeStruct((bsz, H, nq, DH, NSA_GQA * tq), bf16),
                   jax.ShapeDtypeStruct((n, NSA_KV_WIDTH), bf16), vt_shape,
                   jax.ShapeDtypeStruct((n, NSA_KV_WIDTH), bf16), vt_shape],
        compiler_params=_cparams(2),
        name="nsa_qk_prep",
    )(proj, proj, proj, qg, kg)


def _nsa_body(qt_ref, kc_ref, vct_ref, ks_ref, vst_ref, kw_ref, vwt_ref, gt_ref, o_ref,
              m_ref, l_ref, acc_ref, sel_ref, *, tq):
    i = pl.program_id(2)
    G = NSA_GQA
    n_cmp = kc_ref.shape[2]
    n_slc = ks_ref.shape[0] // SLC_BLOCK

    qt = qt_ref[0, 0, 0]
    t_row = i * tq + lax.broadcasted_iota(i32, (1, tq), 1)
    t_all = jnp.concatenate([t_row] * G, axis=1)

    s_c = jnp.dot(kc_ref[0, 0], qt, preferred_element_type=f32)
    cend = lax.broadcasted_iota(i32, (n_cmp, 1), 0) * CMP_STRIDE + (CMP_BLOCK - 1)
    m_c = cend <= t_all
    s_c = jnp.where(m_c, s_c, NEG)
    p_c = jnp.where(m_c, jnp.exp(s_c - jnp.max(s_c, axis=0, keepdims=True)), 0.0)
    p_c = p_c / jnp.maximum(jnp.sum(p_c, axis=0, keepdims=True), 1e-30)
    o_c = jnp.dot(vct_ref[0, 0], p_c.astype(bf16), preferred_element_type=f32)

    imp = p_c[:, 0:tq]
    for g in range(1, G):
        imp = imp + p_c[:, g * tq:(g + 1) * tq]
    per = SLC_BLOCK // CMP_STRIDE
    ni = lax.broadcasted_iota(i32, (n_slc, n_cmp), 0)
    ci = lax.broadcasted_iota(i32, (n_slc, n_cmp), 1)
    fold = ((ci // per == ni).astype(f32) + ((ci + 1) // per == ni).astype(f32))
    imp_b = jnp.dot(fold, imp, preferred_element_type=f32, precision=lax.Precision.HIGHEST)
    jb = lax.broadcasted_iota(i32, (n_slc, tq), 0)
    cur = t_row // SLC_BLOCK
    forced = (jb == 0) | (jb == cur) | (jb == cur - 1)
    val = jnp.where(forced, 1e30, jnp.where(jb <= cur, imp_b, -1.0))
    sel = jnp.zeros((n_slc, tq), f32)
    for _ in range(SLC_TOPK):
        mx = jnp.max(val, axis=0, keepdims=True)
        first = jnp.min(jnp.where(val == mx, jb, n_slc), axis=0, keepdims=True)
        hit = (jb == first) & (mx >= 0.0)
        sel = jnp.where(hit, 1.0, sel)
        val = jnp.where(jb == first, -1.0, val)

    def online(k, vt, bias, slot):
        s_ = jnp.dot(k, qt, preferred_element_type=f32)
        s_ = s_ + jnp.concatenate([bias] * G, axis=1)
        m_old = m_ref[slot]
        m_new = jnp.maximum(m_old, jnp.max(s_, axis=0, keepdims=True))
        alpha = jnp.exp(m_old - m_new)
        p = jnp.exp(s_ - m_new)
        l_ref[slot] = alpha * l_ref[slot] + jnp.sum(p, axis=0, keepdims=True)
        acc_ref[slot] = alpha * acc_ref[slot] + jnp.dot(vt, p.astype(bf16), preferred_element_type=f32)
        m_ref[slot] = m_new

    m_ref[...] = jnp.full(m_ref.shape, NEG, f32)
    l_ref[...] = jnp.zeros(l_ref.shape, f32)
    acc_ref[...] = jnp.zeros(acc_ref.shape, f32)
    krow = lax.broadcasted_iota(i32, (tq, 1), 0)
    kcol = lax.broadcasted_iota(i32, (1, tq), 1)

    sel_ref[...] = jnp.where(sel > 0.5, 0.0, NEG)
    bpt = tq // SLC_BLOCK

    def block_bias(kt):
        rows = [jnp.broadcast_to(sel_ref[pl.ds(kt * bpt + j, 1), :], (SLC_BLOCK, tq))
                for j in range(bpt)]
        return jnp.concatenate(rows, axis=0)

    def slc_step(kt, carry):
        k0 = pl.multiple_of(kt * tq, tq)
        online(ks_ref[pl.ds(k0, tq), :], vst_ref[0, 0, kt], block_bias(kt), 0)
        return carry

    lax.fori_loop(0, i, slc_step, 0)
    q0 = pl.multiple_of(i * tq, tq)
    online(ks_ref[pl.ds(q0, tq), :], vst_ref[0, 0, i],
           jnp.where(krow <= kcol, block_bias(i), NEG), 0)

    def win_step(kt, carry):
        k0 = pl.multiple_of(kt * tq, tq)
        key = k0 + krow
        ok = (key <= t_row) & (key > t_row - WINDOW)
        online(kw_ref[pl.ds(k0, tq), :], vwt_ref[0, 0, kt], jnp.where(ok, 0.0, NEG), 1)
        return carry

    lax.fori_loop(jnp.maximum(i - WINDOW // tq, 0), i + 1, win_step, 0)

    o_s = acc_ref[0] / l_ref[0]
    o_w = acc_ref[1] / l_ref[1]
    gts = 1.0 / (1.0 + jnp.exp(-gt_ref[0]))
    for g in range(G):
        cs = slice(g * tq, (g + 1) * tq)
        out_t = (gts[3 * g:3 * g + 1, :] * o_c[:, cs] + gts[3 * g + 1:3 * g + 2, :] * o_s[:, cs]
                 + gts[3 * g + 2:3 * g + 3, :] * o_w[:, cs])
        o_ref[:, g * DH:(g + 1) * DH] = out_t.T.astype(bf16)


def _nsa_attention(qt, kc, vct, ks, vst, kw, vwt, gates, bsz, t_len, tq):
    n = bsz * t_len
    nq = t_len // tq
    nchunk = t_len // CMP_STRIDE
    R = NSA_GQA * tq
    seq = pl.BlockSpec((t_len, DH), lambda b, h, i: (b, h))
    seq_t = pl.BlockSpec((1, 1, nq, DH, tq), lambda b, h, i: (b, h, 0, 0, 0))
    return pl.pallas_call(
        functools.partial(_nsa_body, tq=tq),
        grid=(bsz, NSA_KV_HEADS, nq),
        in_specs=[pl.BlockSpec((1, 1, 1, DH, R), lambda b, h, i: (b, h, i, 0, 0)),
                  pl.BlockSpec((1, 1, nchunk, DH), lambda b, h, i: (b, h, 0, 0)),
                  pl.BlockSpec((1, 1, DH, nchunk), lambda b, h, i: (b, h, 0, 0)),
                  seq, seq_t, seq, seq_t,
                  pl.BlockSpec((1, 3 * NSA_GQA, tq), lambda b, h, i: (h, 0, b * nq + i))],
        out_specs=pl.BlockSpec((tq, NSA_GQA * DH), lambda b, h, i: (b * nq + i, h)),
        out_shape=jax.ShapeDtypeStruct((n, NSA_WIDTH), bf16),
        scratch_shapes=[pltpu.VMEM((2, 1, R), f32), pltpu.VMEM((2, 1, R), f32),
                        pltpu.VMEM((2, DH, R), f32), pltpu.VMEM((t_len // SLC_BLOCK, tq), f32)],
        compiler_params=_cparams(3),
        name="nsa_attention",
    )(qt, kc, vct, ks, vst, kw, vwt, gates)


ROW_CH = LANE


def _outproj_body(ya_ref, yb_ref, wa_ref, wb_ref, x_ref, g_ref, wr_ref, br_ref,
                  x1_ref, h_ref, ti_ref, tw_ref, row_ref, *, tn):
    j = pl.program_id(1)
    nj = pl.num_programs(1)
    acc = x_ref[...] + jnp.dot(ya_ref[...], wa_ref[...], preferred_element_type=f32)
    acc = acc + jnp.dot(yb_ref[...], wb_ref[...], preferred_element_type=f32)
    x1_ref[...] = acc
    row_ref[j] = acc

    @pl.when(j == nj - 1)
    def _():
        tm = x1_ref.shape[0]
        ssq = jnp.zeros((tm, 1), f32)
        for jj in range(row_ref.shape[0]):
            v = row_ref[jj]
            ssq = ssq + jnp.sum(v * v, axis=-1, keepdims=True)
        rinv = lax.rsqrt(ssq / (row_ref.shape[0] * tn) + EPS)
        logits = jnp.broadcast_to(br_ref[...], (tm, LANE))
        for jj in range(row_ref.shape[0]):
            hh = row_ref[jj] * rinv * g_ref[:, jj * tn:(jj + 1) * tn]
            for c in range(tn // ROW_CH):
                h_ref[:, jj * (tn // ROW_CH) + c, :] = hh[:, c * ROW_CH:(c + 1) * ROW_CH]
            logits = logits + jnp.dot(hh, wr_ref[jj * tn:(jj + 1) * tn, :],
                                      preferred_element_type=f32, precision=lax.Precision.HIGHEST)
        lane = lax.broadcasted_iota(i32, (tm, LANE), 1)
        val = jnp.where(lane < N_EXPERTS, logits, -jnp.inf)
        idxs = jnp.zeros((tm, LANE), i32)
        vals = jnp.full((tm, LANE), -jnp.inf, f32)
        for k in range(TOP_K):
            mx = jnp.max(val, axis=-1, keepdims=True)
            first = jnp.min(jnp.where(val == mx, lane, LANE), axis=-1, keepdims=True)
            idxs = jnp.where(lane == k, first, idxs)
            vals = jnp.where(lane == k, mx, vals)
            val = jnp.where(lane == first, -jnp.inf, val)
        e = jnp.exp(vals - jnp.max(vals, axis=-1, keepdims=True))
        tw_ref[...] = e / jnp.sum(e, axis=-1, keepdims=True)
        ti_ref[...] = idxs


def _outproj_router(ya, yb, wa, wb, x2, g2, wr, br, tm=256, tn=512):
    n, d = x2.shape
    nj = d // tn
    ka = ya.shape[1]
    kb = yb.shape[1]
    return pl.pallas_call(
        functools.partial(_outproj_body, tn=tn),
        grid=(n // tm, nj),
        in_specs=[pl.BlockSpec((tm, ka), lambda i, j: (i, 0)),
                  pl.BlockSpec((tm, kb), lambda i, j: (i, 0)),
                  pl.BlockSpec((ka, tn), lambda i, j: (0, j)),
                  pl.BlockSpec((kb, tn), lambda i, j: (0, j)),
                  pl.BlockSpec((tm, tn), lambda i, j: (i, j)),
                  pl.BlockSpec((1, d), lambda i, j: (0, 0)),
                  pl.BlockSpec((d, LANE), lambda i, j: (0, 0)),
                  pl.BlockSpec((1, LANE), lambda i, j: (0, 0))],
        out_specs=[pl.BlockSpec((tm, tn), lambda i, j: (i, j)),
                   pl.BlockSpec((tm, d // ROW_CH, ROW_CH), lambda i, j: (i, 0, 0)),
                   pl.BlockSpec((tm, LANE), lambda i, j: (i, 0)),
                   pl.BlockSpec((tm, LANE), lambda i, j: (i, 0))],
        out_shape=[jax.ShapeDtypeStruct((n, d), f32),
                   jax.ShapeDtypeStruct((n, d // ROW_CH, ROW_CH), f32),
                   jax.ShapeDtypeStruct((n, LANE), i32),
                   jax.ShapeDtypeStruct((n, LANE), f32)],
        scratch_shapes=[pltpu.VMEM((nj, tm, tn), f32)],
        compiler_params=_cparams(2),
        name="outproj_router",
    )(ya, yb, wa, wb, x2, g2, wr, br)


MOE_TM = 256
DMA_UNROLL = 8


class _RowGather:
    def __init__(self, src_ref, dst_of, sem, count):
        self.src_ref, self.dst_of, self.sem, self.count = src_ref, dst_of, sem, count

    def _copy(self, idx_ref, slot, r):
        return pltpu.make_async_copy(self.src_ref.at[idx_ref[0, 0, r]], self.dst_of(slot, r),
                                     self.sem.at[slot])

    def start(self, idx_ref, slot):
        def body(r, c):
            self._copy(idx_ref, slot, r).start()
            return c
        lax.fori_loop(0, self.count, body, 0, unroll=DMA_UNROLL)

    def wait(self, idx_ref, slot):
        def body(r, c):
            self._copy(idx_ref, slot, r).wait()
            return c
        lax.fori_loop(0, self.count, body, 0, unroll=DMA_UNROLL)


def _gather_body(cur_ref, nxt_ref, nu_ref, src_ref, o_ref, buf_ref, sem):
    i = pl.program_id(0)
    tmr = o_ref.shape[0]
    nch = buf_ref.shape[2]
    slot = lax.rem(i, 2)
    n_live = jnp.minimum(nu_ref[0], pl.num_programs(0))
    rows = _RowGather(src_ref, lambda s, r: buf_ref.at[s, r], sem, tmr)

    @pl.when((i == 0) & (n_live > 0))
    def _():
        rows.start(cur_ref, 0)

    @pl.when(i + 1 < n_live)
    def _():
        rows.start(nxt_ref, 1 - slot)

    @pl.when(i < n_live)
    def _():
        rows.wait(cur_ref, slot)
        for r0 in range(0, tmr, 8):
            blk = pltpu.einshape("rcl->crl", buf_ref[slot, r0:r0 + 8])
            for c in range(nch):
                o_ref[r0:r0 + 8, c * ROW_CH:(c + 1) * ROW_CH] = blk[c].astype(o_ref.dtype)

    @pl.when(i >= n_live)
    def _():
        o_ref[...] = jnp.zeros(o_ref.shape, o_ref.dtype)


def _gather_tokens(src3, idx, n_used, tmr):
    n_rows = idx.shape[0]
    _, nch, chw = src3.shape
    nt = n_rows // tmr
    idx3 = idx.reshape(nt, 1, tmr)
    return pl.pallas_call(
        _gather_body,
        grid=(nt,),
        in_specs=[pl.BlockSpec((1, 1, tmr), lambda i: (i, 0, 0), memory_space=pltpu.SMEM),
                  pl.BlockSpec((1, 1, tmr), lambda i: (jnp.minimum(i + 1, nt - 1), 0, 0),
                               memory_space=pltpu.SMEM),
                  pl.BlockSpec(memory_space=pltpu.SMEM),
                  pl.BlockSpec(memory_space=pl.ANY)],
        out_specs=pl.BlockSpec((tmr, nch * chw), lambda i: (i, 0)),
        out_shape=jax.ShapeDtypeStruct((n_rows, nch * chw), bf16),
        scratch_shapes=[pltpu.VMEM((2, tmr, nch, chw), src3.dtype), pltpu.SemaphoreType.DMA((2,))],
        compiler_params=_cparams(1),
        name="moe_gather",
    )(idx3, idx3, n_used, src3)


def _expert_changed(be_ref, i):
    return (i == 0) | (be_ref[i] != be_ref[jnp.maximum(i - 1, 0)])


def _up_body(be_ref, nu_ref, x_ref, wg_ref, wl_ref, bg_ref, bl_ref, o_ref, wgs_ref, wls_ref):
    i = pl.program_id(1)
    used = i < nu_ref[0]

    @pl.when(used & _expert_changed(be_ref, i))
    def _():
        wgs_ref[...] = wg_ref[0].astype(bf16)
        wls_ref[...] = wl_ref[0].astype(bf16)

    @pl.when(used)
    def _():
        x = x_ref[...]
        glu = jnp.dot(x, wgs_ref[...], preferred_element_type=f32) + bg_ref[0]
        lin = jnp.dot(x, wls_ref[...], preferred_element_type=f32) + bl_ref[0]
        glu = jnp.minimum(glu, SWIGLU_LIMIT)
        lin = jnp.clip(lin, -SWIGLU_LIMIT, SWIGLU_LIMIT)
        act = glu * (1.0 / (1.0 + jnp.exp(-SWIGLU_ALPHA * glu))) * (lin + 1.0)
        o_ref[...] = act.astype(bf16)

    @pl.when(jnp.logical_not(used))
    def _():
        o_ref[...] = jnp.zeros(o_ref.shape, o_ref.dtype)


def _moe_up(blk_e, n_used, xs, w_up, b_up3, tn=384):
    n_rows, d = xs.shape
    nb = n_rows // MOE_TM
    nj = D_EXPERT // tn
    gs = pltpu.PrefetchScalarGridSpec(
        num_scalar_prefetch=2,
        grid=(nj, nb),
        in_specs=[pl.BlockSpec((MOE_TM, d), lambda j, i, be, nu: (i, 0)),
                  pl.BlockSpec((1, d, tn), lambda j, i, be, nu: (be[i], 0, j)),
                  pl.BlockSpec((1, d, tn), lambda j, i, be, nu: (be[i], 0, nj + j)),
                  pl.BlockSpec((1, 1, tn), lambda j, i, be, nu: (be[i], 0, j)),
                  pl.BlockSpec((1, 1, tn), lambda j, i, be, nu: (be[i], 0, nj + j))],
        out_specs=pl.BlockSpec((MOE_TM, tn), lambda j, i, be, nu: (i, j)),
        scratch_shapes=[pltpu.VMEM((d, tn), bf16), pltpu.VMEM((d, tn), bf16)])
    return pl.pallas_call(
        _up_body, grid_spec=gs,
        out_shape=jax.ShapeDtypeStruct((n_rows, D_EXPERT), bf16),
        compiler_params=_cparams(2),
        name="moe_up",
    )(blk_e, n_used, xs, w_up, w_up, b_up3, b_up3)


def _down_body(be_ref, nu_ref, a_ref, w_ref, b_ref, rw_ref, o_ref, ws_ref):
    i = pl.program_id(1)
    used = i < nu_ref[0]

    @pl.when(used & _expert_changed(be_ref, i))
    def _():
        ws_ref[...] = w_ref[0].astype(bf16)

    @pl.when(used)
    def _():
        y = jnp.dot(a_ref[...], ws_ref[...], preferred_element_type=f32) + b_ref[0]
        y = y * rw_ref[...]
        for c in range(o_ref.shape[1]):
            o_ref[:, c, :] = y[:, c * ROW_CH:(c + 1) * ROW_CH]

    @pl.when(jnp.logical_not(used))
    def _():
        o_ref[...] = jnp.zeros(o_ref.shape, o_ref.dtype)


def _moe_down(blk_e, n_used, act, w_down, b_down3, row_w, tn=2048):
    n_rows, de = act.shape
    d = w_down.shape[2]
    nb = n_rows // MOE_TM
    nj = d // tn
    gs = pltpu.PrefetchScalarGridSpec(
        num_scalar_prefetch=2,
        grid=(nj, nb),
        in_specs=[pl.BlockSpec((MOE_TM, de), lambda j, i, be, nu: (i, 0)),
                  pl.BlockSpec((1, de, tn), lambda j, i, be, nu: (be[i], 0, j)),
                  pl.BlockSpec((1, 1, tn), lambda j, i, be, nu: (be[i], 0, j)),
                  pl.BlockSpec((MOE_TM, 1), lambda j, i, be, nu: (i, 0))],
        out_specs=pl.BlockSpec((MOE_TM, tn // ROW_CH, ROW_CH), lambda j, i, be, nu: (i, j, 0)),
        scratch_shapes=[pltpu.VMEM((de, tn), bf16)])
    return pl.pallas_call(
        _down_body, grid_spec=gs,
        out_shape=jax.ShapeDtypeStruct((n_rows, d // ROW_CH, ROW_CH), f32),
        compiler_params=_cparams(2),
        name="moe_down",
    )(blk_e, n_used, act, w_down, b_down3, row_w)


def _combine_body(cur_ref, nxt_ref, ys_ref, x_ref, o_ref, buf_ref, sem):
    i = pl.program_id(0)
    nt = pl.num_programs(0)
    tm = o_ref.shape[0]
    nch = buf_ref.shape[3]
    slot = lax.rem(i, 2)
    rows = _RowGather(
        ys_ref, lambda s, a: buf_ref.at[s, a & (TOP_K - 1), lax.shift_right_logical(a, 2)],
        sem, tm * TOP_K)

    @pl.when(i == 0)
    def _():
        rows.start(cur_ref, 0)

    @pl.when(i + 1 < nt)
    def _():
        rows.start(nxt_ref, 1 - slot)

    rows.wait(cur_ref, slot)
    for r0 in range(0, tm, 8):
        tot = buf_ref[slot, 0, r0:r0 + 8]
        for k in range(1, TOP_K):
            tot = tot + buf_ref[slot, k, r0:r0 + 8]
        blk = pltpu.einshape("rcl->crl", tot)
        for c in range(nch):
            cs = slice(c * ROW_CH, (c + 1) * ROW_CH)
            o_ref[r0:r0 + 8, cs] = x_ref[r0:r0 + 8, cs] + blk[c]


def _combine(ys3, pos, x1, tm=128):
    n, d = x1.shape
    _, nch, chw = ys3.shape
    nt = n // tm
    pos3 = pos.reshape(nt, 1, tm * TOP_K)
    return pl.pallas_call(
        _combine_body,
        grid=(nt,),
        in_specs=[pl.BlockSpec((1, 1, tm * TOP_K), lambda i: (i, 0, 0), memory_space=pltpu.SMEM),
                  pl.BlockSpec((1, 1, tm * TOP_K), lambda i: (jnp.minimum(i + 1, nt - 1), 0, 0),
                               memory_space=pltpu.SMEM),
                  pl.BlockSpec(memory_space=pl.ANY),
                  pl.BlockSpec((tm, d), lambda i: (i, 0))],
        out_specs=pl.BlockSpec((tm, d), lambda i: (i, 0)),
        out_shape=jax.ShapeDtypeStruct((n, d), f32),
        scratch_shapes=[pltpu.VMEM((2, TOP_K, tm, nch, chw), f32), pltpu.SemaphoreType.DMA((2,))],
        compiler_params=_cparams(1),
        name="moe_combine",
    )(pos3, pos3, ys3, x1)


def _route(top_idx, top_w):
    n = top_idx.shape[0]
    n_assign = n * TOP_K
    e_flat = top_idx.reshape(-1)
    w_flat = top_w.reshape(-1)
    order = jnp.argsort(e_flat).astype(i32)
    rank = jnp.argsort(order).astype(i32)
    experts = jnp.arange(N_EXPERTS, dtype=i32)
    counts = jnp.sum((e_flat[:, None] == experts[None, :]).astype(i32), axis=0)
    padded = (counts + MOE_TM - 1) // MOE_TM * MOE_TM
    start = jnp.cumsum(counts) - counts
    pend = jnp.cumsum(padded)
    pstart = pend - padded
    n_blocks = -(-n_assign // MOE_TM) + N_EXPERTS
    n_rows = n_blocks * MOE_TM
    n_used = pend[-1] // MOE_TM
    bi = jnp.arange(n_blocks, dtype=i32)
    raw_e = jnp.minimum(jnp.sum((pend[None, :] <= (bi * MOE_TM)[:, None]).astype(i32), axis=1),
                        N_EXPERTS - 1)
    blk_e = jnp.where(bi < n_used, raw_e, raw_e[jnp.maximum(n_used - 1, 0)]).astype(i32)
    r = jnp.arange(n_rows, dtype=i32)
    e_r = raw_e[r // MOE_TM]
    off = r - pstart[e_r]
    valid = (off < counts[e_r]) & (r < pend[-1])
    a_sorted = order[jnp.clip(start[e_r] + off, 0, n_assign - 1)]
    row_tok = jnp.where(valid, a_sorted // TOP_K, 0).astype(i32)
    row_w = jnp.where(valid, w_flat[a_sorted], 0.0)
    pos = (pstart[e_flat] + rank - start[e_flat]).astype(i32)
    return row_tok, row_w, blk_e, n_used.reshape(1).astype(i32), pos


NSA_TQ = 256


def _layer(x, ln1_g, w_in, conv_w, conv_b, dt_bias, a_log, d_skip, ssd_norm_g, q_norm_g,
           k_norm_g, cmp_pe, cmp_w1, cmp_w2, w_out, ln2_g, w_router, b_router, w_up, b_up,
           w_down, b_down):
    bsz, t_len, d = x.shape
    n = bsz * t_len
    x2 = x.reshape(n, d)

    o_dt = SSD_D_INNER + SSD_D_INNER + 2 * SSD_GN
    o_q = o_dt + SSD_HEADS
    o_kv = o_q + NSA_WIDTH
    o_g = o_kv + 6 * NSA_KV_WIDTH
    n_in = o_g + 3 * NSA_HEADS
    w_perm = jnp.concatenate(
        [w_in[:, :o_dt], w_in[:, o_q:o_g], w_in[:, o_dt:o_q], w_in[:, o_g:n_in],
         jnp.zeros((d, NP_PROJ - n_in), w_in.dtype)], axis=1).astype(bf16)

    proj = _inproj(x2, ln1_g.reshape(1, d), w_perm)

    lane_pad = lambda v: jnp.concatenate([v, jnp.zeros((LANE - v.shape[0],), f32)]).reshape(1, LANE)
    y_ssd = _ssd(proj, bsz, t_len, conv_w, conv_b.reshape(1, -1), lane_pad(dt_bias),
                 lane_pad(a_log), jnp.repeat(d_skip, SSD_HEAD_DIM).reshape(1, -1),
                 ssd_norm_g.reshape(1, -1))

    kc, vct = _compress(proj, bsz, t_len, cmp_pe, cmp_w1.astype(bf16), cmp_w2.astype(bf16),
                        k_norm_g[0:1])
    qt, ks, vst, kw, vwt = _qk_prep(proj, q_norm_g.reshape(1, DH), k_norm_g, bsz, t_len, NSA_TQ)
    gates = proj[:, COL_DT + GATE_OFF:COL_DT + GATE_OFF + 3 * NSA_HEADS]
    gates = gates.reshape(n, NSA_KV_HEADS, 3 * NSA_GQA).transpose(1, 2, 0)
    y_nsa = _nsa_attention(qt, kc, vct, ks, vst, kw, vwt, gates, bsz, t_len, NSA_TQ)

    wo = w_out.astype(bf16)
    wr = jnp.concatenate([w_router, jnp.zeros((d, LANE - N_EXPERTS), f32)], axis=1)
    br = jnp.concatenate([b_router, jnp.zeros((LANE - N_EXPERTS,), f32)]).reshape(1, LANE)
    x1, h3, ti, tw = _outproj_router(y_ssd, y_nsa, wo[:SSD_D_INNER], wo[SSD_D_INNER:], x2,
                                     ln2_g.reshape(1, d), wr, br)

    row_tok, row_w, blk_e, n_used, pos = _route(ti[:, :TOP_K], tw[:, :TOP_K])
    xs = _gather_tokens(h3, row_tok, n_used, MOE_TM)
    act = _moe_up(blk_e, n_used, xs, w_up, b_up.reshape(N_EXPERTS, 1, -1))
    ys3 = _moe_down(blk_e, n_used, act, w_down, b_down.reshape(N_EXPERTS, 1, -1),
                    row_w.reshape(-1, 1))
    out = _combine(ys3, pos, x1)
    return out.reshape(bsz, t_len, d)


def kernel(x, ln1_g, w_in, conv_w, conv_b, dt_bias, a_log, d_skip, ssd_norm_g, q_norm_g,
           k_norm_g, cmp_pe, cmp_w1, cmp_w2, w_out, ln2_g, w_router, b_router, w_up, b_up,
           w_down, b_down):
    return _layer(x, ln1_g[0], w_in[0], conv_w[0], conv_b[0], dt_bias[0], a_log[0], d_skip[0],
                  ssd_norm_g[0], q_norm_g[0], k_norm_g[0], cmp_pe[0], cmp_w1[0], cmp_w2[0],
                  w_out[0], ln2_g[0], w_router[0], b_router[0], w_up[0], b_up[0], w_down[0],
                  b_down[0])
```

```python
import functools
import math

import jax
import jax.numpy as jnp
from jax import lax
from jax.experimental import pallas as pl
from jax.experimental.pallas import tpu as pltpu

f32 = jnp.float32
bf16 = jnp.bfloat16
i32 = jnp.int32

D_MODEL = 4096
SSD_D_INNER = 2048
SSD_HEAD_DIM = 64
SSD_HEADS = 32
SSD_GROUPS = 8
SSD_HPG = 4
SSD_STATE = 128
SSD_CONV = 4
SSD_CHUNK = 256
SSD_GN = SSD_GROUPS * SSD_STATE
SSD_GW = SSD_HPG * SSD_HEAD_DIM
NSA_HEADS = 16
NSA_KV_HEADS = 4
NSA_GQA = 4
DH = 128
NSA_WIDTH = 2048
NSA_KV_WIDTH = 512
CMP_BLOCK = 32
CMP_STRIDE = 16
CMP_HIDDEN = 256
SLC_BLOCK = 64
SLC_TOPK = 16
WINDOW = 512
N_EXPERTS = 32
TOP_K = 4
D_EXPERT = 1536
SWIGLU_LIMIT = 7.0
SWIGLU_ALPHA = 1.702
EPS = 1e-5
NEG = -1e30

COL_Z = 0
COL_X = 2048
COL_B = 4096
COL_C = 5120
COL_Q = 6144
COL_KV = 8192
COL_DT = 11264
GATE_OFF = 32
NP_PROJ = 11520

LANE = 128
VMEM_LIMIT = 56 * 1024 * 1024


def _cparams(n_axes):
    return pltpu.CompilerParams(dimension_semantics=("arbitrary",) * n_axes,
                                vmem_limit_bytes=VMEM_LIMIT)


def _silu(v):
    return v * (1.0 / (1.0 + jnp.exp(-v)))


def _inproj_body(x_ref, g_ref, w_ref, o_ref, h_ref):
    @pl.when(pl.program_id(1) == 0)
    def _():
        x = x_ref[...]
        ms = jnp.mean(x * x, axis=-1, keepdims=True)
        h_ref[...] = (x * lax.rsqrt(ms + EPS) * g_ref[...]).astype(bf16)

    o_ref[...] = jnp.dot(h_ref[...], w_ref[...], preferred_element_type=f32)


def _inproj(x2, g, w, tm=512, tn=768):
    n, d = x2.shape
    npj = w.shape[1]
    return pl.pallas_call(
        _inproj_body,
        grid=(n // tm, npj // tn),
        in_specs=[pl.BlockSpec((tm, d), lambda i, j: (i, 0)),
                  pl.BlockSpec((1, d), lambda i, j: (0, 0)),
                  pl.BlockSpec((d, tn), lambda i, j: (0, j))],
        out_specs=pl.BlockSpec((tm, tn), lambda i, j: (i, j)),
        out_shape=jax.ShapeDtypeStruct((n, npj), f32),
        scratch_shapes=[pltpu.VMEM((tm, d), bf16)],
        compiler_params=_cparams(2),
        name="inproj",
    )(x2, g, w)


def _lane_expand(cols, width):
    L = cols[0].shape[0]
    n = len(cols)
    lane = lax.broadcasted_iota(i32, (L, n * width), 1)
    out = jnp.broadcast_to(cols[n - 1], (L, n * width))
    for j in range(n - 2, -1, -1):
        out = jnp.where(lane < (j + 1) * width, jnp.broadcast_to(cols[j], (L, n * width)), out)
    return out


def _ssd_body(z_ref, x_ref, b_ref, c_ref, dt_ref, cw_ref, cb_ref, dtb_ref, alog_ref,
              dskip_ref, ng_ref, y_ref, buf_ref, xs_ref, bt_ref, cs_ref, h_ref):
    L = SSD_CHUNK
    c_idx = pl.program_id(1)

    @pl.when(c_idx == 0)
    def _():
        buf_ref[0:8, :] = jnp.zeros((8, buf_ref.shape[1]), f32)
        h_ref[...] = jnp.zeros(h_ref.shape, f32)

    buf_ref[8:8 + L, 0:SSD_D_INNER] = x_ref[...]
    buf_ref[8:8 + L, SSD_D_INNER:SSD_D_INNER + SSD_GN] = b_ref[...]
    buf_ref[8:8 + L, SSD_D_INNER + SSD_GN:] = c_ref[...]

    cw = 512
    n_ch = buf_ref.shape[1]
    for c0 in range(0, n_ch, cw):
        acc = jnp.broadcast_to(cb_ref[:, c0:c0 + cw], (L, cw))
        for k in range(SSD_CONV):
            acc = acc + cw_ref[k:k + 1, c0:c0 + cw] * buf_ref[5 + k:5 + k + L, c0:c0 + cw]
        v = _silu(acc)
        if c0 < SSD_D_INNER:
            xs_ref[:, c0:c0 + cw] = v
        elif c0 < SSD_D_INNER + SSD_GN:
            bt_ref[c0 - SSD_D_INNER:c0 - SSD_D_INNER + cw, :] = v.T.astype(bf16)
        else:
            o = c0 - SSD_D_INNER - SSD_GN
            cs_ref[:, o:o + cw] = v.astype(bf16)
    buf_ref[0:8, :] = buf_ref[L:L + 8, :]

    dt = dt_ref[...] + dtb_ref[...]
    dt = jnp.maximum(dt, 0.0) + jnp.log(1.0 + jnp.exp(-jnp.abs(dt)))
    da = dt * -jnp.exp(alog_ref[...])
    r = lax.broadcasted_iota(i32, (L, L), 0)
    s = lax.broadcasted_iota(i32, (L, L), 1)
    tri = (s <= r).astype(f32)
    acs = jnp.dot(tri, da, preferred_element_type=f32,
                  precision=lax.Precision.HIGHEST)
    acs_t = acs.T
    causal = s <= r
    a_last = acs[L - 1:L, :]
    e_acs = jnp.exp(acs)
    e_rem = jnp.exp(a_last - acs)
    e_last = jnp.exp(a_last)
    lane_gw = lax.broadcasted_iota(i32, (L, SSD_GW), 1)

    for g in range(SSD_GROUPS):
        hs = [g * SSD_HPG + j for j in range(SSD_HPG)]
        xg = xs_ref[:, g * SSD_GW:(g + 1) * SSD_GW]
        bgt = bt_ref[g * SSD_STATE:(g + 1) * SSD_STATE, :]
        cg = cs_ref[:, g * SSD_STATE:(g + 1) * SSD_STATE]
        dt_x = _lane_expand([dt[:, h:h + 1] for h in hs], SSD_HEAD_DIM)
        xdt = xg * dt_x
        cbm = jnp.dot(cg, bgt, preferred_element_type=f32)
        y = jnp.zeros((L, SSD_GW), f32)
        for j, h in enumerate(hs):
            seg = jnp.where(causal, acs[:, h:h + 1] - acs_t[h:h + 1, :], NEG)
            w = (cbm * jnp.exp(seg)).astype(bf16)
            band = (lane_gw >= j * SSD_HEAD_DIM) & (lane_gw < (j + 1) * SSD_HEAD_DIM)
            xm = jnp.where(band, xdt, 0.0).astype(bf16)
            y = y + jnp.dot(w, xm, preferred_element_type=f32)
        hprev = h_ref[g]
        y_off = jnp.dot(cg, hprev.astype(bf16), preferred_element_type=f32)
        y = y + y_off * _lane_expand([e_acs[:, h:h + 1] for h in hs], SSD_HEAD_DIM)
        xw = (xdt * _lane_expand([e_rem[:, h:h + 1] for h in hs], SSD_HEAD_DIM)).astype(bf16)
        st = jnp.dot(bgt, xw, preferred_element_type=f32)
        dec = _lane_expand([e_last[:, h:h + 1] for h in hs], SSD_HEAD_DIM)
        h_ref[g] = hprev * dec + st
        y = y + xg * dskip_ref[:, g * SSD_GW:(g + 1) * SSD_GW]
        y = y * _silu(z_ref[:, g * SSD_GW:(g + 1) * SSD_GW])
        y = y * lax.rsqrt(jnp.mean(y * y, axis=-1, keepdims=True) + EPS)
        y_ref[:, g * SSD_GW:(g + 1) * SSD_GW] = (y * ng_ref[:, g * SSD_GW:(g + 1) * SSD_GW]).astype(bf16)


def _ssd(proj, bsz, t_len, conv_w, conv_b, dt_bias, a_log, dskip_x, norm_g):
    L = SSD_CHUNK
    nc = t_len // L
    n = bsz * t_len
    nch = SSD_D_INNER + 2 * SSD_GN
    row = lambda b, c: b * nc + c
    full = lambda shape: pl.BlockSpec(shape, lambda b, c: (0,) * len(shape))
    return pl.pallas_call(
        _ssd_body,
        grid=(bsz, nc),
        in_specs=[pl.BlockSpec((L, SSD_D_INNER), lambda b, c: (row(b, c), COL_Z // SSD_D_INNER)),
                  pl.BlockSpec((L, SSD_D_INNER), lambda b, c: (row(b, c), COL_X // SSD_D_INNER)),
                  pl.BlockSpec((L, SSD_GN), lambda b, c: (row(b, c), COL_B // SSD_GN)),
                  pl.BlockSpec((L, SSD_GN), lambda b, c: (row(b, c), COL_C // SSD_GN)),
                  pl.BlockSpec((L, LANE), lambda b, c: (row(b, c), COL_DT // LANE)),
                  full((SSD_CONV, nch)), full((1, nch)), full((1, LANE)),
                  full((1, LANE)), full((1, SSD_D_INNER)), full((1, SSD_D_INNER))],
        out_specs=pl.BlockSpec((L, SSD_D_INNER), lambda b, c: (row(b, c), 0)),
        out_shape=jax.ShapeDtypeStruct((n, SSD_D_INNER), bf16),
        scratch_shapes=[pltpu.VMEM((L + 8, nch), f32),
                        pltpu.VMEM((L, SSD_D_INNER), f32),
                        pltpu.VMEM((SSD_GN, L), bf16),
                        pltpu.VMEM((L, SSD_GN), bf16),
                        pltpu.VMEM((SSD_GROUPS, SSD_STATE, SSD_GW), f32)],
        compiler_params=_cparams(2),
        name="ssd",
    )(proj, proj, proj, proj, proj, conv_w, conv_b, dt_bias, a_log, dskip_x, norm_g)


def _cmp_body(uk_ref, uv_ref, pe_ref, w1_ref, w2_ref, g_ref, kc_ref, vct_ref):
    nchunk = uk_ref.shape[0]

    def branch(u_ref, kv):
        hid_a = jnp.zeros((nchunk, CMP_HIDDEN), f32)
        hid_b = jnp.zeros((nchunk, CMP_HIDDEN), f32)
        ut = pltpu.einshape("csd->scd", u_ref[...])
        for s in range(CMP_STRIDE):
            us = ut[s]
            hid_a = hid_a + jnp.dot((us + pe_ref[kv, s:s + 1, :]).astype(bf16), w1_ref[kv, s],
                                    preferred_element_type=f32)
            hid_b = hid_b + jnp.dot(
                (us + pe_ref[kv, CMP_STRIDE + s:CMP_STRIDE + s + 1, :]).astype(bf16),
                w1_ref[kv, CMP_STRIDE + s], preferred_element_type=f32)
        hid = hid_a + pltpu.roll(hid_b, nchunk - 1, 0)
        out = jnp.dot(_silu(hid).astype(bf16), w2_ref[kv], preferred_element_type=f32)
        rowi = lax.broadcasted_iota(i32, out.shape, 0)
        return jnp.where(rowi == nchunk - 1, 0.0, out)

    k = branch(uk_ref, 0)
    k = k * lax.rsqrt(jnp.mean(k * k, axis=-1, keepdims=True) + EPS) * g_ref[...]
    kc_ref[0, 0] = k.astype(bf16)
    vct_ref[0, 0] = branch(uv_ref, 1).T.astype(bf16)


def _compress(proj, bsz, t_len, pe, w1, w2, kg):
    nchunk = t_len // CMP_STRIDE
    H = NSA_KV_HEADS
    u3 = proj.reshape(bsz * nchunk, CMP_STRIDE, proj.shape[1])
    useg = lambda kv: pl.BlockSpec((nchunk, CMP_STRIDE, DH),
                                   lambda b, h: (b, 0, COL_KV // DH + kv * H + h))
    full = lambda shape: pl.BlockSpec(shape, lambda b, h: (0,) * len(shape))
    return pl.pallas_call(
        _cmp_body,
        grid=(bsz, H),
        in_specs=[useg(0), useg(1), full((2, CMP_BLOCK, DH)), full((2, CMP_BLOCK, DH, CMP_HIDDEN)),
                  full((2, CMP_HIDDEN, DH)), full((1, DH))],
        out_specs=[pl.BlockSpec((1, 1, nchunk, DH), lambda b, h: (b, h, 0, 0)),
                   pl.BlockSpec((1, 1, DH, nchunk), lambda b, h: (b, h, 0, 0))],
        out_shape=[jax.ShapeDtypeStruct((bsz, H, nchunk, DH), bf16),
                   jax.ShapeDtypeStruct((bsz, H, DH, nchunk), bf16)],
        compiler_params=_cparams(2),
        name="nsa_compress",
    )(u3, u3, pe, w1, w2, kg)


def _qk_prep_body(q_ref, s_ref, w_ref, qg_ref, kg_ref, qt_ref, ks_ref, vst_ref, kw_ref, vwt_ref):
    scale = DH ** -0.5
    tq = q_ref.shape[0]

    def hnorm(v, g):
        return v * lax.rsqrt(jnp.mean(v * v, axis=-1, keepdims=True) + EPS) * g

    for h in range(NSA_KV_HEADS):
        for g in range(NSA_GQA):
            sl = slice((h * NSA_GQA + g) * DH, (h * NSA_GQA + g + 1) * DH)
            qn = hnorm(q_ref[:, sl], qg_ref[...]) * scale
            qt_ref[0, h, 0, :, g * tq:(g + 1) * tq] = qn.T.astype(bf16)
        sl = slice(h * DH, (h + 1) * DH)
        sv = slice(NSA_KV_WIDTH + h * DH, NSA_KV_WIDTH + (h + 1) * DH)
        ks_ref[:, sl] = hnorm(s_ref[:, sl], kg_ref[1:2, :]).astype(bf16)
        kw_ref[:, sl] = hnorm(w_ref[:, sl], kg_ref[2:3, :]).astype(bf16)
        vst_ref[0, h, 0] = s_ref[:, sv].T.astype(bf16)
        vwt_ref[0, h, 0] = w_ref[:, sv].T.astype(bf16)


def _qk_prep(proj, qg, kg, bsz, t_len, tq):
    n = proj.shape[0]
    nq = t_len // tq
    H = NSA_KV_HEADS
    w2 = 2 * NSA_KV_WIDTH
    row = lambda b, i: b * nq + i
    vt_spec = pl.BlockSpec((1, H, 1, DH, tq), lambda b, i: (b, 0, i, 0, 0))
    vt_shape = jax.ShapeDtypeStruct((bsz, H, nq, DH, tq), bf16)
    return pl.pallas_call(
        _qk_prep_body,
        grid=(bsz, nq),
        in_specs=[pl.BlockSpec((tq, NSA_WIDTH), lambda b, i: (row(b, i), COL_Q // NSA_WIDTH)),
                  pl.BlockSpec((tq, w2), lambda b, i: (row(b, i), (COL_KV + w2) // w2)),
                  pl.BlockSpec((tq, w2), lambda b, i: (row(b, i), (COL_KV + 2 * w2) // w2)),
                  pl.BlockSpec((1, DH), lambda b, i: (0, 0)),
                  pl.BlockSpec((3, DH), lambda b, i: (0, 0))],
        out_specs=[pl.BlockSpec((1, H, 1, DH, NSA_GQA * tq), lambda b, i: (b, 0, i, 0, 0)),
                   pl.BlockSpec((tq, NSA_KV_WIDTH), lambda b, i: (row(b, i), 0)),
                   vt_spec,
                   pl.BlockSpec((tq, NSA_KV_WIDTH), lambda b, i: (row(b, i), 0)),
                   vt_spec],
        out_shape=[jax.ShapeDtypeStruct((bsz, H, nq, DH, NSA_GQA * tq), bf16),
                   jax.ShapeDtypeStruct((n, NSA_KV_WIDTH), bf16), vt_shape,
                   jax.ShapeDtypeStruct((n, NSA_KV_WIDTH), bf16), vt_shape],
        compiler_params=_cparams(2),
        name="nsa_qk_prep",
    )(proj, proj, proj, qg, kg)


def _nsa_body(qt_ref, kc_ref, vct_ref, ks_ref, vst_ref, kw_ref, vwt_ref, gt_ref, o_ref,
              m_ref, l_ref, acc_ref, sel_ref, *, tq):
    i = pl.program_id(2)
    G = NSA_GQA
    n_cmp = kc_ref.shape[2]
    n_slc = ks_ref.shape[0] // SLC_BLOCK

    qt = qt_ref[0, 0, 0]
    t_row = i * tq + lax.broadcasted_iota(i32, (1, tq), 1)
    t_all = jnp.concatenate([t_row] * G, axis=1)

    s_c = jnp.dot(kc_ref[0, 0], qt, preferred_element_type=f32)
    cend = lax.broadcasted_iota(i32, (n_cmp, 1), 0) * CMP_STRIDE + (CMP_BLOCK - 1)
    m_c = cend <= t_all
    s_c = jnp.where(m_c, s_c, NEG)
    p_c = jnp.where(m_c, jnp.exp(s_c - jnp.max(s_c, axis=0, keepdims=True)), 0.0)
    p_c = p_c / jnp.maximum(jnp.sum(p_c, axis=0, keepdims=True), 1e-30)
    o_c = jnp.dot(vct_ref[0, 0], p_c.astype(bf16), preferred_element_type=f32)

    imp = p_c[:, 0:tq]
    for g in range(1, G):
        imp = imp + p_c[:, g * tq:(g + 1) * tq]
    per = SLC_BLOCK // CMP_STRIDE
    ni = lax.broadcasted_iota(i32, (n_slc, n_cmp), 0)
    ci = lax.broadcasted_iota(i32, (n_slc, n_cmp), 1)
    fold = ((ci // per == ni).astype(f32) + ((ci + 1) // per == ni).astype(f32))
    imp_b = jnp.dot(fold, imp, preferred_element_type=f32, precision=lax.Precision.HIGHEST)
    jb = lax.broadcasted_iota(i32, (n_slc, tq), 0)
    cur = t_row // SLC_BLOCK
    forced = (jb == 0) | (jb == cur) | (jb == cur - 1)
    val = jnp.where(forced, 1e30, jnp.where(jb <= cur, imp_b, -1.0))
    sel = jnp.zeros((n_slc, tq), f32)
    for _ in range(SLC_TOPK):
        mx = jnp.max(val, axis=0, keepdims=True)
        first = jnp.min(jnp.where(val == mx, jb, n_slc), axis=0, keepdims=True)
        hit = (jb == first) & (mx >= 0.0)
        sel = jnp.where(hit, 1.0, sel)
        val = jnp.where(jb == first, -1.0, val)

    def online(k, vt, bias, slot):
        s_ = jnp.dot(k, qt, preferred_element_type=f32)
        s_ = s_ + jnp.concatenate([bias] * G, axis=1)
        m_old = m_ref[slot]
        m_new = jnp.maximum(m_old, jnp.max(s_, axis=0, keepdims=True))
        alpha = jnp.exp(m_old - m_new)
        p = jnp.exp(s_ - m_new)
        l_ref[slot] = alpha * l_ref[slot] + jnp.sum(p, axis=0, keepdims=True)
        acc_ref[slot] = alpha * acc_ref[slot] + jnp.dot(vt, p.astype(bf16), preferred_element_type=f32)
        m_ref[slot] = m_new

    m_ref[...] = jnp.full(m_ref.shape, NEG, f32)
    l_ref[...] = jnp.zeros(l_ref.shape, f32)
    acc_ref[...] = jnp.zeros(acc_ref.shape, f32)
    krow = lax.broadcasted_iota(i32, (tq, 1), 0)
    kcol = lax.broadcasted_iota(i32, (1, tq), 1)

    sel_ref[...] = jnp.where(sel > 0.5, 0.0, NEG)
    bpt = tq // SLC_BLOCK

    def block_bias(kt):
        rows = [jnp.broadcast_to(sel_ref[pl.ds(kt * bpt + j, 1), :], (SLC_BLOCK, tq))
                for j in range(bpt)]
        return jnp.concatenate(rows, axis=0)

    def slc_step(kt, carry):
        k0 = pl.multiple_of(kt * tq, tq)
        online(ks_ref[pl.ds(k0, tq), :], vst_ref[0, 0, kt], block_bias(kt), 0)
        return carry

    lax.fori_loop(0, i, slc_step, 0)
    q0 = pl.multiple_of(i * tq, tq)
    online(ks_ref[pl.ds(q0, tq), :], vst_ref[0, 0, i],
           jnp.where(krow <= kcol, block_bias(i), NEG), 0)

    def win_step(kt, carry):
        k0 = pl.multiple_of(kt * tq, tq)
        key = k0 + krow
        ok = (key <= t_row) & (key > t_row - WINDOW)
        online(kw_ref[pl.ds(k0, tq), :], vwt_ref[0, 0, kt], jnp.where(ok, 0.0, NEG), 1)
        return carry

    lax.fori_loop(jnp.maximum(i - WINDOW // tq, 0), i + 1, win_step, 0)

    o_s = acc_ref[0] / l_ref[0]
    o_w = acc_ref[1] / l_ref[1]
    gts = 1.0 / (1.0 + jnp.exp(-gt_ref[0]))
    for g in range(G):
        cs = slice(g * tq, (g + 1) * tq)
        out_t = (gts[3 * g:3 * g + 1, :] * o_c[:, cs] + gts[3 * g + 1:3 * g + 2, :] * o_s[:, cs]
                 + gts[3 * g + 2:3 * g + 3, :] * o_w[:, cs])
        o_ref[:, g * DH:(g + 1) * DH] = out_t.T.astype(bf16)


def _nsa_attention(qt, kc, vct, ks, vst, kw, vwt, gates, bsz, t_len, tq):
    n = bsz * t_len
    nq = t_len // tq
    nchunk = t_len // CMP_STRIDE
    R = NSA_GQA * tq
    seq = pl.BlockSpec((t_len, DH), lambda b, h, i: (b, h))
    seq_t = pl.BlockSpec((1, 1, nq, DH, tq), lambda b, h, i: (b, h, 0, 0, 0))
    return pl.pallas_call(
        functools.partial(_nsa_body, tq=tq),
        grid=(bsz, NSA_KV_HEADS, nq),
        in_specs=[pl.BlockSpec((1, 1, 1, DH, R), lambda b, h, i: (b, h, i, 0, 0)),
                  pl.BlockSpec((1, 1, nchunk, DH), lambda b, h, i: (b, h, 0, 0)),
                  pl.BlockSpec((1, 1, DH, nchunk), lambda b, h, i: (b, h, 0, 0)),
                  seq, seq_t, seq, seq_t,
                  pl.BlockSpec((1, 3 * NSA_GQA, tq), lambda b, h, i: (h, 0, b * nq + i))],
        out_specs=pl.BlockSpec((tq, NSA_GQA * DH), lambda b, h, i: (b * nq + i, h)),
        out_shape=jax.ShapeDtypeStruct((n, NSA_WIDTH), bf16),
        scratch_shapes=[pltpu.VMEM((2, 1, R), f32), pltpu.VMEM((2, 1, R), f32),
                        pltpu.VMEM((2, DH, R), f32), pltpu.VMEM((t_len // SLC_BLOCK, tq), f32)],
        compiler_params=_cparams(3),
        name="nsa_attention",
    )(qt, kc, vct, ks, vst, kw, vwt, gates)


ROW_CH = LANE


def _outproj_body(ya_ref, yb_ref, wa_ref, wb_ref, x_ref, g_ref, wr_ref, br_ref,
                  x1_ref, h_ref, ti_ref, tw_ref, row_ref, *, tn):
    j = pl.program_id(1)
    nj = pl.num_programs(1)
    acc = x_ref[...] + jnp.dot(ya_ref[...], wa_ref[...], preferred_element_type=f32)
    acc = acc + jnp.dot(yb_ref[...], wb_ref[...], preferred_element_type=f32)
    x1_ref[...] = acc
    row_ref[j] = acc

    @pl.when(j == nj - 1)
    def _():
        tm = x1_ref.shape[0]
        ssq = jnp.zeros((tm, 1), f32)
        for jj in range(row_ref.shape[0]):
            v = row_ref[jj]
            ssq = ssq + jnp.sum(v * v, axis=-1, keepdims=True)
        rinv = lax.rsqrt(ssq / (row_ref.shape[0] * tn) + EPS)
        logits = jnp.broadcast_to(br_ref[...], (tm, LANE))
        for jj in range(row_ref.shape[0]):
            hh = row_ref[jj] * rinv * g_ref[:, jj * tn:(jj + 1) * tn]
            row_ref[jj] = hh
            hi = hh.astype(bf16)
            lo = (hh - hi.astype(f32)).astype(bf16)
            wch = wr_ref[jj * tn:(jj + 1) * tn, :]
            t = jnp.dot(hi, wch, preferred_element_type=f32)
            logits = logits + (t[:, :LANE] + (t[:, LANE:] + jnp.dot(lo, wch[:, :LANE],
                                                                     preferred_element_type=f32)))
        per = tn // ROW_CH
        for r0 in range(0, tm, 8):
            blk = jnp.stack([row_ref[c // per, r0:r0 + 8, (c % per) * ROW_CH:(c % per + 1) * ROW_CH]
                             for c in range(h_ref.shape[1])], axis=0)
            h_ref[r0:r0 + 8] = pltpu.einshape("crl->rcl", blk)
        lane = lax.broadcasted_iota(i32, (tm, LANE), 1)
        val = jnp.where(lane < N_EXPERTS, logits, -jnp.inf)
        idxs = jnp.zeros((tm, LANE), i32)
        vals = jnp.full((tm, LANE), -jnp.inf, f32)
        for k in range(TOP_K):
            mx = jnp.max(val, axis=-1, keepdims=True)
            first = jnp.min(jnp.where(val == mx, lane, LANE), axis=-1, keepdims=True)
            idxs = jnp.where(lane == k, first, idxs)
            vals = jnp.where(lane == k, mx, vals)
            val = jnp.where(lane == first, -jnp.inf, val)
        e = jnp.exp(vals - jnp.max(vals, axis=-1, keepdims=True))
        tw_ref[...] = e / jnp.sum(e, axis=-1, keepdims=True)
        ti_ref[...] = idxs


def _outproj_router(ya, yb, wa, wb, x2, g2, wr, br, tm=256, tn=512):
    n, d = x2.shape
    nj = d // tn
    ka = ya.shape[1]
    kb = yb.shape[1]
    return pl.pallas_call(
        functools.partial(_outproj_body, tn=tn),
        grid=(n // tm, nj),
        in_specs=[pl.BlockSpec((tm, ka), lambda i, j: (i, 0)),
                  pl.BlockSpec((tm, kb), lambda i, j: (i, 0)),
                  pl.BlockSpec((ka, tn), lambda i, j: (0, j)),
                  pl.BlockSpec((kb, tn), lambda i, j: (0, j)),
                  pl.BlockSpec((tm, tn), lambda i, j: (i, j)),
                  pl.BlockSpec((1, d), lambda i, j: (0, 0)),
                  pl.BlockSpec((d, 2 * LANE), lambda i, j: (0, 0)),
                  pl.BlockSpec((1, LANE), lambda i, j: (0, 0))],
        out_specs=[pl.BlockSpec((tm, tn), lambda i, j: (i, j)),
                   pl.BlockSpec((tm, d // ROW_CH, ROW_CH), lambda i, j: (i, 0, 0)),
                   pl.BlockSpec((tm, LANE), lambda i, j: (i, 0)),
                   pl.BlockSpec((tm, LANE), lambda i, j: (i, 0))],
        out_shape=[jax.ShapeDtypeStruct((n, d), f32),
                   jax.ShapeDtypeStruct((n, d // ROW_CH, ROW_CH), f32),
                   jax.ShapeDtypeStruct((n, LANE), i32),
                   jax.ShapeDtypeStruct((n, LANE), f32)],
        scratch_shapes=[pltpu.VMEM((nj, tm, tn), f32)],
        compiler_params=_cparams(2),
        name="outproj_router",
    )(ya, yb, wa, wb, x2, g2, wr, br)


MOE_TM = 512
DMA_UNROLL = 8


class _RowGather:
    def __init__(self, src_ref, dst_of, sem, count):
        self.src_ref, self.dst_of, self.sem, self.count = src_ref, dst_of, sem, count

    def _copy(self, idx_of, slot, r):
        return pltpu.make_async_copy(self.src_ref.at[idx_of(r)], self.dst_of(slot, r),
                                     self.sem.at[slot])

    def start(self, idx_of, slot):
        def body(r, c):
            self._copy(idx_of, slot, r).start()
            return c
        lax.fori_loop(0, self.count, body, 0, unroll=DMA_UNROLL)

    def wait(self, idx_of, slot):
        def body(r, c):
            self._copy(idx_of, slot, r).wait()
            return c
        lax.fori_loop(0, self.count, body, 0, unroll=DMA_UNROLL)


def _gather_body(tok_ref, s0_ref, end_ref, nu_ref, src_ref, o_ref, buf_ref, sem):
    i = pl.program_id(0)
    tmr = o_ref.shape[0]
    nch = buf_ref.shape[2]
    slot = lax.rem(i, 2)
    n_live = jnp.minimum(nu_ref[0], pl.num_programs(0))

    def tokens_of(tile):
        s0 = s0_ref[tile]
        last = end_ref[tile] - 1
        return lambda r: tok_ref[jnp.minimum(s0 + r, last)]

    rows = _RowGather(src_ref, lambda s, r: buf_ref.at[s, r], sem, tmr)

    @pl.when((i == 0) & (n_live > 0))
    def _():
        rows.start(tokens_of(0), 0)

    @pl.when(i + 1 < n_live)
    def _():
        rows.start(tokens_of(i + 1), 1 - slot)

    @pl.when(i < n_live)
    def _():
        rows.wait(tokens_of(i), slot)
        for r0 in range(0, tmr, 8):
            blk = pltpu.einshape("rcl->crl", buf_ref[slot, r0:r0 + 8])
            for c in range(nch):
                o_ref[r0:r0 + 8, c * ROW_CH:(c + 1) * ROW_CH] = blk[c].astype(o_ref.dtype)

    @pl.when(i >= n_live)
    def _():
        o_ref[...] = jnp.zeros(o_ref.shape, o_ref.dtype)


def _gather_tokens(src3, tok_sorted, blk_s0, blk_end, n_used, tmr):
    _, nch, chw = src3.shape
    nt = blk_s0.shape[0]
    gs = pltpu.PrefetchScalarGridSpec(
        num_scalar_prefetch=4,
        grid=(nt,),
        in_specs=[pl.BlockSpec(memory_space=pl.ANY)],
        out_specs=pl.BlockSpec((tmr, nch * chw), lambda i, *_: (i, 0)),
        scratch_shapes=[pltpu.VMEM((2, tmr, nch, chw), src3.dtype), pltpu.SemaphoreType.DMA((2,))])
    return pl.pallas_call(
        _gather_body, grid_spec=gs,
        out_shape=jax.ShapeDtypeStruct((nt * tmr, nch * chw), bf16),
        compiler_params=_cparams(1),
        name="moe_gather",
    )(tok_sorted, blk_s0, blk_end, n_used, src3)


def _expert_changed(be_ref, i):
    return (i == 0) | (be_ref[i] != be_ref[jnp.maximum(i - 1, 0)])


def _up_body(be_ref, nu_ref, x_ref, wg_ref, wl_ref, bg_ref, bl_ref, o_ref, wgs_ref, wls_ref):
    i = pl.program_id(1)
    used = i < nu_ref[0]

    @pl.when(used & _expert_changed(be_ref, i))
    def _():
        wgs_ref[...] = wg_ref[0].astype(bf16)
        wls_ref[...] = wl_ref[0].astype(bf16)

    @pl.when(used)
    def _():
        x = x_ref[...]
        glu = jnp.dot(x, wgs_ref[...], preferred_element_type=f32) + bg_ref[0]
        lin = jnp.dot(x, wls_ref[...], preferred_element_type=f32) + bl_ref[0]
        glu = jnp.minimum(glu, SWIGLU_LIMIT)
        lin = jnp.clip(lin, -SWIGLU_LIMIT, SWIGLU_LIMIT)
        act = glu * (1.0 / (1.0 + jnp.exp(-SWIGLU_ALPHA * glu))) * (lin + 1.0)
        o_ref[...] = act.astype(bf16)

    @pl.when(jnp.logical_not(used))
    def _():
        o_ref[...] = jnp.zeros(o_ref.shape, o_ref.dtype)


def _moe_up(blk_e, n_used, xs, w_up, b_up3, tn=384):
    n_rows, d = xs.shape
    nb = n_rows // MOE_TM
    nj = D_EXPERT // tn
    gs = pltpu.PrefetchScalarGridSpec(
        num_scalar_prefetch=2,
        grid=(nj, nb),
        in_specs=[pl.BlockSpec((MOE_TM, d), lambda j, i, be, nu: (i, 0)),
                  pl.BlockSpec((1, d, tn), lambda j, i, be, nu: (be[i], 0, j)),
                  pl.BlockSpec((1, d, tn), lambda j, i, be, nu: (be[i], 0, nj + j)),
                  pl.BlockSpec((1, 1, tn), lambda j, i, be, nu: (be[i], 0, j)),
                  pl.BlockSpec((1, 1, tn), lambda j, i, be, nu: (be[i], 0, nj + j))],
        out_specs=pl.BlockSpec((MOE_TM, tn), lambda j, i, be, nu: (i, j)),
        scratch_shapes=[pltpu.VMEM((d, tn), bf16), pltpu.VMEM((d, tn), bf16)])
    return pl.pallas_call(
        _up_body, grid_spec=gs,
        out_shape=jax.ShapeDtypeStruct((n_rows, D_EXPERT), bf16),
        compiler_params=_cparams(2),
        name="moe_up",
    )(blk_e, n_used, xs, w_up, w_up, b_up3, b_up3)


def _down_body(be_ref, nu_ref, a_ref, w_ref, b_ref, o_ref, ws_ref):
    i = pl.program_id(1)
    used = i < nu_ref[0]

    @pl.when(used & _expert_changed(be_ref, i))
    def _():
        ws_ref[...] = w_ref[0].astype(bf16)

    @pl.when(used)
    def _():
        y = jnp.dot(a_ref[...], ws_ref[...], preferred_element_type=f32) + b_ref[0]
        nch = o_ref.shape[1]
        for r0 in range(0, o_ref.shape[0], 8):
            blk = jnp.stack([y[r0:r0 + 8, c * ROW_CH:(c + 1) * ROW_CH] for c in range(nch)], axis=0)
            o_ref[r0:r0 + 8] = pltpu.einshape("crl->rcl", blk)

    @pl.when(jnp.logical_not(used))
    def _():
        o_ref[...] = jnp.zeros(o_ref.shape, o_ref.dtype)


def _moe_down(blk_e, n_used, act, w_down, b_down3, tn=1024):
    n_rows, de = act.shape
    d = w_down.shape[2]
    nb = n_rows // MOE_TM
    nj = d // tn
    gs = pltpu.PrefetchScalarGridSpec(
        num_scalar_prefetch=2,
        grid=(nj, nb),
        in_specs=[pl.BlockSpec((MOE_TM, de), lambda j, i, be, nu: (i, 0)),
                  pl.BlockSpec((1, de, tn), lambda j, i, be, nu: (be[i], 0, j)),
                  pl.BlockSpec((1, 1, tn), lambda j, i, be, nu: (be[i], 0, j))],
        out_specs=pl.BlockSpec((MOE_TM, tn // ROW_CH, ROW_CH), lambda j, i, be, nu: (i, j, 0)),
        scratch_shapes=[pltpu.VMEM((de, tn), bf16)])
    return pl.pallas_call(
        _down_body, grid_spec=gs,
        out_shape=jax.ShapeDtypeStruct((n_rows, d // ROW_CH, ROW_CH), f32),
        compiler_params=_cparams(2),
        name="moe_down",
    )(blk_e, n_used, act, w_down, b_down3)


def _combine_body(cur_ref, nxt_ref, w_ref, ys_ref, x_ref, o_ref, buf_ref, tot_ref, sem):
    i = pl.program_id(0)
    nt = pl.num_programs(0)
    tm = o_ref.shape[0]
    nch = buf_ref.shape[3]
    slot = lax.rem(i, 2)
    rows = _RowGather(
        ys_ref, lambda s, a: buf_ref.at[s, a & (TOP_K - 1), lax.shift_right_logical(a, 2)],
        sem, tm * TOP_K)
    cur = lambda a: cur_ref[0, 0, a]

    @pl.when(i == 0)
    def _():
        rows.start(cur, 0)

    @pl.when(i + 1 < nt)
    def _():
        rows.start(lambda a: nxt_ref[0, 0, a], 1 - slot)

    rows.wait(cur, slot)

    def weigh(t, c):
        tot = buf_ref[slot, 0, t] * w_ref[0, 0, TOP_K * t]
        for k in range(1, TOP_K):
            tot = tot + buf_ref[slot, k, t] * w_ref[0, 0, TOP_K * t + k]
        tot_ref[t] = tot
        return c

    lax.fori_loop(0, tm, weigh, 0, unroll=4)
    for r0 in range(0, tm, 8):
        blk = pltpu.einshape("rcl->crl", tot_ref[r0:r0 + 8])
        for c in range(nch):
            cs = slice(c * ROW_CH, (c + 1) * ROW_CH)
            o_ref[r0:r0 + 8, cs] = x_ref[r0:r0 + 8, cs] + blk[c]


def _combine(ys3, pos, top_w, x1, tm=128):
    n, d = x1.shape
    _, nch, chw = ys3.shape
    nt = n // tm
    pos3 = pos.reshape(nt, 1, tm * TOP_K)
    idx_spec = lambda f: pl.BlockSpec((1, 1, tm * TOP_K), f, memory_space=pltpu.SMEM)
    return pl.pallas_call(
        _combine_body,
        grid=(nt,),
        in_specs=[idx_spec(lambda i: (i, 0, 0)),
                  idx_spec(lambda i: (jnp.minimum(i + 1, nt - 1), 0, 0)),
                  idx_spec(lambda i: (i, 0, 0)),
                  pl.BlockSpec(memory_space=pl.ANY),
                  pl.BlockSpec((tm, d), lambda i: (i, 0))],
        out_specs=pl.BlockSpec((tm, d), lambda i: (i, 0)),
        out_shape=jax.ShapeDtypeStruct((n, d), f32),
        scratch_shapes=[pltpu.VMEM((2, TOP_K, tm, nch, chw), f32), pltpu.VMEM((tm, nch, chw), f32),
                        pltpu.SemaphoreType.DMA((2,))],
        compiler_params=_cparams(1),
        name="moe_combine",
    )(pos3, pos3, top_w.reshape(nt, 1, tm * TOP_K), ys3, x1)


def _route(top_idx):
    n = top_idx.shape[0]
    n_assign = n * TOP_K
    e_flat = top_idx.reshape(-1)
    a_iota = jnp.arange(n_assign, dtype=i32)
    _, order = lax.sort((e_flat, a_iota), num_keys=1, is_stable=True)
    _, rank = lax.sort((order, a_iota), num_keys=1)
    experts = jnp.arange(N_EXPERTS, dtype=i32)
    onehot = experts[:, None] == e_flat[None, :]
    counts = jnp.sum(onehot.astype(i32), axis=1)
    padded = (counts + MOE_TM - 1) // MOE_TM * MOE_TM
    start = jnp.cumsum(counts) - counts
    pend = jnp.cumsum(padded)
    shift = pend - padded - start
    pos = rank + jnp.sum(jnp.where(onehot, shift[:, None], 0), axis=0)
    n_blocks = -(-n_assign // MOE_TM) + N_EXPERTS
    n_used = pend[-1] // MOE_TM
    bi = jnp.arange(n_blocks, dtype=i32)
    raw_e = jnp.minimum(jnp.sum((pend[None, :] <= (bi * MOE_TM)[:, None]).astype(i32), axis=1),
                        N_EXPERTS - 1)
    blk_e = jnp.where(bi < n_used, raw_e, raw_e[jnp.maximum(n_used - 1, 0)]).astype(i32)
    blk_s0 = (bi * MOE_TM - shift[blk_e]).astype(i32)
    blk_end = (start[blk_e] + counts[blk_e]).astype(i32)
    tok_sorted = lax.shift_right_logical(order, 2)
    return tok_sorted, blk_e, blk_s0, blk_end, n_used.reshape(1).astype(i32), pos.astype(i32)


NSA_TQ = 256


def _layer(x, ln1_g, w_in, conv_w, conv_b, dt_bias, a_log, d_skip, ssd_norm_g, q_norm_g,
           k_norm_g, cmp_pe, cmp_w1, cmp_w2, w_out, ln2_g, w_router, b_router, w_up, b_up,
           w_down, b_down):
    bsz, t_len, d = x.shape
    n = bsz * t_len
    x2 = x.reshape(n, d)

    o_dt = SSD_D_INNER + SSD_D_INNER + 2 * SSD_GN
    o_q = o_dt + SSD_HEADS
    o_kv = o_q + NSA_WIDTH
    o_g = o_kv + 6 * NSA_KV_WIDTH
    n_in = o_g + 3 * NSA_HEADS
    w_perm = jnp.concatenate(
        [w_in[:, :o_dt], w_in[:, o_q:o_g], w_in[:, o_dt:o_q], w_in[:, o_g:n_in],
         jnp.zeros((d, NP_PROJ - n_in), w_in.dtype)], axis=1).astype(bf16)

    proj = _inproj(x2, ln1_g.reshape(1, d), w_perm)

    lane_pad = lambda v: jnp.concatenate([v, jnp.zeros((LANE - v.shape[0],), f32)]).reshape(1, LANE)
    y_ssd = _ssd(proj, bsz, t_len, conv_w, conv_b.reshape(1, -1), lane_pad(dt_bias),
                 lane_pad(a_log), jnp.repeat(d_skip, SSD_HEAD_DIM).reshape(1, -1),
                 ssd_norm_g.reshape(1, -1))

    kc, vct = _compress(proj, bsz, t_len, cmp_pe, cmp_w1.astype(bf16), cmp_w2.astype(bf16),
                        k_norm_g[0:1])
    qt, ks, vst, kw, vwt = _qk_prep(proj, q_norm_g.reshape(1, DH), k_norm_g, bsz, t_len, NSA_TQ)
    gates = proj[:, COL_DT + GATE_OFF:COL_DT + GATE_OFF + 3 * NSA_HEADS]
    gates = gates.reshape(n, NSA_KV_HEADS, 3 * NSA_GQA).transpose(1, 2, 0)
    y_nsa = _nsa_attention(qt, kc, vct, ks, vst, kw, vwt, gates, bsz, t_len, NSA_TQ)

    wo = w_out.astype(bf16)
    wr = jnp.concatenate([w_router, jnp.zeros((d, LANE - N_EXPERTS), f32)], axis=1)
    wr_hi = wr.astype(bf16)
    wr = jnp.concatenate([wr_hi, (wr - wr_hi.astype(f32)).astype(bf16)], axis=1)
    br =jnp.concatenate([b_router, jnp.zeros((LANE - N_EXPERTS,), f32)]).reshape(1, LANE)
    x1, h3, ti, tw = _outproj_router(y_ssd, y_nsa, wo[:SSD_D_INNER], wo[SSD_D_INNER:], x2,
                                     ln2_g.reshape(1, d), wr, br)

    tok_sorted, blk_e, blk_s0, blk_end, n_used, pos = _route(ti[:, :TOP_K])
    xs = _gather_tokens(h3, tok_sorted, blk_s0, blk_end, n_used, MOE_TM)
    act = _moe_up(blk_e, n_used, xs, w_up, b_up.reshape(N_EXPERTS, 1, -1))
    ys3 = _moe_down(blk_e, n_used, act, w_down, b_down.reshape(N_EXPERTS, 1, -1))
    out = _combine(ys3, pos, tw[:, :TOP_K], x1)
    return out.reshape(bsz, t_len, d)


def kernel(x, ln1_g, w_in, conv_w, conv_b, dt_bias, a_log, d_skip, ssd_norm_g, q_norm_g,
           k_norm_g, cmp_pe, cmp_w1, cmp_w2, w_out, ln2_g, w_router, b_router, w_up, b_up,
           w_down, b_down):
    return _layer(x, ln1_g[0], w_in[0], conv_w[0], conv_b[0], dt_bias[0], a_log[0], d_skip[0],
                  ssd_norm_g[0], q_norm_g[0], k_norm_g[0], cmp_pe[0], cmp_w1[0], cmp_w2[0],
                  w_out[0], ln2_g[0], w_router[0], b_router[0], w_up[0], b_up[0], w_down[0],
                  b_down[0])
```

```python
import functools
import math

import jax
import jax.numpy as jnp
from jax import lax
from jax.experimental import pallas as pl
from jax.experimental.pallas import tpu as pltpu

f32 = jnp.float32
bf16 = jnp.bfloat16
i32 = jnp.int32

D_MODEL = 4096
SSD_D_INNER = 2048
SSD_HEAD_DIM = 64
SSD_HEADS = 32
SSD_GROUPS = 8
SSD_HPG = 4
SSD_STATE = 128
SSD_CONV = 4
SSD_CHUNK = 256
SSD_GN = SSD_GROUPS * SSD_STATE
SSD_GW = SSD_HPG * SSD_HEAD_DIM
NSA_HEADS = 16
NSA_KV_HEADS = 4
NSA_GQA = 4
DH = 128
NSA_WIDTH = 2048
NSA_KV_WIDTH = 512
CMP_BLOCK = 32
CMP_STRIDE = 16
CMP_HIDDEN = 256
SLC_BLOCK = 64
SLC_TOPK = 16
WINDOW = 512
N_EXPERTS = 32
TOP_K = 4
D_EXPERT = 1536
SWIGLU_LIMIT = 7.0
SWIGLU_ALPHA = 1.702
EPS = 1e-5
NEG = -1e30
LOG2E = 1.4426950408889634
SLC_GROUP = 4
VT_PAD = 16

COL_Z = 0
COL_X = 2048
COL_B = 4096
COL_C = 5120
COL_Q = 6144
COL_KV = 8192
COL_DT = 11264
GATE_OFF = 32
NP_PROJ = 11520

LANE = 128
VMEM_LIMIT = 56 * 1024 * 1024


def _cparams(n_axes):
    return pltpu.CompilerParams(dimension_semantics=("arbitrary",) * n_axes,
                                vmem_limit_bytes=VMEM_LIMIT)


def _silu(v):
    return v * (1.0 / (1.0 + jnp.exp(-v)))


def _inproj_body(x_ref, g_ref, w_ref, o_ref, h_ref):
    @pl.when(pl.program_id(1) == 0)
    def _():
        x = x_ref[...]
        ms = jnp.mean(x * x, axis=-1, keepdims=True)
        h_ref[...] = (x * lax.rsqrt(ms + EPS) * g_ref[...]).astype(bf16)

    o_ref[...] = jnp.dot(h_ref[...], w_ref[...], preferred_element_type=f32)


def _inproj(x2, g, w, tm=512, tn=768):
    n, d = x2.shape
    npj = w.shape[1]
    return pl.pallas_call(
        _inproj_body,
        grid=(n // tm, npj // tn),
        in_specs=[pl.BlockSpec((tm, d), lambda i, j: (i, 0)),
                  pl.BlockSpec((1, d), lambda i, j: (0, 0)),
                  pl.BlockSpec((d, tn), lambda i, j: (0, j))],
        out_specs=pl.BlockSpec((tm, tn), lambda i, j: (i, j)),
        out_shape=jax.ShapeDtypeStruct((n, npj), f32),
        scratch_shapes=[pltpu.VMEM((tm, d), bf16)],
        compiler_params=_cparams(2),
        name="inproj",
    )(x2, g, w)


def _lane_expand(cols, width):
    L = cols[0].shape[0]
    n = len(cols)
    lane = lax.broadcasted_iota(i32, (L, n * width), 1)
    out = jnp.broadcast_to(cols[n - 1], (L, n * width))
    for j in range(n - 2, -1, -1):
        out = jnp.where(lane < (j + 1) * width, jnp.broadcast_to(cols[j], (L, n * width)), out)
    return out


def _ssd_body(z_ref, x_ref, b_ref, c_ref, dt_ref, cw_ref, cb_ref, dtb_ref, alog_ref,
              dskip_ref, ng_ref, y_ref, buf_ref, xs_ref, bt_ref, cs_ref, h_ref):
    L = SSD_CHUNK
    c_idx = pl.program_id(1)

    @pl.when(c_idx == 0)
    def _():
        buf_ref[0:8, :] = jnp.zeros((8, buf_ref.shape[1]), f32)
        h_ref[...] = jnp.zeros(h_ref.shape, f32)

    buf_ref[8:8 + L, 0:SSD_D_INNER] = x_ref[...]
    buf_ref[8:8 + L, SSD_D_INNER:SSD_D_INNER + SSD_GN] = b_ref[...]
    buf_ref[8:8 + L, SSD_D_INNER + SSD_GN:] = c_ref[...]

    cw = 512
    n_ch = buf_ref.shape[1]
    for c0 in range(0, n_ch, cw):
        acc = jnp.broadcast_to(cb_ref[:, c0:c0 + cw], (L, cw))
        for k in range(SSD_CONV):
            acc = acc + cw_ref[k:k + 1, c0:c0 + cw] * buf_ref[5 + k:5 + k + L, c0:c0 + cw]
        v = _silu(acc)
        if c0 < SSD_D_INNER:
            xs_ref[:, c0:c0 + cw] = v
        elif c0 < SSD_D_INNER + SSD_GN:
            bt_ref[c0 - SSD_D_INNER:c0 - SSD_D_INNER + cw, :] = v.T.astype(bf16)
        else:
            o = c0 - SSD_D_INNER - SSD_GN
            cs_ref[:, o:o + cw] = v.astype(bf16)
    buf_ref[0:8, :] = buf_ref[L:L + 8, :]

    dt = dt_ref[...] + dtb_ref[...]
    dt = jnp.maximum(dt, 0.0) + jnp.log(1.0 + jnp.exp(-jnp.abs(dt)))
    da = dt * -jnp.exp(alog_ref[...])
    r = lax.broadcasted_iota(i32, (L, L), 0)
    s = lax.broadcasted_iota(i32, (L, L), 1)
    tri = (s <= r).astype(f32)
    acs = jnp.dot(tri, da, preferred_element_type=f32,
                  precision=lax.Precision.HIGHEST)
    acs_t = acs.T
    causal = s <= r
    a_last = acs[L - 1:L, :]
    e_acs = jnp.exp(acs)
    e_rem = jnp.exp(a_last - acs)
    e_last = jnp.exp(a_last)
    lane_gw = lax.broadcasted_iota(i32, (L, SSD_GW), 1)

    for g in range(SSD_GROUPS):
        hs = [g * SSD_HPG + j for j in range(SSD_HPG)]
        xg = xs_ref[:, g * SSD_GW:(g + 1) * SSD_GW]
        bgt = bt_ref[g * SSD_STATE:(g + 1) * SSD_STATE, :]
        cg = cs_ref[:, g * SSD_STATE:(g + 1) * SSD_STATE]
        dt_x = _lane_expand([dt[:, h:h + 1] for h in hs], SSD_HEAD_DIM)
        xdt = xg * dt_x
        cbm = jnp.dot(cg, bgt, preferred_element_type=f32)
        y = jnp.zeros((L, SSD_GW), f32)
        for j, h in enumerate(hs):
            seg = jnp.where(causal, acs[:, h:h + 1] - acs_t[h:h + 1, :], NEG)
            w = (cbm * jnp.exp(seg)).astype(bf16)
            band = (lane_gw >= j * SSD_HEAD_DIM) & (lane_gw < (j + 1) * SSD_HEAD_DIM)
            xm = jnp.where(band, xdt, 0.0).astype(bf16)
            y = y + jnp.dot(w, xm, preferred_element_type=f32)
        hprev = h_ref[g]
        y_off = jnp.dot(cg, hprev.astype(bf16), preferred_element_type=f32)
        y = y + y_off * _lane_expand([e_acs[:, h:h + 1] for h in hs], SSD_HEAD_DIM)
        xw = (xdt * _lane_expand([e_rem[:, h:h + 1] for h in hs], SSD_HEAD_DIM)).astype(bf16)
        st = jnp.dot(bgt, xw, preferred_element_type=f32)
        dec = _lane_expand([e_last[:, h:h + 1] for h in hs], SSD_HEAD_DIM)
        h_ref[g] = hprev * dec + st
        y = y + xg * dskip_ref[:, g * SSD_GW:(g + 1) * SSD_GW]
        y = y * _silu(z_ref[:, g * SSD_GW:(g + 1) * SSD_GW])
        y = y * lax.rsqrt(jnp.mean(y * y, axis=-1, keepdims=True) + EPS)
        y_ref[:, g * SSD_GW:(g + 1) * SSD_GW] = (y * ng_ref[:, g * SSD_GW:(g + 1) * SSD_GW]).astype(bf16)


def _ssd(proj, bsz, t_len, conv_w, conv_b, dt_bias, a_log, dskip_x, norm_g):
    L = SSD_CHUNK
    nc = t_len // L
    n = bsz * t_len
    nch = SSD_D_INNER + 2 * SSD_GN
    row = lambda b, c: b * nc + c
    full = lambda shape: pl.BlockSpec(shape, lambda b, c: (0,) * len(shape))
    return pl.pallas_call(
        _ssd_body,
        grid=(bsz, nc),
        in_specs=[pl.BlockSpec((L, SSD_D_INNER), lambda b, c: (row(b, c), COL_Z // SSD_D_INNER)),
                  pl.BlockSpec((L, SSD_D_INNER), lambda b, c: (row(b, c), COL_X // SSD_D_INNER)),
                  pl.BlockSpec((L, SSD_GN), lambda b, c: (row(b, c), COL_B // SSD_GN)),
                  pl.BlockSpec((L, SSD_GN), lambda b, c: (row(b, c), COL_C // SSD_GN)),
                  pl.BlockSpec((L, LANE), lambda b, c: (row(b, c), COL_DT // LANE)),
                  full((SSD_CONV, nch)), full((1, nch)), full((1, LANE)),
                  full((1, LANE)), full((1, SSD_D_INNER)), full((1, SSD_D_INNER))],
        out_specs=pl.BlockSpec((L, SSD_D_INNER), lambda b, c: (row(b, c), 0)),
        out_shape=jax.ShapeDtypeStruct((n, SSD_D_INNER), bf16),
        scratch_shapes=[pltpu.VMEM((L + 8, nch), f32),
                        pltpu.VMEM((L, SSD_D_INNER), f32),
                        pltpu.VMEM((SSD_GN, L), bf16),
                        pltpu.VMEM((L, SSD_GN), bf16),
                        pltpu.VMEM((SSD_GROUPS, SSD_STATE, SSD_GW), f32)],
        compiler_params=_cparams(2),
        name="ssd",
    )(proj, proj, proj, proj, proj, conv_w, conv_b, dt_bias, a_log, dskip_x, norm_g)


def _cmp_body(uk_ref, uv_ref, pe_ref, w1_ref, w2_ref, g_ref, kc_ref, vct_ref):
    nchunk = uk_ref.shape[0]

    def branch(u_ref, kv):
        hid_a = jnp.zeros((nchunk, CMP_HIDDEN), f32)
        hid_b = jnp.zeros((nchunk, CMP_HIDDEN), f32)
        ut = pltpu.einshape("csd->scd", u_ref[...])
        for s in range(CMP_STRIDE):
            us = ut[s]
            hid_a = hid_a + jnp.dot((us + pe_ref[kv, s:s + 1, :]).astype(bf16), w1_ref[kv, s],
                                    preferred_element_type=f32)
            hid_b = hid_b + jnp.dot(
                (us + pe_ref[kv, CMP_STRIDE + s:CMP_STRIDE + s + 1, :]).astype(bf16),
                w1_ref[kv, CMP_STRIDE + s], preferred_element_type=f32)
        hid = hid_a + pltpu.roll(hid_b, nchunk - 1, 0)
        out = jnp.dot(_silu(hid).astype(bf16), w2_ref[kv], preferred_element_type=f32)
        rowi = lax.broadcasted_iota(i32, out.shape, 0)
        return jnp.where(rowi == nchunk - 1, 0.0, out)

    k = branch(uk_ref, 0)
    k = k * lax.rsqrt(jnp.mean(k * k, axis=-1, keepdims=True) + EPS) * g_ref[...]
    kc_ref[0, 0] = k.astype(bf16)
    vct_ref[0, 0] = branch(uv_ref, 1).T.astype(bf16)


def _compress(proj, bsz, t_len, pe, w1, w2, kg):
    nchunk = t_len // CMP_STRIDE
    H = NSA_KV_HEADS
    u3 = proj.reshape(bsz * nchunk, CMP_STRIDE, proj.shape[1])
    useg = lambda kv: pl.BlockSpec((nchunk, CMP_STRIDE, DH),
                                   lambda b, h: (b, 0, COL_KV // DH + kv * H + h))
    full = lambda shape: pl.BlockSpec(shape, lambda b, h: (0,) * len(shape))
    return pl.pallas_call(
        _cmp_body,
        grid=(bsz, H),
        in_specs=[useg(0), useg(1), full((2, CMP_BLOCK, DH)), full((2, CMP_BLOCK, DH, CMP_HIDDEN)),
                  full((2, CMP_HIDDEN, DH)), full((1, DH))],
        out_specs=[pl.BlockSpec((1, 1, nchunk, DH), lambda b, h: (b, h, 0, 0)),
                   pl.BlockSpec((1, 1, DH, nchunk), lambda b, h: (b, h, 0, 0))],
        out_shape=[jax.ShapeDtypeStruct((bsz, H, nchunk, DH), bf16),
                   jax.ShapeDtypeStruct((bsz, H, DH, nchunk), bf16)],
        compiler_params=_cparams(2),
        name="nsa_compress",
    )(u3, u3, pe, w1, w2, kg)


def _qk_prep_body(q_ref, s_ref, w_ref, qg_ref, kg_ref, qt_ref, ks_ref, vst_ref, kw_ref, vwt_ref):
    scale = DH ** -0.5 * LOG2E
    tq = q_ref.shape[0]

    def hnorm(v, g):
        return v * lax.rsqrt(jnp.mean(v * v, axis=-1, keepdims=True) + EPS) * g

    ones_rows = (lax.broadcasted_iota(i32, (VT_PAD, tq), 0) == 0).astype(bf16)
    for h in range(NSA_KV_HEADS):
        for g in range(NSA_GQA):
            sl = slice((h * NSA_GQA + g) * DH, (h * NSA_GQA + g + 1) * DH)
            qn = hnorm(q_ref[:, sl], qg_ref[...]) * scale
            qt_ref[0, h, 0, :, g * tq:(g + 1) * tq] = qn.T.astype(bf16)
        sl = slice(h * DH, (h + 1) * DH)
        sv = slice(NSA_KV_WIDTH + h * DH, NSA_KV_WIDTH + (h + 1) * DH)
        ks_ref[:, sl] = hnorm(s_ref[:, sl], kg_ref[1:2, :]).astype(bf16)
        kw_ref[:, sl] = hnorm(w_ref[:, sl], kg_ref[2:3, :]).astype(bf16)
        vst_ref[0, h, 0, 0:DH] = s_ref[:, sv].T.astype(bf16)
        vwt_ref[0, h, 0, 0:DH] = w_ref[:, sv].T.astype(bf16)
        vst_ref[0, h, 0, DH:DH + VT_PAD] = ones_rows
        vwt_ref[0, h, 0, DH:DH + VT_PAD] = ones_rows


def _qk_prep(proj, qg, kg, bsz, t_len, tq):
    n = proj.shape[0]
    nq = t_len // tq
    H = NSA_KV_HEADS
    w2 = 2 * NSA_KV_WIDTH
    row = lambda b, i: b * nq + i
    vt_spec = pl.BlockSpec((1, H, 1, DH + VT_PAD, tq), lambda b, i: (b, 0, i, 0, 0))
    vt_shape = jax.ShapeDtypeStruct((bsz, H, nq, DH + VT_PAD, tq), bf16)
    return pl.pallas_call(
        _qk_prep_body,
        grid=(bsz, nq),
        in_specs=[pl.BlockSpec((tq, NSA_WIDTH), lambda b, i: (row(b, i), COL_Q // NSA_WIDTH)),
                  pl.BlockSpec((tq, w2), lambda b, i: (row(b, i), (COL_KV + w2) // w2)),
                  pl.BlockSpec((tq, w2), lambda b, i: (row(b, i), (COL_KV + 2 * w2) // w2)),
                  pl.BlockSpec((1, DH), lambda b, i: (0, 0)),
                  pl.BlockSpec((3, DH), lambda b, i: (0, 0))],
        out_specs=[pl.BlockSpec((1, H, 1, DH, NSA_GQA * tq), lambda b, i: (b, 0, i, 0, 0)),
                   pl.BlockSpec((tq, NSA_KV_WIDTH), lambda b, i: (row(b, i), 0)),
                   vt_spec,
                   pl.BlockSpec((tq, NSA_KV_WIDTH), lambda b, i: (row(b, i), 0)),
                   vt_spec],
        out_shape=[jax.ShapeDtypeStruct((bsz, H, nq, DH, NSA_GQA * tq), bf16),
                   jax.ShapeDtypeStruct((n, NSA_KV_WIDTH), bf16), vt_shape,
                   jax.ShapeDtypeStruct((n, NSA_KV_WIDTH), bf16), vt_shape],
        compiler_params=_cparams(2),
        name="nsa_qk_prep",
    )(proj, proj, proj, qg, kg)


def _nsa_body(qt_ref, kc_ref, vct_ref, ks_ref, vst_ref, kw_ref, vwt_ref, gt_ref, o_ref,
              m_ref, acc_ref, sel_ref, *, tq):
    i = pl.program_id(2)
    G = NSA_GQA
    n_cmp = kc_ref.shape[2]
    n_slc = ks_ref.shape[0] // SLC_BLOCK

    qt = qt_ref[0, 0, 0]
    t_row = i * tq + lax.broadcasted_iota(i32, (1, tq), 1)
    t_all = jnp.concatenate([t_row] * G, axis=1)

    s_c = jnp.dot(kc_ref[0, 0], qt, preferred_element_type=f32)
    cend = lax.broadcasted_iota(i32, (n_cmp, 1), 0) * CMP_STRIDE + (CMP_BLOCK - 1)
    m_c = cend <= t_all
    s_c = jnp.where(m_c, s_c, NEG)
    p_c = jnp.where(m_c, jnp.exp2(s_c - jnp.max(s_c, axis=0, keepdims=True)), 0.0)
    p_c = p_c / jnp.maximum(jnp.sum(p_c, axis=0, keepdims=True), 1e-30)
    o_c = jnp.dot(vct_ref[0, 0], p_c.astype(bf16), preferred_element_type=f32)

    imp = p_c[:, 0:tq]
    for g in range(1, G):
        imp = imp + p_c[:, g * tq:(g + 1) * tq]
    per = SLC_BLOCK // CMP_STRIDE
    ni = lax.broadcasted_iota(i32, (n_slc, n_cmp), 0)
    ci = lax.broadcasted_iota(i32, (n_slc, n_cmp), 1)
    fold = ((ci // per == ni).astype(f32) + ((ci + 1) // per == ni).astype(f32))
    imp_b = jnp.dot(fold, imp, preferred_element_type=f32, precision=lax.Precision.HIGHEST)
    jb = lax.broadcasted_iota(i32, (n_slc, tq), 0)
    cur = t_row // SLC_BLOCK
    forced = (jb == 0) | (jb == cur) | (jb == cur - 1)
    val = jnp.where(forced, 1e30, jnp.where(jb <= cur, imp_b, -1.0))
    sel = jnp.zeros((n_slc, tq), f32)
    for _ in range(SLC_TOPK):
        mx = jnp.max(val, axis=0, keepdims=True)
        first = jnp.min(jnp.where(val == mx, jb, n_slc), axis=0, keepdims=True)
        hit = (jb == first) & (mx >= 0.0)
        sel = jnp.where(hit, 1.0, sel)
        val = jnp.where(jb == first, -1.0, val)

    def online(tiles, slot):
        scores = [jnp.dot(k, qt, preferred_element_type=f32) + jnp.concatenate([bias] * G, axis=1)
                  for k, _, bias in tiles]
        m_old = m_ref[slot]
        m_new = m_old
        for s_ in scores:
            m_new = jnp.maximum(m_new, jnp.max(s_, axis=0, keepdims=True))
        acc = jnp.exp2(m_old - m_new) * acc_ref[slot]
        for s_, (_, vt, _) in zip(scores, tiles):
            acc = acc + jnp.dot(vt, jnp.exp2(s_ - m_new).astype(bf16), preferred_element_type=f32)
        acc_ref[slot] = acc
        m_ref[slot] = m_new

    m_ref[...] = jnp.full(m_ref.shape, NEG, f32)
    acc_ref[...] = jnp.zeros(acc_ref.shape, f32)
    krow = lax.broadcasted_iota(i32, (tq, 1), 0)
    kcol = lax.broadcasted_iota(i32, (1, tq), 1)

    sel_ref[...] = jnp.where(sel > 0.5, 0.0, NEG)
    bpt = tq // SLC_BLOCK

    def block_bias(kt):
        rows = [jnp.broadcast_to(sel_ref[pl.ds(kt * bpt + j, 1), :], (SLC_BLOCK, tq))
                for j in range(bpt)]
        return jnp.concatenate(rows, axis=0)

    def slc_tile(kt, causal=False):
        bias = block_bias(kt)
        if causal:
            bias = jnp.where(krow <= kcol, bias, NEG)
        return ks_ref[pl.ds(pl.multiple_of(kt * tq, tq), tq), :], vst_ref[0, 0, kt], bias

    def slc_group(kg, carry):
        online([slc_tile(SLC_GROUP * kg + j) for j in range(SLC_GROUP)], 0)
        return carry

    n_grp = i // SLC_GROUP
    lax.fori_loop(0, n_grp, slc_group, 0)
    for r in range(SLC_GROUP):
        @pl.when(i - n_grp * SLC_GROUP == r)
        def _():
            online([slc_tile(n_grp * SLC_GROUP + j) for j in range(r)] + [slc_tile(i, causal=True)], 0)

    def win_tile(kt):
        k0 = pl.multiple_of(kt * tq, tq)
        key = k0 + krow
        ok = (key <= t_row) & (key > t_row - WINDOW)
        return kw_ref[pl.ds(k0, tq), :], vwt_ref[0, 0, kt], jnp.where(ok, 0.0, NEG)

    back = WINDOW // tq
    for c in range(back + 1):
        @pl.when(jnp.minimum(i, back) == c)
        def _():
            online([win_tile(i - c + j) for j in range(c + 1)], 1)

    o_s = acc_ref[0, 0:DH] / acc_ref[0, DH:DH + 1]
    o_w = acc_ref[1, 0:DH] / acc_ref[1, DH:DH + 1]
    gts = 1.0 / (1.0 + jnp.exp(-gt_ref[0]))
    for g in range(G):
        cs = slice(g * tq, (g + 1) * tq)
        out_t = (gts[3 * g:3 * g + 1, :] * o_c[:, cs] + gts[3 * g + 1:3 * g + 2, :] * o_s[:, cs]
                 + gts[3 * g + 2:3 * g + 3, :] * o_w[:, cs])
        o_ref[:, g * DH:(g + 1) * DH] = out_t.T.astype(bf16)


def _nsa_attention(qt, kc, vct, ks, vst, kw, vwt, gates, bsz, t_len, tq):
    n = bsz * t_len
    nq = t_len // tq
    nchunk = t_len // CMP_STRIDE
    R = NSA_GQA * tq
    seq = pl.BlockSpec((t_len, DH), lambda b, h, i: (b, h))
    seq_t = pl.BlockSpec((1, 1, nq, DH + VT_PAD, tq), lambda b, h, i: (b, h, 0, 0, 0))
    return pl.pallas_call(
        functools.partial(_nsa_body, tq=tq),
        grid=(bsz, NSA_KV_HEADS, nq),
        in_specs=[pl.BlockSpec((1, 1, 1, DH, R), lambda b, h, i: (b, h, i, 0, 0)),
                  pl.BlockSpec((1, 1, nchunk, DH), lambda b, h, i: (b, h, 0, 0)),
                  pl.BlockSpec((1, 1, DH, nchunk), lambda b, h, i: (b, h, 0, 0)),
                  seq, seq_t, seq, seq_t,
                  pl.BlockSpec((1, 3 * NSA_GQA, tq), lambda b, h, i: (h, 0, b * nq + i))],
        out_specs=pl.BlockSpec((tq, NSA_GQA * DH), lambda b, h, i: (b * nq + i, h)),
        out_shape=jax.ShapeDtypeStruct((n, NSA_WIDTH), bf16),
        scratch_shapes=[pltpu.VMEM((2, 1, R), f32), pltpu.VMEM((2, DH + VT_PAD, R), f32),
                        pltpu.VMEM((t_len // SLC_BLOCK, tq), f32)],
        compiler_params=_cparams(3),
        name="nsa_attention",
    )(qt, kc, vct, ks, vst, kw, vwt, gates)


ROW_CH = LANE


def _outproj_body(ya_ref, yb_ref, wa_ref, wb_ref, x_ref, g_ref, wr_ref, br_ref,
                  x1_ref, h_ref, ti_ref, tw_ref, row_ref, *, tn):
    j = pl.program_id(1)
    nj = pl.num_programs(1)
    acc = x_ref[...] + jnp.dot(ya_ref[...], wa_ref[...], preferred_element_type=f32)
    acc = acc + jnp.dot(yb_ref[...], wb_ref[...], preferred_element_type=f32)
    x1_ref[...] = acc
    row_ref[j] = acc

    @pl.when(j == nj - 1)
    def _():
        tm = x1_ref.shape[0]
        ssq = jnp.zeros((tm, 1), f32)
        for jj in range(row_ref.shape[0]):
            v = row_ref[jj]
            ssq = ssq + jnp.sum(v * v, axis=-1, keepdims=True)
        rinv = lax.rsqrt(ssq / (row_ref.shape[0] * tn) + EPS)
        logits = jnp.broadcast_to(br_ref[...], (tm, LANE))
        for jj in range(row_ref.shape[0]):
            hh = row_ref[jj] * rinv * g_ref[:, jj * tn:(jj + 1) * tn]
            row_ref[jj] = hh
            hi = hh.astype(bf16)
            lo = (hh - hi.astype(f32)).astype(bf16)
            wch = wr_ref[jj * tn:(jj + 1) * tn, :]
            t = jnp.dot(hi, wch, preferred_element_type=f32)
            logits = logits + (t[:, :LANE] + (t[:, LANE:] + jnp.dot(lo, wch[:, :LANE],
                                                                     preferred_element_type=f32)))
        per = tn // ROW_CH
        for r0 in range(0, tm, 8):
            blk = jnp.stack([row_ref[c // per, r0:r0 + 8, (c % per) * ROW_CH:(c % per + 1) * ROW_CH]
                             for c in range(h_ref.shape[1])], axis=0)
            h_ref[r0:r0 + 8] = pltpu.einshape("crl->rcl", blk)
        lane = lax.broadcasted_iota(i32, (tm, LANE), 1)
        val = jnp.where(lane < N_EXPERTS, logits, -jnp.inf)
        idxs = jnp.zeros((tm, LANE), i32)
        vals = jnp.full((tm, LANE), -jnp.inf, f32)
        for k in range(TOP_K):
            mx = jnp.max(val, axis=-1, keepdims=True)
            first = jnp.min(jnp.where(val == mx, lane, LANE), axis=-1, keepdims=True)
            idxs = jnp.where(lane == k, first, idxs)
            vals = jnp.where(lane == k, mx, vals)
            val = jnp.where(lane == first, -jnp.inf, val)
        e = jnp.exp(vals - jnp.max(vals, axis=-1, keepdims=True))
        tw_ref[...] = e / jnp.sum(e, axis=-1, keepdims=True)
        ti_ref[...] = idxs


def _outproj_router(ya, yb, wa, wb, x2, g2, wr, br, tm=256, tn=512):
    n, d = x2.shape
    nj = d // tn
    ka = ya.shape[1]
    kb = yb.shape[1]
    return pl.pallas_call(
        functools.partial(_outproj_body, tn=tn),
        grid=(n // tm, nj),
        in_specs=[pl.BlockSpec((tm, ka), lambda i, j: (i, 0)),
                  pl.BlockSpec((tm, kb), lambda i, j: (i, 0)),
                  pl.BlockSpec((ka, tn), lambda i, j: (0, j)),
                  pl.BlockSpec((kb, tn), lambda i, j: (0, j)),
                  pl.BlockSpec((tm, tn), lambda i, j: (i, j)),
                  pl.BlockSpec((1, d), lambda i, j: (0, 0)),
                  pl.BlockSpec((d, 2 * LANE), lambda i, j: (0, 0)),
                  pl.BlockSpec((1, LANE), lambda i, j: (0, 0))],
        out_specs=[pl.BlockSpec((tm, tn), lambda i, j: (i, j)),
                   pl.BlockSpec((tm, d // ROW_CH, ROW_CH), lambda i, j: (i, 0, 0)),
                   pl.BlockSpec((tm, LANE), lambda i, j: (i, 0)),
                   pl.BlockSpec((tm, LANE), lambda i, j: (i, 0))],
        out_shape=[jax.ShapeDtypeStruct((n, d), f32),
                   jax.ShapeDtypeStruct((n, d // ROW_CH, ROW_CH), f32),
                   jax.ShapeDtypeStruct((n, LANE), i32),
                   jax.ShapeDtypeStruct((n, LANE), f32)],
        scratch_shapes=[pltpu.VMEM((nj, tm, tn), f32)],
        compiler_params=_cparams(2),
        name="outproj_router",
    )(ya, yb, wa, wb, x2, g2, wr, br)


MOE_TM = 512
DMA_UNROLL = 8


class _RowGather:
    def __init__(self, src_ref, dst_of, bulk_of, sem, count):
        self.src_ref, self.dst_of, self.bulk_of, self.sem, self.count = src_ref, dst_of, bulk_of, sem, count

    def start(self, idx_of, slot):
        def body(r, c):
            pltpu.make_async_copy(self.src_ref.at[idx_of(r)], self.dst_of(slot, r),
                                  self.sem.at[slot]).start()
            return c
        lax.fori_loop(0, self.count, body, 0, unroll=DMA_UNROLL)

    def wait(self, slot):
        for src, dst in self.bulk_of(slot):
            pltpu.make_async_copy(src, dst, self.sem.at[slot]).wait()


def _gather_body(tok_ref, s0_ref, end_ref, nu_ref, src_ref, o_ref, buf_ref, sem):
    i = pl.program_id(0)
    tmr = o_ref.shape[0]
    nch = buf_ref.shape[2]
    slot = lax.rem(i, 2)
    n_live = jnp.minimum(nu_ref[0], pl.num_programs(0))

    def tokens_of(tile):
        s0 = s0_ref[tile]
        last = end_ref[tile] - 1
        return lambda r: tok_ref[jnp.minimum(s0 + r, last)]

    rows = _RowGather(src_ref, lambda s, r: buf_ref.at[s, r],
                      lambda s: [(src_ref.at[pl.ds(0, tmr)], buf_ref.at[s])], sem, tmr)

    @pl.when((i == 0) & (n_live > 0))
    def _():
        rows.start(tokens_of(0), 0)

    @pl.when(i + 1 < n_live)
    def _():
        rows.start(tokens_of(i + 1), 1 - slot)

    @pl.when(i < n_live)
    def _():
        rows.wait(slot)
        for r0 in range(0, tmr, 8):
            blk = pltpu.einshape("rcl->crl", buf_ref[slot, r0:r0 + 8])
            for c in range(nch):
                o_ref[r0:r0 + 8, c * ROW_CH:(c + 1) * ROW_CH] = blk[c].astype(o_ref.dtype)

    @pl.when(i >= n_live)
    def _():
        o_ref[...] = jnp.zeros(o_ref.shape, o_ref.dtype)


def _gather_tokens(src3, tok_sorted, blk_s0, blk_end, n_used, tmr):
    _, nch, chw = src3.shape
    nt = blk_s0.shape[0]
    gs = pltpu.PrefetchScalarGridSpec(
        num_scalar_prefetch=4,
        grid=(nt,),
        in_specs=[pl.BlockSpec(memory_space=pl.ANY)],
        out_specs=pl.BlockSpec((tmr, nch * chw), lambda i, *_: (i, 0)),
        scratch_shapes=[pltpu.VMEM((2, tmr, nch, chw), src3.dtype), pltpu.SemaphoreType.DMA((2,))])
    return pl.pallas_call(
        _gather_body, grid_spec=gs,
        out_shape=jax.ShapeDtypeStruct((nt * tmr, nch * chw), bf16),
        compiler_params=_cparams(1),
        name="moe_gather",
    )(tok_sorted, blk_s0, blk_end, n_used, src3)


def _expert_changed(be_ref, i):
    return (i == 0) | (be_ref[i] != be_ref[jnp.maximum(i - 1, 0)])


def _up_body(be_ref, nu_ref, x_ref, wg_ref, wl_ref, bg_ref, bl_ref, o_ref, wgs_ref, wls_ref):
    i = pl.program_id(1)
    used = i < nu_ref[0]

    @pl.when(used & _expert_changed(be_ref, i))
    def _():
        wgs_ref[...] = wg_ref[0].astype(bf16)
        wls_ref[...] = wl_ref[0].astype(bf16)

    @pl.when(used)
    def _():
        x = x_ref[...]
        glu = jnp.dot(x, wgs_ref[...], preferred_element_type=f32) + bg_ref[0]
        lin = jnp.dot(x, wls_ref[...], preferred_element_type=f32) + bl_ref[0]
        glu = jnp.minimum(glu, SWIGLU_LIMIT)
        lin = jnp.clip(lin, -SWIGLU_LIMIT, SWIGLU_LIMIT)
        act = glu * (1.0 / (1.0 + jnp.exp(-SWIGLU_ALPHA * glu))) * (lin + 1.0)
        o_ref[...] = act.astype(bf16)

    @pl.when(jnp.logical_not(used))
    def _():
        o_ref[...] = jnp.zeros(o_ref.shape, o_ref.dtype)


def _moe_up(blk_e, n_used, xs, w_up, b_up3, tn=384):
    n_rows, d = xs.shape
    nb = n_rows // MOE_TM
    nj = D_EXPERT // tn
    gs = pltpu.PrefetchScalarGridSpec(
        num_scalar_prefetch=2,
        grid=(nj, nb),
        in_specs=[pl.BlockSpec((MOE_TM, d), lambda j, i, be, nu: (i, 0)),
                  pl.BlockSpec((1, d, tn), lambda j, i, be, nu: (be[i], 0, j)),
                  pl.BlockSpec((1, d, tn), lambda j, i, be, nu: (be[i], 0, nj + j)),
                  pl.BlockSpec((1, 1, tn), lambda j, i, be, nu: (be[i], 0, j)),
                  pl.BlockSpec((1, 1, tn), lambda j, i, be, nu: (be[i], 0, nj + j))],
        out_specs=pl.BlockSpec((MOE_TM, tn), lambda j, i, be, nu: (i, j)),
        scratch_shapes=[pltpu.VMEM((d, tn), bf16), pltpu.VMEM((d, tn), bf16)])
    return pl.pallas_call(
        _up_body, grid_spec=gs,
        out_shape=jax.ShapeDtypeStruct((n_rows, D_EXPERT), bf16),
        compiler_params=_cparams(2),
        name="moe_up",
    )(blk_e, n_used, xs, w_up, w_up, b_up3, b_up3)


def _down_body(be_ref, nu_ref, a_ref, w_ref, b_ref, o_ref, ws_ref):
    i = pl.program_id(1)
    used = i < nu_ref[0]

    @pl.when(used & _expert_changed(be_ref, i))
    def _():
        ws_ref[...] = w_ref[0].astype(bf16)

    @pl.when(used)
    def _():
        y = jnp.dot(a_ref[...], ws_ref[...], preferred_element_type=f32) + b_ref[0]
        nch = o_ref.shape[1]
        for r0 in range(0, o_ref.shape[0], 8):
            blk = jnp.stack([y[r0:r0 + 8, c * ROW_CH:(c + 1) * ROW_CH] for c in range(nch)], axis=0)
            o_ref[r0:r0 + 8] = pltpu.einshape("crl->rcl", blk)

    @pl.when(jnp.logical_not(used))
    def _():
        o_ref[...] = jnp.zeros(o_ref.shape, o_ref.dtype)


def _moe_down(blk_e, n_used, act, w_down, b_down3, tn=1024):
    n_rows, de = act.shape
    d = w_down.shape[2]
    nb = n_rows // MOE_TM
    nj = d // tn
    gs = pltpu.PrefetchScalarGridSpec(
        num_scalar_prefetch=2,
        grid=(nj, nb),
        in_specs=[pl.BlockSpec((MOE_TM, de), lambda j, i, be, nu: (i, 0)),
                  pl.BlockSpec((1, de, tn), lambda j, i, be, nu: (be[i], 0, j)),
                  pl.BlockSpec((1, 1, tn), lambda j, i, be, nu: (be[i], 0, j))],
        out_specs=pl.BlockSpec((MOE_TM, tn // ROW_CH, ROW_CH), lambda j, i, be, nu: (i, j, 0)),
        scratch_shapes=[pltpu.VMEM((de, tn), bf16)])
    return pl.pallas_call(
        _down_body, grid_spec=gs,
        out_shape=jax.ShapeDtypeStruct((n_rows, d // ROW_CH, ROW_CH), f32),
        compiler_params=_cparams(2),
        name="moe_down",
    )(blk_e, n_used, act, w_down, b_down3)


def _combine_body(cur_ref, nxt_ref, w_ref, ys_ref, x_ref, o_ref, buf_ref, tot_ref, sem):
    i = pl.program_id(0)
    nt = pl.num_programs(0)
    tm = o_ref.shape[0]
    nch = buf_ref.shape[3]
    slot = lax.rem(i, 2)
    rows = _RowGather(
        ys_ref, lambda s, a: buf_ref.at[s, a & (TOP_K - 1), lax.shift_right_logical(a, 2)],
        lambda s: [(ys_ref.at[pl.ds(0, tm)], buf_ref.at[s, k]) for k in range(TOP_K)],
        sem, tm * TOP_K)
    cur = lambda a: cur_ref[0, 0, a]

    @pl.when(i == 0)
    def _():
        rows.start(cur, 0)

    @pl.when(i + 1 < nt)
    def _():
        rows.start(lambda a: nxt_ref[0, 0, a], 1 - slot)

    rows.wait(slot)

    def weigh(t, c):
        tot = buf_ref[slot, 0, t] * w_ref[0, 0, TOP_K * t]
        for k in range(1, TOP_K):
            tot = tot + buf_ref[slot, k, t] * w_ref[0, 0, TOP_K * t + k]
        tot_ref[t] = tot
        return c

    lax.fori_loop(0, tm, weigh, 0, unroll=4)
    for r0 in range(0, tm, 8):
        blk = pltpu.einshape("rcl->crl", tot_ref[r0:r0 + 8])
        for c in range(nch):
            cs = slice(c * ROW_CH, (c + 1) * ROW_CH)
            o_ref[r0:r0 + 8, cs] = x_ref[r0:r0 + 8, cs] + blk[c]


def _combine(ys3, pos, top_w, x1, tm=128):
    n, d = x1.shape
    _, nch, chw = ys3.shape
    nt = n // tm
    pos3 = pos.reshape(nt, 1, tm * TOP_K)
    idx_spec = lambda f: pl.BlockSpec((1, 1, tm * TOP_K), f, memory_space=pltpu.SMEM)
    return pl.pallas_call(
        _combine_body,
        grid=(nt,),
        in_specs=[idx_spec(lambda i: (i, 0, 0)),
                  idx_spec(lambda i: (jnp.minimum(i + 1, nt - 1), 0, 0)),
                  idx_spec(lambda i: (i, 0, 0)),
                  pl.BlockSpec(memory_space=pl.ANY),
                  pl.BlockSpec((tm, d), lambda i: (i, 0))],
        out_specs=pl.BlockSpec((tm, d), lambda i: (i, 0)),
        out_shape=jax.ShapeDtypeStruct((n, d), f32),
        scratch_shapes=[pltpu.VMEM((2, TOP_K, tm, nch, chw), f32), pltpu.VMEM((tm, nch, chw), f32),
                        pltpu.SemaphoreType.DMA((2,))],
        compiler_params=_cparams(1),
        name="moe_combine",
    )(pos3, pos3, top_w.reshape(nt, 1, tm * TOP_K), ys3, x1)


def _route(top_idx):
    n = top_idx.shape[0]
    n_assign = n * TOP_K
    e_flat = top_idx.reshape(-1)
    a_iota = jnp.arange(n_assign, dtype=i32)
    _, order = lax.sort((e_flat, a_iota), num_keys=1, is_stable=True)
    _, rank = lax.sort((order, a_iota), num_keys=1)
    experts = jnp.arange(N_EXPERTS, dtype=i32)
    onehot = experts[:, None] == e_flat[None, :]
    counts = jnp.sum(onehot.astype(i32), axis=1)
    padded = (counts + MOE_TM - 1) // MOE_TM * MOE_TM
    start = jnp.cumsum(counts) - counts
    pend = jnp.cumsum(padded)
    shift = pend - padded - start
    pos = rank + jnp.sum(jnp.where(onehot, shift[:, None], 0), axis=0)
    n_blocks = -(-n_assign // MOE_TM) + N_EXPERTS
    n_used = pend[-1] // MOE_TM
    bi = jnp.arange(n_blocks, dtype=i32)
    raw_e = jnp.minimum(jnp.sum((pend[None, :] <= (bi * MOE_TM)[:, None]).astype(i32), axis=1),
                        N_EXPERTS - 1)
    blk_e = jnp.where(bi < n_used, raw_e, raw_e[jnp.maximum(n_used - 1, 0)]).astype(i32)
    blk_s0 = (bi * MOE_TM - shift[blk_e]).astype(i32)
    blk_end = (start[blk_e] + counts[blk_e]).astype(i32)
    tok_sorted = lax.shift_right_logical(order, 2)
    return tok_sorted, blk_e, blk_s0, blk_end, n_used.reshape(1).astype(i32), pos.astype(i32)


NSA_TQ = 256


def _layer(x, ln1_g, w_in, conv_w, conv_b, dt_bias, a_log, d_skip, ssd_norm_g, q_norm_g,
           k_norm_g, cmp_pe, cmp_w1, cmp_w2, w_out, ln2_g, w_router, b_router, w_up, b_up,
           w_down, b_down):
    bsz, t_len, d = x.shape
    n = bsz * t_len
    x2 = x.reshape(n, d)

    o_dt = SSD_D_INNER + SSD_D_INNER + 2 * SSD_GN
    o_q = o_dt + SSD_HEADS
    o_kv = o_q + NSA_WIDTH
    o_g = o_kv + 6 * NSA_KV_WIDTH
    n_in = o_g + 3 * NSA_HEADS
    w_perm = jnp.concatenate(
        [w_in[:, :o_dt], w_in[:, o_q:o_g], w_in[:, o_dt:o_q], w_in[:, o_g:n_in],
         jnp.zeros((d, NP_PROJ - n_in), w_in.dtype)], axis=1).astype(bf16)

    proj = _inproj(x2, ln1_g.reshape(1, d), w_perm)

    lane_pad = lambda v: jnp.concatenate([v, jnp.zeros((LANE - v.shape[0],), f32)]).reshape(1, LANE)
    y_ssd = _ssd(proj, bsz, t_len, conv_w, conv_b.reshape(1, -1), lane_pad(dt_bias),
                 lane_pad(a_log), jnp.repeat(d_skip, SSD_HEAD_DIM).reshape(1, -1),
                 ssd_norm_g.reshape(1, -1))

    kc, vct = _compress(proj, bsz, t_len, cmp_pe, cmp_w1.astype(bf16), cmp_w2.astype(bf16),
                        k_norm_g[0:1])
    qt, ks, vst, kw, vwt = _qk_prep(proj, q_norm_g.reshape(1, DH), k_norm_g, bsz, t_len, NSA_TQ)
    gates = proj[:, COL_DT + GATE_OFF:COL_DT + GATE_OFF + 3 * NSA_HEADS]
    gates = gates.reshape(n, NSA_KV_HEADS, 3 * NSA_GQA).transpose(1, 2, 0)
    y_nsa = _nsa_attention(qt, kc, vct, ks, vst, kw, vwt, gates, bsz, t_len, NSA_TQ)

    wo = w_out.astype(bf16)
    wr = jnp.concatenate([w_router, jnp.zeros((d, LANE - N_EXPERTS), f32)], axis=1)
    wr_hi = wr.astype(bf16)
    wr = jnp.concatenate([wr_hi, (wr - wr_hi.astype(f32)).astype(bf16)], axis=1)
    br =jnp.concatenate([b_router, jnp.zeros((LANE - N_EXPERTS,), f32)]).reshape(1, LANE)
    x1, h3, ti, tw = _outproj_router(y_ssd, y_nsa, wo[:SSD_D_INNER], wo[SSD_D_INNER:], x2,
                                     ln2_g.reshape(1, d), wr, br)

    tok_sorted, blk_e, blk_s0, blk_end, n_used, pos = _route(ti[:, :TOP_K])
    xs = _gather_tokens(h3, tok_sorted, blk_s0, blk_end, n_used, MOE_TM)
    act = _moe_up(blk_e, n_used, xs, w_up, b_up.reshape(N_EXPERTS, 1, -1))
    ys3 = _moe_down(blk_e, n_used, act, w_down, b_down.reshape(N_EXPERTS, 1, -1))
    out = _combine(ys3, pos, tw[:, :TOP_K], x1)
    return out.reshape(bsz, t_len, d)


def kernel(x, ln1_g, w_in, conv_w, conv_b, dt_bias, a_log, d_skip, ssd_norm_g, q_norm_g,
           k_norm_g, cmp_pe, cmp_w1, cmp_w2, w_out, ln2_g, w_router, b_router, w_up, b_up,
           w_down, b_down):
    return _layer(x, ln1_g[0], w_in[0], conv_w[0], conv_b[0], dt_bias[0], a_log[0], d_skip[0],
                  ssd_norm_g[0], q_norm_g[0], k_norm_g[0], cmp_pe[0], cmp_w1[0], cmp_w2[0],
                  w_out[0], ln2_g[0], w_router[0], b_router[0], w_up[0], b_up[0], w_down[0],
                  b_down[0])
```

```python
import functools
import math

import jax
import jax.numpy as jnp
from jax import lax
from jax.experimental import pallas as pl
from jax.experimental.pallas import tpu as pltpu

f32 = jnp.float32
bf16 = jnp.bfloat16
i32 = jnp.int32

D_MODEL = 4096
SSD_D_INNER = 2048
SSD_HEAD_DIM = 64
SSD_HEADS = 32
SSD_GROUPS = 8
SSD_HPG = 4
SSD_STATE = 128
SSD_CONV = 4
SSD_CHUNK = 256
SSD_GN = SSD_GROUPS * SSD_STATE
SSD_GW = SSD_HPG * SSD_HEAD_DIM
NSA_HEADS = 16
NSA_KV_HEADS = 4
NSA_GQA = 4
DH = 128
NSA_WIDTH = 2048
NSA_KV_WIDTH = 512
CMP_BLOCK = 32
CMP_STRIDE = 16
CMP_HIDDEN = 256
SLC_BLOCK = 64
SLC_TOPK = 16
WINDOW = 512
N_EXPERTS = 32
TOP_K = 4
D_EXPERT = 1536
SWIGLU_LIMIT = 7.0
SWIGLU_ALPHA = 1.702
EPS = 1e-5
NEG = -1e30
LOG2E = 1.4426950408889634
SLC_GROUP = 4
VT_PAD = 16

COL_Z = 0
COL_X = 2048
COL_B = 4096
COL_C = 5120
COL_Q = 6144
COL_KV = 8192
COL_DT = 11264
GATE_OFF = 32
NP_PROJ = 11520

LANE = 128
VMEM_LIMIT = 56 * 1024 * 1024


def _cparams(n_axes):
    return pltpu.CompilerParams(dimension_semantics=("arbitrary",) * n_axes,
                                vmem_limit_bytes=VMEM_LIMIT)


def _silu(v):
    return v * (1.0 / (1.0 + jnp.exp(-v)))


def _inproj_body(x_ref, g_ref, w_ref, o_ref, h_ref):
    @pl.when(pl.program_id(1) == 0)
    def _():
        x = x_ref[...]
        ms = jnp.mean(x * x, axis=-1, keepdims=True)
        h_ref[...] = (x * lax.rsqrt(ms + EPS) * g_ref[...]).astype(bf16)

    o_ref[...] = jnp.dot(h_ref[...], w_ref[...], preferred_element_type=f32)


def _inproj(x2, g, w, tm=512, tn=768):
    n, d = x2.shape
    npj = w.shape[1]
    return pl.pallas_call(
        _inproj_body,
        grid=(n // tm, npj // tn),
        in_specs=[pl.BlockSpec((tm, d), lambda i, j: (i, 0)),
                  pl.BlockSpec((1, d), lambda i, j: (0, 0)),
                  pl.BlockSpec((d, tn), lambda i, j: (0, j))],
        out_specs=pl.BlockSpec((tm, tn), lambda i, j: (i, j)),
        out_shape=jax.ShapeDtypeStruct((n, npj), f32),
        scratch_shapes=[pltpu.VMEM((tm, d), bf16)],
        compiler_params=_cparams(2),
        name="inproj",
    )(x2, g, w)


def _lane_expand(cols, width):
    L = cols[0].shape[0]
    n = len(cols)
    lane = lax.broadcasted_iota(i32, (L, n * width), 1)
    out = jnp.broadcast_to(cols[n - 1], (L, n * width))
    for j in range(n - 2, -1, -1):
        out = jnp.where(lane < (j + 1) * width, jnp.broadcast_to(cols[j], (L, n * width)), out)
    return out


def _ssd_body(z_ref, x_ref, b_ref, c_ref, dt_ref, cw_ref, cb_ref, dtb_ref, alog_ref,
              dskip_ref, ng_ref, y_ref, buf_ref, xs_ref, bt_ref, cs_ref, h_ref):
    L = SSD_CHUNK
    c_idx = pl.program_id(1)

    @pl.when(c_idx == 0)
    def _():
        buf_ref[0:8, :] = jnp.zeros((8, buf_ref.shape[1]), f32)
        h_ref[...] = jnp.zeros(h_ref.shape, f32)

    buf_ref[8:8 + L, 0:SSD_D_INNER] = x_ref[...]
    buf_ref[8:8 + L, SSD_D_INNER:SSD_D_INNER + SSD_GN] = b_ref[...]
    buf_ref[8:8 + L, SSD_D_INNER + SSD_GN:] = c_ref[...]

    cw = 512
    n_ch = buf_ref.shape[1]
    for c0 in range(0, n_ch, cw):
        acc = jnp.broadcast_to(cb_ref[:, c0:c0 + cw], (L, cw))
        for k in range(SSD_CONV):
            acc = acc + cw_ref[k:k + 1, c0:c0 + cw] * buf_ref[5 + k:5 + k + L, c0:c0 + cw]
        v = _silu(acc)
        if c0 < SSD_D_INNER:
            xs_ref[:, c0:c0 + cw] = v
        elif c0 < SSD_D_INNER + SSD_GN:
            bt_ref[c0 - SSD_D_INNER:c0 - SSD_D_INNER + cw, :] = v.T.astype(bf16)
        else:
            o = c0 - SSD_D_INNER - SSD_GN
            cs_ref[:, o:o + cw] = v.astype(bf16)
    buf_ref[0:8, :] = buf_ref[L:L + 8, :]

    dt = dt_ref[...] + dtb_ref[...]
    dt = jnp.maximum(dt, 0.0) + jnp.log(1.0 + jnp.exp(-jnp.abs(dt)))
    da = dt * -jnp.exp(alog_ref[...])
    r = lax.broadcasted_iota(i32, (L, L), 0)
    s = lax.broadcasted_iota(i32, (L, L), 1)
    tri = (s <= r).astype(f32)
    acs = jnp.dot(tri, da, preferred_element_type=f32,
                  precision=lax.Precision.HIGHEST)
    acs_t = acs.T
    causal = s <= r
    a_last = acs[L - 1:L, :]
    e_acs = jnp.exp(acs)
    e_rem = jnp.exp(a_last - acs)
    e_last = jnp.exp(a_last)
    lane_gw = lax.broadcasted_iota(i32, (L, SSD_GW), 1)

    for g in range(SSD_GROUPS):
        hs = [g * SSD_HPG + j for j in range(SSD_HPG)]
        xg = xs_ref[:, g * SSD_GW:(g + 1) * SSD_GW]
        bgt = bt_ref[g * SSD_STATE:(g + 1) * SSD_STATE, :]
        cg = cs_ref[:, g * SSD_STATE:(g + 1) * SSD_STATE]
        dt_x = _lane_expand([dt[:, h:h + 1] for h in hs], SSD_HEAD_DIM)
        xdt = xg * dt_x
        cbm = jnp.dot(cg, bgt, preferred_element_type=f32)
        y = jnp.zeros((L, SSD_GW), f32)
        for j, h in enumerate(hs):
            seg = jnp.where(causal, acs[:, h:h + 1] - acs_t[h:h + 1, :], NEG)
            w = (cbm * jnp.exp(seg)).astype(bf16)
            band = (lane_gw >= j * SSD_HEAD_DIM) & (lane_gw < (j + 1) * SSD_HEAD_DIM)
            xm = jnp.where(band, xdt, 0.0).astype(bf16)
            y = y + jnp.dot(w, xm, preferred_element_type=f32)
        hprev = h_ref[g]
        y_off = jnp.dot(cg, hprev.astype(bf16), preferred_element_type=f32)
        y = y + y_off * _lane_expand([e_acs[:, h:h + 1] for h in hs], SSD_HEAD_DIM)
        xw = (xdt * _lane_expand([e_rem[:, h:h + 1] for h in hs], SSD_HEAD_DIM)).astype(bf16)
        st = jnp.dot(bgt, xw, preferred_element_type=f32)
        dec = _lane_expand([e_last[:, h:h + 1] for h in hs], SSD_HEAD_DIM)
        h_ref[g] = hprev * dec + st
        y = y + xg * dskip_ref[:, g * SSD_GW:(g + 1) * SSD_GW]
        y = y * _silu(z_ref[:, g * SSD_GW:(g + 1) * SSD_GW])
        y = y * lax.rsqrt(jnp.mean(y * y, axis=-1, keepdims=True) + EPS)
        y_ref[:, g * SSD_GW:(g + 1) * SSD_GW] = (y * ng_ref[:, g * SSD_GW:(g + 1) * SSD_GW]).astype(bf16)


def _ssd(proj, bsz, t_len, conv_w, conv_b, dt_bias, a_log, dskip_x, norm_g):
    L = SSD_CHUNK
    nc = t_len // L
    n = bsz * t_len
    nch = SSD_D_INNER + 2 * SSD_GN
    row = lambda b, c: b * nc + c
    full = lambda shape: pl.BlockSpec(shape, lambda b, c: (0,) * len(shape))
    return pl.pallas_call(
        _ssd_body,
        grid=(bsz, nc),
        in_specs=[pl.BlockSpec((L, SSD_D_INNER), lambda b, c: (row(b, c), COL_Z // SSD_D_INNER)),
                  pl.BlockSpec((L, SSD_D_INNER), lambda b, c: (row(b, c), COL_X // SSD_D_INNER)),
                  pl.BlockSpec((L, SSD_GN), lambda b, c: (row(b, c), COL_B // SSD_GN)),
                  pl.BlockSpec((L, SSD_GN), lambda b, c: (row(b, c), COL_C // SSD_GN)),
                  pl.BlockSpec((L, LANE), lambda b, c: (row(b, c), COL_DT // LANE)),
                  full((SSD_CONV, nch)), full((1, nch)), full((1, LANE)),
                  full((1, LANE)), full((1, SSD_D_INNER)), full((1, SSD_D_INNER))],
        out_specs=pl.BlockSpec((L, SSD_D_INNER), lambda b, c: (row(b, c), 0)),
        out_shape=jax.ShapeDtypeStruct((n, SSD_D_INNER), bf16),
        scratch_shapes=[pltpu.VMEM((L + 8, nch), f32),
                        pltpu.VMEM((L, SSD_D_INNER), f32),
                        pltpu.VMEM((SSD_GN, L), bf16),
                        pltpu.VMEM((L, SSD_GN), bf16),
                        pltpu.VMEM((SSD_GROUPS, SSD_STATE, SSD_GW), f32)],
        compiler_params=_cparams(2),
        name="ssd",
    )(proj, proj, proj, proj, proj, conv_w, conv_b, dt_bias, a_log, dskip_x, norm_g)


def _cmp_body(uk_ref, uv_ref, pe_ref, w1_ref, w2_ref, g_ref, kc_ref, vct_ref):
    nchunk = uk_ref.shape[0]

    def branch(u_ref, kv):
        hid_a = jnp.zeros((nchunk, CMP_HIDDEN), f32)
        hid_b = jnp.zeros((nchunk, CMP_HIDDEN), f32)
        ut = pltpu.einshape("csd->scd", u_ref[...])
        for s in range(CMP_STRIDE):
            us = ut[s]
            hid_a = hid_a + jnp.dot((us + pe_ref[kv, s:s + 1, :]).astype(bf16), w1_ref[kv, s],
                                    preferred_element_type=f32)
            hid_b = hid_b + jnp.dot(
                (us + pe_ref[kv, CMP_STRIDE + s:CMP_STRIDE + s + 1, :]).astype(bf16),
                w1_ref[kv, CMP_STRIDE + s], preferred_element_type=f32)
        hid = hid_a + pltpu.roll(hid_b, nchunk - 1, 0)
        out = jnp.dot(_silu(hid).astype(bf16), w2_ref[kv], preferred_element_type=f32)
        rowi = lax.broadcasted_iota(i32, out.shape, 0)
        return jnp.where(rowi == nchunk - 1, 0.0, out)

    k = branch(uk_ref, 0)
    k = k * lax.rsqrt(jnp.mean(k * k, axis=-1, keepdims=True) + EPS) * g_ref[...]
    kc_ref[0, 0] = k.astype(bf16)
    vct_ref[0, 0] = branch(uv_ref, 1).T.astype(bf16)


def _compress(proj, bsz, t_len, pe, w1, w2, kg):
    nchunk = t_len // CMP_STRIDE
    H = NSA_KV_HEADS
    u3 = proj.reshape(bsz * nchunk, CMP_STRIDE, proj.shape[1])
    useg = lambda kv: pl.BlockSpec((nchunk, CMP_STRIDE, DH),
                                   lambda b, h: (b, 0, COL_KV // DH + kv * H + h))
    full = lambda shape: pl.BlockSpec(shape, lambda b, h: (0,) * len(shape))
    return pl.pallas_call(
        _cmp_body,
        grid=(bsz, H),
        in_specs=[useg(0), useg(1), full((2, CMP_BLOCK, DH)), full((2, CMP_BLOCK, DH, CMP_HIDDEN)),
                  full((2, CMP_HIDDEN, DH)), full((1, DH))],
        out_specs=[pl.BlockSpec((1, 1, nchunk, DH), lambda b, h: (b, h, 0, 0)),
                   pl.BlockSpec((1, 1, DH, nchunk), lambda b, h: (b, h, 0, 0))],
        out_shape=[jax.ShapeDtypeStruct((bsz, H, nchunk, DH), bf16),
                   jax.ShapeDtypeStruct((bsz, H, DH, nchunk), bf16)],
        compiler_params=_cparams(2),
        name="nsa_compress",
    )(u3, u3, pe, w1, w2, kg)


def _qk_prep_body(q_ref, s_ref, w_ref, qg_ref, kg_ref, qt_ref, ks_ref, vst_ref, kw_ref, vwt_ref):
    scale = DH ** -0.5 * LOG2E
    tq = q_ref.shape[0]

    def hnorm(v, g):
        return v * lax.rsqrt(jnp.mean(v * v, axis=-1, keepdims=True) + EPS) * g

    ones_rows = (lax.broadcasted_iota(i32, (VT_PAD, tq), 0) == 0).astype(bf16)
    for h in range(NSA_KV_HEADS):
        for g in range(NSA_GQA):
            sl = slice((h * NSA_GQA + g) * DH, (h * NSA_GQA + g + 1) * DH)
            qn = hnorm(q_ref[:, sl], qg_ref[...]) * scale
            qt_ref[0, h, 0, :, g * tq:(g + 1) * tq] = qn.T.astype(bf16)
        sl = slice(h * DH, (h + 1) * DH)
        sv = slice(NSA_KV_WIDTH + h * DH, NSA_KV_WIDTH + (h + 1) * DH)
        ks_ref[:, sl] = hnorm(s_ref[:, sl], kg_ref[1:2, :]).astype(bf16)
        kw_ref[:, sl] = hnorm(w_ref[:, sl], kg_ref[2:3, :]).astype(bf16)
        vst_ref[0, h, 0, 0:DH] = s_ref[:, sv].T.astype(bf16)
        vwt_ref[0, h, 0, 0:DH] = w_ref[:, sv].T.astype(bf16)
        vst_ref[0, h, 0, DH:DH + VT_PAD] = ones_rows
        vwt_ref[0, h, 0, DH:DH + VT_PAD] = ones_rows


def _qk_prep(proj, qg, kg, bsz, t_len, tq):
    n = proj.shape[0]
    nq = t_len // tq
    H = NSA_KV_HEADS
    w2 = 2 * NSA_KV_WIDTH
    row = lambda b, i: b * nq + i
    vt_spec = pl.BlockSpec((1, H, 1, DH + VT_PAD, tq), lambda b, i: (b, 0, i, 0, 0))
    vt_shape = jax.ShapeDtypeStruct((bsz, H, nq, DH + VT_PAD, tq), bf16)
    return pl.pallas_call(
        _qk_prep_body,
        grid=(bsz, nq),
        in_specs=[pl.BlockSpec((tq, NSA_WIDTH), lambda b, i: (row(b, i), COL_Q // NSA_WIDTH)),
                  pl.BlockSpec((tq, w2), lambda b, i: (row(b, i), (COL_KV + w2) // w2)),
                  pl.BlockSpec((tq, w2), lambda b, i: (row(b, i), (COL_KV + 2 * w2) // w2)),
                  pl.BlockSpec((1, DH), lambda b, i: (0, 0)),
                  pl.BlockSpec((3, DH), lambda b, i: (0, 0))],
        out_specs=[pl.BlockSpec((1, H, 1, DH, NSA_GQA * tq), lambda b, i: (b, 0, i, 0, 0)),
                   pl.BlockSpec((tq, NSA_KV_WIDTH), lambda b, i: (row(b, i), 0)),
                   vt_spec,
                   pl.BlockSpec((tq, NSA_KV_WIDTH), lambda b, i: (row(b, i), 0)),
                   vt_spec],
        out_shape=[jax.ShapeDtypeStruct((bsz, H, nq, DH, NSA_GQA * tq), bf16),
                   jax.ShapeDtypeStruct((n, NSA_KV_WIDTH), bf16), vt_shape,
                   jax.ShapeDtypeStruct((n, NSA_KV_WIDTH), bf16), vt_shape],
        compiler_params=_cparams(2),
        name="nsa_qk_prep",
    )(proj, proj, proj, qg, kg)


def _nsa_body(qt_ref, kc_ref, vct_ref, ks_ref, vst_ref, kw_ref, vwt_ref, gt_ref, o_ref,
              m_ref, acc_ref, sel_ref, *, tq):
    i = pl.program_id(2)
    G = NSA_GQA
    n_cmp = kc_ref.shape[2]
    n_slc = ks_ref.shape[0] // SLC_BLOCK

    qt = qt_ref[0, 0, 0]
    t_row = i * tq + lax.broadcasted_iota(i32, (1, tq), 1)
    t_all = jnp.concatenate([t_row] * G, axis=1)

    s_c = jnp.dot(kc_ref[0, 0], qt, preferred_element_type=f32)
    cend = lax.broadcasted_iota(i32, (n_cmp, 1), 0) * CMP_STRIDE + (CMP_BLOCK - 1)
    m_c = cend <= t_all
    s_c = jnp.where(m_c, s_c, NEG)
    p_c = jnp.where(m_c, jnp.exp2(s_c - jnp.max(s_c, axis=0, keepdims=True)), 0.0)
    p_c = p_c / jnp.maximum(jnp.sum(p_c, axis=0, keepdims=True), 1e-30)
    o_c = jnp.dot(vct_ref[0, 0], p_c.astype(bf16), preferred_element_type=f32)

    imp = p_c[:, 0:tq]
    for g in range(1, G):
        imp = imp + p_c[:, g * tq:(g + 1) * tq]
    per = SLC_BLOCK // CMP_STRIDE
    ni = lax.broadcasted_iota(i32, (n_slc, n_cmp), 0)
    ci = lax.broadcasted_iota(i32, (n_slc, n_cmp), 1)
    fold = ((ci // per == ni).astype(f32) + ((ci + 1) // per == ni).astype(f32))
    imp_b = jnp.dot(fold, imp, preferred_element_type=f32, precision=lax.Precision.HIGHEST)
    jb = lax.broadcasted_iota(i32, (n_slc, tq), 0)
    cur = t_row // SLC_BLOCK
    forced = (jb == 0) | (jb == cur) | (jb == cur - 1)
    val = jnp.where(forced, 1e30, jnp.where(jb <= cur, imp_b, -1.0))
    sel = jnp.zeros((n_slc, tq), f32)
    for _ in range(SLC_TOPK):
        mx = jnp.max(val, axis=0, keepdims=True)
        first = jnp.min(jnp.where(val == mx, jb, n_slc), axis=0, keepdims=True)
        hit = (jb == first) & (mx >= 0.0)
        sel = jnp.where(hit, 1.0, sel)
        val = jnp.where(jb == first, -1.0, val)

    def online(tiles, slot):
        scores = [jnp.dot(k, qt, preferred_element_type=f32) + jnp.concatenate([bias] * G, axis=1)
                  for k, _, bias in tiles]
        m_old = m_ref[slot]
        m_new = m_old
        for s_ in scores:
            m_new = jnp.maximum(m_new, jnp.max(s_, axis=0, keepdims=True))
        acc = jnp.exp2(m_old - m_new) * acc_ref[slot]
        for s_, (_, vt, _) in zip(scores, tiles):
            acc = acc + jnp.dot(vt, jnp.exp2(s_ - m_new).astype(bf16), preferred_element_type=f32)
        acc_ref[slot] = acc
        m_ref[slot] = m_new

    m_ref[...] = jnp.full(m_ref.shape, NEG, f32)
    acc_ref[...] = jnp.zeros(acc_ref.shape, f32)
    krow = lax.broadcasted_iota(i32, (tq, 1), 0)
    kcol = lax.broadcasted_iota(i32, (1, tq), 1)

    sel_ref[...] = jnp.where(sel > 0.5, 0.0, NEG)
    bpt = tq // SLC_BLOCK

    def block_bias(kt):
        rows = [jnp.broadcast_to(sel_ref[pl.ds(kt * bpt + j, 1), :], (SLC_BLOCK, tq))
                for j in range(bpt)]
        return jnp.concatenate(rows, axis=0)

    def slc_tile(kt, causal=False):
        bias = block_bias(kt)
        if causal:
            bias = jnp.where(krow <= kcol, bias, NEG)
        return ks_ref[pl.ds(pl.multiple_of(kt * tq, tq), tq), :], vst_ref[0, 0, kt], bias

    def slc_group(kg, carry):
        online([slc_tile(SLC_GROUP * kg + j) for j in range(SLC_GROUP)], 0)
        return carry

    n_grp = i // SLC_GROUP
    lax.fori_loop(0, n_grp, slc_group, 0)
    for r in range(SLC_GROUP):
        @pl.when(i - n_grp * SLC_GROUP == r)
        def _():
            online([slc_tile(n_grp * SLC_GROUP + j) for j in range(r)] + [slc_tile(i, causal=True)], 0)

    def win_tile(kt):
        k0 = pl.multiple_of(kt * tq, tq)
        key = k0 + krow
        ok = (key <= t_row) & (key > t_row - WINDOW)
        return kw_ref[pl.ds(k0, tq), :], vwt_ref[0, 0, kt], jnp.where(ok, 0.0, NEG)

    back = WINDOW // tq
    for c in range(back + 1):
        @pl.when(jnp.minimum(i, back) == c)
        def _():
            online([win_tile(i - c + j) for j in range(c + 1)], 1)

    o_s = acc_ref[0, 0:DH] / acc_ref[0, DH:DH + 1]
    o_w = acc_ref[1, 0:DH] / acc_ref[1, DH:DH + 1]
    gts = 1.0 / (1.0 + jnp.exp(-gt_ref[0]))
    for g in range(G):
        cs = slice(g * tq, (g + 1) * tq)
        out_t = (gts[3 * g:3 * g + 1, :] * o_c[:, cs] + gts[3 * g + 1:3 * g + 2, :] * o_s[:, cs]
                 + gts[3 * g + 2:3 * g + 3, :] * o_w[:, cs])
        o_ref[:, g * DH:(g + 1) * DH] = out_t.T.astype(bf16)


def _nsa_attention(qt, kc, vct, ks, vst, kw, vwt, gates, bsz, t_len, tq):
    n = bsz * t_len
    nq = t_len // tq
    nchunk = t_len // CMP_STRIDE
    R = NSA_GQA * tq
    seq = pl.BlockSpec((t_len, DH), lambda b, h, i: (b, h))
    seq_t = pl.BlockSpec((1, 1, nq, DH + VT_PAD, tq), lambda b, h, i: (b, h, 0, 0, 0))
    return pl.pallas_call(
        functools.partial(_nsa_body, tq=tq),
        grid=(bsz, NSA_KV_HEADS, nq),
        in_specs=[pl.BlockSpec((1, 1, 1, DH, R), lambda b, h, i: (b, h, i, 0, 0)),
                  pl.BlockSpec((1, 1, nchunk, DH), lambda b, h, i: (b, h, 0, 0)),
                  pl.BlockSpec((1, 1, DH, nchunk), lambda b, h, i: (b, h, 0, 0)),
                  seq, seq_t, seq, seq_t,
                  pl.BlockSpec((1, 3 * NSA_GQA, tq), lambda b, h, i: (h, 0, b * nq + i))],
        out_specs=pl.BlockSpec((tq, NSA_GQA * DH), lambda b, h, i: (b * nq + i, h)),
        out_shape=jax.ShapeDtypeStruct((n, NSA_WIDTH), bf16),
        scratch_shapes=[pltpu.VMEM((2, 1, R), f32), pltpu.VMEM((2, DH + VT_PAD, R), f32),
                        pltpu.VMEM((t_len // SLC_BLOCK, tq), f32)],
        compiler_params=_cparams(3),
        name="nsa_attention",
    )(qt, kc, vct, ks, vst, kw, vwt, gates)


ROW_CH = LANE


def _outproj_body(ya_ref, yb_ref, wa_ref, wb_ref, x_ref, x1_ref):
    acc = x_ref[...] + jnp.dot(ya_ref[...], wa_ref[...], preferred_element_type=f32)
    x1_ref[...] = acc + jnp.dot(yb_ref[...], wb_ref[...], preferred_element_type=f32)


def _outproj(ya, yb, wa, wb, x2, tm=512, tn=1024):
    n, d = x2.shape
    ka = ya.shape[1]
    kb = yb.shape[1]
    return pl.pallas_call(
        _outproj_body,
        grid=(n // tm, d // tn),
        in_specs=[pl.BlockSpec((tm, ka), lambda i, j: (i, 0)),
                  pl.BlockSpec((tm, kb), lambda i, j: (i, 0)),
                  pl.BlockSpec((ka, tn), lambda i, j: (0, j)),
                  pl.BlockSpec((kb, tn), lambda i, j: (0, j)),
                  pl.BlockSpec((tm, tn), lambda i, j: (i, j))],
        out_specs=pl.BlockSpec((tm, tn), lambda i, j: (i, j)),
        out_shape=jax.ShapeDtypeStruct((n, d), f32),
        compiler_params=_cparams(2),
        name="outproj",
    )(ya, yb, wa, wb, x2)


def _router_body(x_ref, g_ref, wr_ref, br_ref, h_ref, ti_ref, tw_ref):
    tm, d = x_ref.shape
    x = x_ref[...]
    hh = x * lax.rsqrt(jnp.mean(x * x, axis=-1, keepdims=True) + EPS) * g_ref[...]
    hi = hh.astype(bf16)
    lo = (hh - hi.astype(f32)).astype(bf16)
    t = jnp.dot(hi, wr_ref[...], preferred_element_type=f32)
    logits = br_ref[...] + (t[:, :LANE] + (t[:, LANE:] + jnp.dot(lo, wr_ref[:, 0:LANE],
                                                                  preferred_element_type=f32)))
    for r0 in range(0, tm, 8):
        blk = jnp.stack([hh[r0:r0 + 8, c * ROW_CH:(c + 1) * ROW_CH] for c in range(d // ROW_CH)], axis=0)
        h_ref[r0:r0 + 8] = pltpu.einshape("crl->rcl", blk)
    lane = lax.broadcasted_iota(i32, (tm, LANE), 1)
    val = jnp.where(lane < N_EXPERTS, logits, -jnp.inf)
    idxs = jnp.zeros((tm, LANE), i32)
    vals = jnp.full((tm, LANE), -jnp.inf, f32)
    for k in range(TOP_K):
        mx = jnp.max(val, axis=-1, keepdims=True)
        first = jnp.min(jnp.where(val == mx, lane, LANE), axis=-1, keepdims=True)
        idxs = jnp.where(lane == k, first, idxs)
        vals = jnp.where(lane == k, mx, vals)
        val = jnp.where(lane == first, -jnp.inf, val)
    e = jnp.exp(vals - jnp.max(vals, axis=-1, keepdims=True))
    tw_ref[...] = e / jnp.sum(e, axis=-1, keepdims=True)
    ti_ref[...] = idxs


def _norm_router(x1, g2, wr, br, tm=256):
    n, d = x1.shape
    return pl.pallas_call(
        _router_body,
        grid=(n // tm,),
        in_specs=[pl.BlockSpec((tm, d), lambda i: (i, 0)),
                  pl.BlockSpec((1, d), lambda i: (0, 0)),
                  pl.BlockSpec((d, 2 * LANE), lambda i: (0, 0)),
                  pl.BlockSpec((1, LANE), lambda i: (0, 0))],
        out_specs=[pl.BlockSpec((tm, d // ROW_CH, ROW_CH), lambda i: (i, 0, 0)),
                   pl.BlockSpec((tm, LANE), lambda i: (i, 0)),
                   pl.BlockSpec((tm, LANE), lambda i: (i, 0))],
        out_shape=[jax.ShapeDtypeStruct((n, d // ROW_CH, ROW_CH), f32),
                   jax.ShapeDtypeStruct((n, LANE), i32),
                   jax.ShapeDtypeStruct((n, LANE), f32)],
        compiler_params=_cparams(1),
        name="norm_router",
    )(x1, g2, wr, br)


MOE_TM = 512
DMA_UNROLL = 8


class _RowGather:
    def __init__(self, src_ref, dst_of, bulk_of, sem, count):
        self.src_ref, self.dst_of, self.bulk_of, self.sem, self.count = src_ref, dst_of, bulk_of, sem, count

    def start(self, idx_of, slot):
        def body(r, c):
            pltpu.make_async_copy(self.src_ref.at[idx_of(r)], self.dst_of(slot, r),
                                  self.sem.at[slot]).start()
            return c
        lax.fori_loop(0, self.count, body, 0, unroll=DMA_UNROLL)

    def wait(self, slot):
        for src, dst in self.bulk_of(slot):
            pltpu.make_async_copy(src, dst, self.sem.at[slot]).wait()


def _gather_body(tok_ref, s0_ref, end_ref, nu_ref, src_ref, o_ref, buf_ref, sem):
    i = pl.program_id(0)
    tmr = o_ref.shape[0]
    nch = buf_ref.shape[2]
    slot = lax.rem(i, 2)
    n_live = jnp.minimum(nu_ref[0], pl.num_programs(0))

    def tokens_of(tile):
        s0 = s0_ref[tile]
        last = end_ref[tile] - 1
        return lambda r: tok_ref[jnp.minimum(s0 + r, last)]

    rows = _RowGather(src_ref, lambda s, r: buf_ref.at[s, r],
                      lambda s: [(src_ref.at[pl.ds(0, tmr)], buf_ref.at[s])], sem, tmr)

    @pl.when((i == 0) & (n_live > 0))
    def _():
        rows.start(tokens_of(0), 0)

    @pl.when(i + 1 < n_live)
    def _():
        rows.start(tokens_of(i + 1), 1 - slot)

    @pl.when(i < n_live)
    def _():
        rows.wait(slot)
        for r0 in range(0, tmr, 8):
            blk = pltpu.einshape("rcl->crl", buf_ref[slot, r0:r0 + 8])
            for c in range(nch):
                o_ref[r0:r0 + 8, c * ROW_CH:(c + 1) * ROW_CH] = blk[c].astype(o_ref.dtype)

    @pl.when(i >= n_live)
    def _():
        o_ref[...] = jnp.zeros(o_ref.shape, o_ref.dtype)


def _gather_tokens(src3, tok_sorted, blk_s0, blk_end, n_used, tmr):
    _, nch, chw = src3.shape
    nt = blk_s0.shape[0]
    gs = pltpu.PrefetchScalarGridSpec(
        num_scalar_prefetch=4,
        grid=(nt,),
        in_specs=[pl.BlockSpec(memory_space=pl.ANY)],
        out_specs=pl.BlockSpec((tmr, nch * chw), lambda i, *_: (i, 0)),
        scratch_shapes=[pltpu.VMEM((2, tmr, nch, chw), src3.dtype), pltpu.SemaphoreType.DMA((2,))])
    return pl.pallas_call(
        _gather_body, grid_spec=gs,
        out_shape=jax.ShapeDtypeStruct((nt * tmr, nch * chw), bf16),
        compiler_params=_cparams(1),
        name="moe_gather",
    )(tok_sorted, blk_s0, blk_end, n_used, src3)


def _expert_changed(be_ref, i):
    return (i == 0) | (be_ref[i] != be_ref[jnp.maximum(i - 1, 0)])


def _up_body(be_ref, nu_ref, x_ref, wg_ref, wl_ref, bg_ref, bl_ref, o_ref, ws_ref):
    i = pl.program_id(1)
    tn = o_ref.shape[1]
    used = i < nu_ref[0]

    @pl.when(used & _expert_changed(be_ref, i))
    def _():
        ws_ref[:, 0:tn] = wg_ref[0].astype(bf16)
        ws_ref[:, tn:2 * tn] = wl_ref[0].astype(bf16)

    @pl.when(used)
    def _():
        u = jnp.dot(x_ref[...], ws_ref[...], preferred_element_type=f32)
        glu = u[:, 0:tn] + bg_ref[0]
        lin = u[:, tn:2 * tn] + bl_ref[0]
        glu = jnp.minimum(glu, SWIGLU_LIMIT)
        lin = jnp.clip(lin, -SWIGLU_LIMIT, SWIGLU_LIMIT)
        act = glu * (1.0 / (1.0 + jnp.exp(-SWIGLU_ALPHA * glu))) * (lin + 1.0)
        o_ref[...] = act.astype(bf16)

    @pl.when(jnp.logical_not(used))
    def _():
        o_ref[...] = jnp.zeros(o_ref.shape, o_ref.dtype)


def _moe_up(blk_e, n_used, xs, w_up, b_up3, tn=384):
    n_rows, d = xs.shape
    nb = n_rows // MOE_TM
    nj = D_EXPERT // tn
    gs = pltpu.PrefetchScalarGridSpec(
        num_scalar_prefetch=2,
        grid=(nj, nb),
        in_specs=[pl.BlockSpec((MOE_TM, d), lambda j, i, be, nu: (i, 0)),
                  pl.BlockSpec((1, d, tn), lambda j, i, be, nu: (be[i], 0, j)),
                  pl.BlockSpec((1, d, tn), lambda j, i, be, nu: (be[i], 0, nj + j)),
                  pl.BlockSpec((1, 1, tn), lambda j, i, be, nu: (be[i], 0, j)),
                  pl.BlockSpec((1, 1, tn), lambda j, i, be, nu: (be[i], 0, nj + j))],
        out_specs=pl.BlockSpec((MOE_TM, tn), lambda j, i, be, nu: (i, j)),
        scratch_shapes=[pltpu.VMEM((d, 2 * tn), bf16)])
    return pl.pallas_call(
        _up_body, grid_spec=gs,
        out_shape=jax.ShapeDtypeStruct((n_rows, D_EXPERT), bf16),
        compiler_params=_cparams(2),
        name="moe_up",
    )(blk_e, n_used, xs, w_up, w_up, b_up3, b_up3)


def _down_body(be_ref, nu_ref, a_ref, w_ref, b_ref, o_ref, ws_ref):
    i = pl.program_id(1)
    used = i < nu_ref[0]

    @pl.when(used & _expert_changed(be_ref, i))
    def _():
        ws_ref[...] = w_ref[0].astype(bf16)

    @pl.when(used)
    def _():
        y = jnp.dot(a_ref[...], ws_ref[...], preferred_element_type=f32) + b_ref[0]
        nch = o_ref.shape[1]
        for r0 in range(0, o_ref.shape[0], 8):
            blk = jnp.stack([y[r0:r0 + 8, c * ROW_CH:(c + 1) * ROW_CH] for c in range(nch)], axis=0)
            o_ref[r0:r0 + 8] = pltpu.einshape("crl->rcl", blk)

    @pl.when(jnp.logical_not(used))
    def _():
        o_ref[...] = jnp.zeros(o_ref.shape, o_ref.dtype)


def _moe_down(blk_e, n_used, act, w_down, b_down3, tn=1024):
    n_rows, de = act.shape
    d = w_down.shape[2]
    nb = n_rows // MOE_TM
    nj = d // tn
    gs = pltpu.PrefetchScalarGridSpec(
        num_scalar_prefetch=2,
        grid=(nj, nb),
        in_specs=[pl.BlockSpec((MOE_TM, de), lambda j, i, be, nu: (i, 0)),
                  pl.BlockSpec((1, de, tn), lambda j, i, be, nu: (be[i], 0, j)),
                  pl.BlockSpec((1, 1, tn), lambda j, i, be, nu: (be[i], 0, j))],
        out_specs=pl.BlockSpec((MOE_TM, tn // ROW_CH, ROW_CH), lambda j, i, be, nu: (i, j, 0)),
        scratch_shapes=[pltpu.VMEM((de, tn), bf16)])
    return pl.pallas_call(
        _down_body, grid_spec=gs,
        out_shape=jax.ShapeDtypeStruct((n_rows, d // ROW_CH, ROW_CH), f32),
        compiler_params=_cparams(2),
        name="moe_down",
    )(blk_e, n_used, act, w_down, b_down3)


def _combine_body(cur_ref, nxt_ref, w_ref, ys_ref, x_ref, o_ref, buf_ref, tot_ref, sem):
    i = pl.program_id(0)
    nt = pl.num_programs(0)
    tm = o_ref.shape[0]
    nch = buf_ref.shape[3]
    slot = lax.rem(i, 2)
    rows = _RowGather(
        ys_ref, lambda s, a: buf_ref.at[s, a & (TOP_K - 1), lax.shift_right_logical(a, 2)],
        lambda s: [(ys_ref.at[pl.ds(0, tm)], buf_ref.at[s, k]) for k in range(TOP_K)],
        sem, tm * TOP_K)
    cur = lambda a: cur_ref[0, 0, a]

    @pl.when(i == 0)
    def _():
        rows.start(cur, 0)

    @pl.when(i + 1 < nt)
    def _():
        rows.start(lambda a: nxt_ref[0, 0, a], 1 - slot)

    rows.wait(slot)

    def weigh(t, c):
        tot = buf_ref[slot, 0, t] * w_ref[0, 0, TOP_K * t]
        for k in range(1, TOP_K):
            tot = tot + buf_ref[slot, k, t] * w_ref[0, 0, TOP_K * t + k]
        tot_ref[t] = tot
        return c

    lax.fori_loop(0, tm, weigh, 0, unroll=4)
    for r0 in range(0, tm, 8):
        blk = pltpu.einshape("rcl->crl", tot_ref[r0:r0 + 8])
        for c in range(nch):
            cs = slice(c * ROW_CH, (c + 1) * ROW_CH)
            o_ref[r0:r0 + 8, cs] = x_ref[r0:r0 + 8, cs] + blk[c]


def _combine(ys3, pos, top_w, x1, tm=128):
    n, d = x1.shape
    _, nch, chw = ys3.shape
    nt = n // tm
    pos3 = pos.reshape(nt, 1, tm * TOP_K)
    idx_spec = lambda f: pl.BlockSpec((1, 1, tm * TOP_K), f, memory_space=pltpu.SMEM)
    return pl.pallas_call(
        _combine_body,
        grid=(nt,),
        in_specs=[idx_spec(lambda i: (i, 0, 0)),
                  idx_spec(lambda i: (jnp.minimum(i + 1, nt - 1), 0, 0)),
                  idx_spec(lambda i: (i, 0, 0)),
                  pl.BlockSpec(memory_space=pl.ANY),
                  pl.BlockSpec((tm, d), lambda i: (i, 0))],
        out_specs=pl.BlockSpec((tm, d), lambda i: (i, 0)),
        out_shape=jax.ShapeDtypeStruct((n, d), f32),
        scratch_shapes=[pltpu.VMEM((2, TOP_K, tm, nch, chw), f32), pltpu.VMEM((tm, nch, chw), f32),
                        pltpu.SemaphoreType.DMA((2,))],
        compiler_params=_cparams(1),
        name="moe_combine",
    )(pos3, pos3, top_w.reshape(nt, 1, tm * TOP_K), ys3, x1)


def _route(top_idx):
    n = top_idx.shape[0]
    n_assign = n * TOP_K
    e_flat = top_idx.reshape(-1)
    a_iota = jnp.arange(n_assign, dtype=i32)
    _, order = lax.sort((e_flat, a_iota), num_keys=1, is_stable=True)
    _, rank = lax.sort((order, a_iota), num_keys=1)
    experts = jnp.arange(N_EXPERTS, dtype=i32)
    onehot = experts[:, None] == e_flat[None, :]
    counts = jnp.sum(onehot.astype(i32), axis=1)
    padded = (counts + MOE_TM - 1) // MOE_TM * MOE_TM
    start = jnp.cumsum(counts) - counts
    pend = jnp.cumsum(padded)
    shift = pend - padded - start
    pos = rank + jnp.sum(jnp.where(onehot, shift[:, None], 0), axis=0)
    n_blocks = -(-n_assign // MOE_TM) + N_EXPERTS
    n_used = pend[-1] // MOE_TM
    bi = jnp.arange(n_blocks, dtype=i32)
    raw_e = jnp.minimum(jnp.sum((pend[None, :] <= (bi * MOE_TM)[:, None]).astype(i32), axis=1),
                        N_EXPERTS - 1)
    blk_e = jnp.where(bi < n_used, raw_e, raw_e[jnp.maximum(n_used - 1, 0)]).astype(i32)
    blk_s0 = (bi * MOE_TM - shift[blk_e]).astype(i32)
    blk_end = (start[blk_e] + counts[blk_e]).astype(i32)
    tok_sorted = lax.shift_right_logical(order, 2)
    return tok_sorted, blk_e, blk_s0, blk_end, n_used.reshape(1).astype(i32), pos.astype(i32)


NSA_TQ = 256


def _layer(x, ln1_g, w_in, conv_w, conv_b, dt_bias, a_log, d_skip, ssd_norm_g, q_norm_g,
           k_norm_g, cmp_pe, cmp_w1, cmp_w2, w_out, ln2_g, w_router, b_router, w_up, b_up,
           w_down, b_down):
    bsz, t_len, d = x.shape
    n = bsz * t_len
    x2 = x.reshape(n, d)

    o_dt = SSD_D_INNER + SSD_D_INNER + 2 * SSD_GN
    o_q = o_dt + SSD_HEADS
    o_kv = o_q + NSA_WIDTH
    o_g = o_kv + 6 * NSA_KV_WIDTH
    n_in = o_g + 3 * NSA_HEADS
    w_perm = jnp.concatenate(
        [w_in[:, :o_dt], w_in[:, o_q:o_g], w_in[:, o_dt:o_q], w_in[:, o_g:n_in],
         jnp.zeros((d, NP_PROJ - n_in), w_in.dtype)], axis=1).astype(bf16)

    proj = _inproj(x2, ln1_g.reshape(1, d), w_perm)

    lane_pad = lambda v: jnp.concatenate([v, jnp.zeros((LANE - v.shape[0],), f32)]).reshape(1, LANE)
    y_ssd = _ssd(proj, bsz, t_len, conv_w, conv_b.reshape(1, -1), lane_pad(dt_bias),
                 lane_pad(a_log), jnp.repeat(d_skip, SSD_HEAD_DIM).reshape(1, -1),
                 ssd_norm_g.reshape(1, -1))

    kc, vct = _compress(proj, bsz, t_len, cmp_pe, cmp_w1.astype(bf16), cmp_w2.astype(bf16),
                        k_norm_g[0:1])
    qt, ks, vst, kw, vwt = _qk_prep(proj, q_norm_g.reshape(1, DH), k_norm_g, bsz, t_len, NSA_TQ)
    gates = proj[:, COL_DT + GATE_OFF:COL_DT + GATE_OFF + 3 * NSA_HEADS]
    gates = gates.reshape(n, NSA_KV_HEADS, 3 * NSA_GQA).transpose(1, 2, 0)
    y_nsa = _nsa_attention(qt, kc, vct, ks, vst, kw, vwt, gates, bsz, t_len, NSA_TQ)

    wo = w_out.astype(bf16)
    wr = jnp.concatenate([w_router, jnp.zeros((d, LANE - N_EXPERTS), f32)], axis=1)
    wr_hi = wr.astype(bf16)
    wr = jnp.concatenate([wr_hi, (wr - wr_hi.astype(f32)).astype(bf16)], axis=1)
    br =jnp.concatenate([b_router, jnp.zeros((LANE - N_EXPERTS,), f32)]).reshape(1, LANE)
    x1 = _outproj(y_ssd, y_nsa, wo[:SSD_D_INNER], wo[SSD_D_INNER:], x2)
    h3, ti, tw = _norm_router(x1, ln2_g.reshape(1, d), wr, br)

    tok_sorted, blk_e, blk_s0, blk_end, n_used, pos = _route(ti[:, :TOP_K])
    xs = _gather_tokens(h3, tok_sorted, blk_s0, blk_end, n_used, MOE_TM)
    act = _moe_up(blk_e, n_used, xs, w_up, b_up.reshape(N_EXPERTS, 1, -1))
    ys3 = _moe_down(blk_e, n_used, act, w_down, b_down.reshape(N_EXPERTS, 1, -1))
    out = _combine(ys3, pos, tw[:, :TOP_K], x1)
    return out.reshape(bsz, t_len, d)


def kernel(x, ln1_g, w_in, conv_w, conv_b, dt_bias, a_log, d_skip, ssd_norm_g, q_norm_g,
           k_norm_g, cmp_pe, cmp_w1, cmp_w2, w_out, ln2_g, w_router, b_router, w_up, b_up,
           w_down, b_down):
    return _layer(x, ln1_g[0], w_in[0], conv_w[0], conv_b[0], dt_bias[0], a_log[0], d_skip[0],
                  ssd_norm_g[0], q_norm_g[0], k_norm_g[0], cmp_pe[0], cmp_w1[0], cmp_w2[0],
                  w_out[0], ln2_g[0], w_router[0], b_router[0], w_up[0], b_up[0], w_down[0],
                  b_down[0])
```

```python
import functools
import math

import jax
import jax.numpy as jnp
from jax import lax
from jax.experimental import pallas as pl
from jax.experimental.pallas import tpu as pltpu

f32 = jnp.float32
bf16 = jnp.bfloat16
i32 = jnp.int32

D_MODEL = 4096
SSD_D_INNER = 2048
SSD_HEAD_DIM = 64
SSD_HEADS = 32
SSD_GROUPS = 8
SSD_HPG = 4
SSD_STATE = 128
SSD_CONV = 4
SSD_CHUNK = 256
SSD_GN = SSD_GROUPS * SSD_STATE
SSD_GW = SSD_HPG * SSD_HEAD_DIM
NSA_HEADS = 16
NSA_KV_HEADS = 4
NSA_GQA = 4
DH = 128
NSA_WIDTH = 2048
NSA_KV_WIDTH = 512
CMP_BLOCK = 32
CMP_STRIDE = 16
CMP_HIDDEN = 256
SLC_BLOCK = 64
SLC_TOPK = 16
WINDOW = 512
N_EXPERTS = 32
TOP_K = 4
D_EXPERT = 1536
SWIGLU_LIMIT = 7.0
SWIGLU_ALPHA = 1.702
EPS = 1e-5
NEG = -1e30
LOG2E = 1.4426950408889634
SLC_GROUP = 4
VT_PAD = 16

COL_Z = 0
COL_X = 2048
COL_B = 4096
COL_C = 5120
COL_Q = 6144
COL_KV = 8192
COL_DT = 11264
GATE_OFF = 32
NP_PROJ = 11520

LANE = 128
VMEM_LIMIT = 56 * 1024 * 1024


def _cparams(n_axes):
    return pltpu.CompilerParams(dimension_semantics=("arbitrary",) * n_axes,
                                vmem_limit_bytes=VMEM_LIMIT)


def _silu(v):
    return v * (1.0 / (1.0 + jnp.exp(-v)))


def _inproj_body(x_ref, g_ref, w_ref, o_ref, h_ref):
    @pl.when(pl.program_id(1) == 0)
    def _():
        x = x_ref[...]
        ms = jnp.mean(x * x, axis=-1, keepdims=True)
        h_ref[...] = (x * lax.rsqrt(ms + EPS) * g_ref[...]).astype(bf16)

    o_ref[...] = jnp.dot(h_ref[...], w_ref[...], preferred_element_type=f32)


def _inproj(x2, g, w, tm=512, tn=768):
    n, d = x2.shape
    npj = w.shape[1]
    return pl.pallas_call(
        _inproj_body,
        grid=(n // tm, npj // tn),
        in_specs=[pl.BlockSpec((tm, d), lambda i, j: (i, 0)),
                  pl.BlockSpec((1, d), lambda i, j: (0, 0)),
                  pl.BlockSpec((d, tn), lambda i, j: (0, j))],
        out_specs=pl.BlockSpec((tm, tn), lambda i, j: (i, j)),
        out_shape=jax.ShapeDtypeStruct((n, npj), f32),
        scratch_shapes=[pltpu.VMEM((tm, d), bf16)],
        compiler_params=_cparams(2),
        name="inproj",
    )(x2, g, w)


def _lane_expand(cols, width):
    L = cols[0].shape[0]
    n = len(cols)
    lane = lax.broadcasted_iota(i32, (L, n * width), 1)
    out = jnp.broadcast_to(cols[n - 1], (L, n * width))
    for j in range(n - 2, -1, -1):
        out = jnp.where(lane < (j + 1) * width, jnp.broadcast_to(cols[j], (L, n * width)), out)
    return out


def _ssd_body(z_ref, x_ref, b_ref, c_ref, dt_ref, cw_ref, cb_ref, dtb_ref, alog_ref,
              dskip_ref, ng_ref, y_ref, buf_ref, xs_ref, bt_ref, cs_ref, h_ref):
    L = SSD_CHUNK
    c_idx = pl.program_id(1)

    @pl.when(c_idx == 0)
    def _():
        buf_ref[0:8, :] = jnp.zeros((8, buf_ref.shape[1]), f32)
        h_ref[...] = jnp.zeros(h_ref.shape, f32)

    buf_ref[8:8 + L, 0:SSD_D_INNER] = x_ref[...]
    buf_ref[8:8 + L, SSD_D_INNER:SSD_D_INNER + SSD_GN] = b_ref[...]
    buf_ref[8:8 + L, SSD_D_INNER + SSD_GN:] = c_ref[...]

    cw = 512
    n_ch = buf_ref.shape[1]
    for c0 in range(0, n_ch, cw):
        acc = jnp.broadcast_to(cb_ref[:, c0:c0 + cw], (L, cw))
        for k in range(SSD_CONV):
            acc = acc + cw_ref[k:k + 1, c0:c0 + cw] * buf_ref[5 + k:5 + k + L, c0:c0 + cw]
        v = _silu(acc)
        if c0 < SSD_D_INNER:
            xs_ref[:, c0:c0 + cw] = v
        elif c0 < SSD_D_INNER + SSD_GN:
            bt_ref[c0 - SSD_D_INNER:c0 - SSD_D_INNER + cw, :] = v.T.astype(bf16)
        else:
            o = c0 - SSD_D_INNER - SSD_GN
            cs_ref[:, o:o + cw] = v.astype(bf16)
    buf_ref[0:8, :] = buf_ref[L:L + 8, :]

    dt = dt_ref[...] + dtb_ref[...]
    dt = jnp.maximum(dt, 0.0) + jnp.log(1.0 + jnp.exp(-jnp.abs(dt)))
    da = dt * -jnp.exp(alog_ref[...])
    r = lax.broadcasted_iota(i32, (L, L), 0)
    s = lax.broadcasted_iota(i32, (L, L), 1)
    tri = (s <= r).astype(f32)
    acs = jnp.dot(tri, da, preferred_element_type=f32,
                  precision=lax.Precision.HIGHEST)
    acs_t = acs.T
    causal = s <= r
    a_last = acs[L - 1:L, :]
    e_acs = jnp.exp(acs)
    e_rem = jnp.exp(a_last - acs)
    e_last = jnp.exp(a_last)
    lane_gw = lax.broadcasted_iota(i32, (L, SSD_GW), 1)

    for g in range(SSD_GROUPS):
        hs = [g * SSD_HPG + j for j in range(SSD_HPG)]
        xg = xs_ref[:, g * SSD_GW:(g + 1) * SSD_GW]
        bgt = bt_ref[g * SSD_STATE:(g + 1) * SSD_STATE, :]
        cg = cs_ref[:, g * SSD_STATE:(g + 1) * SSD_STATE]
        dt_x = _lane_expand([dt[:, h:h + 1] for h in hs], SSD_HEAD_DIM)
        xdt = xg * dt_x
        cbm = jnp.dot(cg, bgt, preferred_element_type=f32)
        y = jnp.zeros((L, SSD_GW), f32)
        for j, h in enumerate(hs):
            seg = jnp.where(causal, acs[:, h:h + 1] - acs_t[h:h + 1, :], NEG)
            w = (cbm * jnp.exp(seg)).astype(bf16)
            band = (lane_gw >= j * SSD_HEAD_DIM) & (lane_gw < (j + 1) * SSD_HEAD_DIM)
            xm = jnp.where(band, xdt, 0.0).astype(bf16)
            y = y + jnp.dot(w, xm, preferred_element_type=f32)
        hprev = h_ref[g]
        y_off = jnp.dot(cg, hprev.astype(bf16), preferred_element_type=f32)
        y = y + y_off * _lane_expand([e_acs[:, h:h + 1] for h in hs], SSD_HEAD_DIM)
        xw = (xdt * _lane_expand([e_rem[:, h:h + 1] for h in hs], SSD_HEAD_DIM)).astype(bf16)
        st = jnp.dot(bgt, xw, preferred_element_type=f32)
        dec = _lane_expand([e_last[:, h:h + 1] for h in hs], SSD_HEAD_DIM)
        h_ref[g] = hprev * dec + st
        y = y + xg * dskip_ref[:, g * SSD_GW:(g + 1) * SSD_GW]
        y = y * _silu(z_ref[:, g * SSD_GW:(g + 1) * SSD_GW])
        y = y * lax.rsqrt(jnp.mean(y * y, axis=-1, keepdims=True) + EPS)
        y_ref[:, g * SSD_GW:(g + 1) * SSD_GW] = (y * ng_ref[:, g * SSD_GW:(g + 1) * SSD_GW]).astype(bf16)


def _ssd(proj, bsz, t_len, conv_w, conv_b, dt_bias, a_log, dskip_x, norm_g):
    L = SSD_CHUNK
    nc = t_len // L
    n = bsz * t_len
    nch = SSD_D_INNER + 2 * SSD_GN
    row = lambda b, c: b * nc + c
    full = lambda shape: pl.BlockSpec(shape, lambda b, c: (0,) * len(shape))
    return pl.pallas_call(
        _ssd_body,
        grid=(bsz, nc),
        in_specs=[pl.BlockSpec((L, SSD_D_INNER), lambda b, c: (row(b, c), COL_Z // SSD_D_INNER)),
                  pl.BlockSpec((L, SSD_D_INNER), lambda b, c: (row(b, c), COL_X // SSD_D_INNER)),
                  pl.BlockSpec((L, SSD_GN), lambda b, c: (row(b, c), COL_B // SSD_GN)),
                  pl.BlockSpec((L, SSD_GN), lambda b, c: (row(b, c), COL_C // SSD_GN)),
                  pl.BlockSpec((L, LANE), lambda b, c: (row(b, c), COL_DT // LANE)),
                  full((SSD_CONV, nch)), full((1, nch)), full((1, LANE)),
                  full((1, LANE)), full((1, SSD_D_INNER)), full((1, SSD_D_INNER))],
        out_specs=pl.BlockSpec((L, SSD_D_INNER), lambda b, c: (row(b, c), 0)),
        out_shape=jax.ShapeDtypeStruct((n, SSD_D_INNER), bf16),
        scratch_shapes=[pltpu.VMEM((L + 8, nch), f32),
                        pltpu.VMEM((L, SSD_D_INNER), f32),
                        pltpu.VMEM((SSD_GN, L), bf16),
                        pltpu.VMEM((L, SSD_GN), bf16),
                        pltpu.VMEM((SSD_GROUPS, SSD_STATE, SSD_GW), f32)],
        compiler_params=_cparams(2),
        name="ssd",
    )(proj, proj, proj, proj, proj, conv_w, conv_b, dt_bias, a_log, dskip_x, norm_g)


def _cmp_body(uk_ref, uv_ref, pe_ref, w1_ref, w2_ref, g_ref, kc_ref, vct_ref):
    nchunk = uk_ref.shape[0]

    def branch(u_ref, kv):
        hid_a = jnp.zeros((nchunk, CMP_HIDDEN), f32)
        hid_b = jnp.zeros((nchunk, CMP_HIDDEN), f32)
        ut = pltpu.einshape("csd->scd", u_ref[...])
        for s in range(CMP_STRIDE):
            us = ut[s]
            hid_a = hid_a + jnp.dot((us + pe_ref[kv, s:s + 1, :]).astype(bf16), w1_ref[kv, s],
                                    preferred_element_type=f32)
            hid_b = hid_b + jnp.dot(
                (us + pe_ref[kv, CMP_STRIDE + s:CMP_STRIDE + s + 1, :]).astype(bf16),
                w1_ref[kv, CMP_STRIDE + s], preferred_element_type=f32)
        hid = hid_a + pltpu.roll(hid_b, nchunk - 1, 0)
        out = jnp.dot(_silu(hid).astype(bf16), w2_ref[kv], preferred_element_type=f32)
        rowi = lax.broadcasted_iota(i32, out.shape, 0)
        return jnp.where(rowi == nchunk - 1, 0.0, out)

    k = branch(uk_ref, 0)
    k = k * lax.rsqrt(jnp.mean(k * k, axis=-1, keepdims=True) + EPS) * g_ref[...]
    kc_ref[0, 0] = k.astype(bf16)
    vct_ref[0, 0] = branch(uv_ref, 1).T.astype(bf16)


def _compress(proj, bsz, t_len, pe, w1, w2, kg):
    nchunk = t_len // CMP_STRIDE
    H = NSA_KV_HEADS
    u3 = proj.reshape(bsz * nchunk, CMP_STRIDE, proj.shape[1])
    useg = lambda kv: pl.BlockSpec((nchunk, CMP_STRIDE, DH),
                                   lambda b, h: (b, 0, COL_KV // DH + kv * H + h))
    full = lambda shape: pl.BlockSpec(shape, lambda b, h: (0,) * len(shape))
    return pl.pallas_call(
        _cmp_body,
        grid=(bsz, H),
        in_specs=[useg(0), useg(1), full((2, CMP_BLOCK, DH)), full((2, CMP_BLOCK, DH, CMP_HIDDEN)),
                  full((2, CMP_HIDDEN, DH)), full((1, DH))],
        out_specs=[pl.BlockSpec((1, 1, nchunk, DH), lambda b, h: (b, h, 0, 0)),
                   pl.BlockSpec((1, 1, DH, nchunk), lambda b, h: (b, h, 0, 0))],
        out_shape=[jax.ShapeDtypeStruct((bsz, H, nchunk, DH), bf16),
                   jax.ShapeDtypeStruct((bsz, H, DH, nchunk), bf16)],
        compiler_params=_cparams(2),
        name="nsa_compress",
    )(u3, u3, pe, w1, w2, kg)


def _qk_prep_body(q_ref, s_ref, w_ref, qg_ref, kg_ref, qt_ref, ks_ref, vst_ref, kw_ref, vwt_ref):
    scale = DH ** -0.5 * LOG2E
    tq = q_ref.shape[0]

    def hnorm(v, g):
        return v * lax.rsqrt(jnp.mean(v * v, axis=-1, keepdims=True) + EPS) * g

    ones_rows = (lax.broadcasted_iota(i32, (VT_PAD, tq), 0) == 0).astype(bf16)
    for h in range(NSA_KV_HEADS):
        for g in range(NSA_GQA):
            sl = slice((h * NSA_GQA + g) * DH, (h * NSA_GQA + g + 1) * DH)
            qn = hnorm(q_ref[:, sl], qg_ref[...]) * scale
            qt_ref[0, h, 0, :, g * tq:(g + 1) * tq] = qn.T.astype(bf16)
        sl = slice(h * DH, (h + 1) * DH)
        sv = slice(NSA_KV_WIDTH + h * DH, NSA_KV_WIDTH + (h + 1) * DH)
        ks_ref[:, sl] = hnorm(s_ref[:, sl], kg_ref[1:2, :]).astype(bf16)
        kw_ref[:, sl] = hnorm(w_ref[:, sl], kg_ref[2:3, :]).astype(bf16)
        vst_ref[0, h, 0, 0:DH] = s_ref[:, sv].T.astype(bf16)
        vwt_ref[0, h, 0, 0:DH] = w_ref[:, sv].T.astype(bf16)
        vst_ref[0, h, 0, DH:DH + VT_PAD] = ones_rows
        vwt_ref[0, h, 0, DH:DH + VT_PAD] = ones_rows


def _qk_prep(proj, qg, kg, bsz, t_len, tq):
    n = proj.shape[0]
    nq = t_len // tq
    H = NSA_KV_HEADS
    w2 = 2 * NSA_KV_WIDTH
    row = lambda b, i: b * nq + i
    vt_spec = pl.BlockSpec((1, H, 1, DH + VT_PAD, tq), lambda b, i: (b, 0, i, 0, 0))
    vt_shape = jax.ShapeDtypeStruct((bsz, H, nq, DH + VT_PAD, tq), bf16)
    return pl.pallas_call(
        _qk_prep_body,
        grid=(bsz, nq),
        in_specs=[pl.BlockSpec((tq, NSA_WIDTH), lambda b, i: (row(b, i), COL_Q // NSA_WIDTH)),
                  pl.BlockSpec((tq, w2), lambda b, i: (row(b, i), (COL_KV + w2) // w2)),
                  pl.BlockSpec((tq, w2), lambda b, i: (row(b, i), (COL_KV + 2 * w2) // w2)),
                  pl.BlockSpec((1, DH), lambda b, i: (0, 0)),
                  pl.BlockSpec((3, DH), lambda b, i: (0, 0))],
        out_specs=[pl.BlockSpec((1, H, 1, DH, NSA_GQA * tq), lambda b, i: (b, 0, i, 0, 0)),
                   pl.BlockSpec((tq, NSA_KV_WIDTH), lambda b, i: (row(b, i), 0)),
                   vt_spec,
                   pl.BlockSpec((tq, NSA_KV_WIDTH), lambda b, i: (row(b, i), 0)),
                   vt_spec],
        out_shape=[jax.ShapeDtypeStruct((bsz, H, nq, DH, NSA_GQA * tq), bf16),
                   jax.ShapeDtypeStruct((n, NSA_KV_WIDTH), bf16), vt_shape,
                   jax.ShapeDtypeStruct((n, NSA_KV_WIDTH), bf16), vt_shape],
        compiler_params=_cparams(2),
        name="nsa_qk_prep",
    )(proj, proj, proj, qg, kg)


def _nsa_body(qt_ref, kc_ref, vct_ref, ks_ref, vst_ref, kw_ref, vwt_ref, gt_ref, o_ref,
              m_ref, acc_ref, sel_ref, *, tq):
    i = pl.program_id(2)
    G = NSA_GQA
    n_cmp = kc_ref.shape[2]
    n_slc = ks_ref.shape[0] // SLC_BLOCK

    qt = qt_ref[0, 0, 0]
    t_row = i * tq + lax.broadcasted_iota(i32, (1, tq), 1)
    t_all = jnp.concatenate([t_row] * G, axis=1)

    s_c = jnp.dot(kc_ref[0, 0], qt, preferred_element_type=f32)
    cend = lax.broadcasted_iota(i32, (n_cmp, 1), 0) * CMP_STRIDE + (CMP_BLOCK - 1)
    m_c = cend <= t_all
    s_c = jnp.where(m_c, s_c, NEG)
    p_c = jnp.where(m_c, jnp.exp2(s_c - jnp.max(s_c, axis=0, keepdims=True)), 0.0)
    p_c = p_c / jnp.maximum(jnp.sum(p_c, axis=0, keepdims=True), 1e-30)
    o_c = jnp.dot(vct_ref[0, 0], p_c.astype(bf16), preferred_element_type=f32)

    imp = p_c[:, 0:tq]
    for g in range(1, G):
        imp = imp + p_c[:, g * tq:(g + 1) * tq]
    per = SLC_BLOCK // CMP_STRIDE
    ni = lax.broadcasted_iota(i32, (n_slc, n_cmp), 0)
    ci = lax.broadcasted_iota(i32, (n_slc, n_cmp), 1)
    fold = ((ci // per == ni).astype(f32) + ((ci + 1) // per == ni).astype(f32))
    imp_b = jnp.dot(fold, imp, preferred_element_type=f32, precision=lax.Precision.HIGHEST)
    jb = lax.broadcasted_iota(i32, (n_slc, tq), 0)
    cur = t_row // SLC_BLOCK
    forced = (jb == 0) | (jb == cur) | (jb == cur - 1)
    val = jnp.where(forced, 1e30, jnp.where(jb <= cur, imp_b, -1.0))
    sel = jnp.zeros((n_slc, tq), f32)
    for _ in range(SLC_TOPK):
        mx = jnp.max(val, axis=0, keepdims=True)
        first = jnp.min(jnp.where(val == mx, jb, n_slc), axis=0, keepdims=True)
        hit = (jb == first) & (mx >= 0.0)
        sel = jnp.where(hit, 1.0, sel)
        val = jnp.where(jb == first, -1.0, val)

    def online(tiles, slot):
        scores = [jnp.dot(k, qt, preferred_element_type=f32) + jnp.concatenate([bias] * G, axis=1)
                  for k, _, bias in tiles]
        m_old = m_ref[slot]
        m_new = m_old
        for s_ in scores:
            m_new = jnp.maximum(m_new, jnp.max(s_, axis=0, keepdims=True))
        acc = jnp.exp2(m_old - m_new) * acc_ref[slot]
        for s_, (_, vt, _) in zip(scores, tiles):
            acc = acc + jnp.dot(vt, jnp.exp2(s_ - m_new).astype(bf16), preferred_element_type=f32)
        acc_ref[slot] = acc
        m_ref[slot] = m_new

    m_ref[...] = jnp.full(m_ref.shape, NEG, f32)
    acc_ref[...] = jnp.zeros(acc_ref.shape, f32)
    krow = lax.broadcasted_iota(i32, (tq, 1), 0)
    kcol = lax.broadcasted_iota(i32, (1, tq), 1)

    sel_ref[...] = jnp.where(sel > 0.5, 0.0, NEG)
    bpt = tq // SLC_BLOCK

    def block_bias(kt):
        rows = [jnp.broadcast_to(sel_ref[pl.ds(kt * bpt + j, 1), :], (SLC_BLOCK, tq))
                for j in range(bpt)]
        return jnp.concatenate(rows, axis=0)

    def slc_tile(kt, causal=False):
        bias = block_bias(kt)
        if causal:
            bias = jnp.where(krow <= kcol, bias, NEG)
        return ks_ref[pl.ds(pl.multiple_of(kt * tq, tq), tq), :], vst_ref[0, 0, kt], bias

    def slc_group(kg, carry):
        online([slc_tile(SLC_GROUP * kg + j) for j in range(SLC_GROUP)], 0)
        return carry

    n_grp = i // SLC_GROUP
    lax.fori_loop(0, n_grp, slc_group, 0)
    for r in range(SLC_GROUP):
        @pl.when(i - n_grp * SLC_GROUP == r)
        def _():
            online([slc_tile(n_grp * SLC_GROUP + j) for j in range(r)] + [slc_tile(i, causal=True)], 0)

    def win_tile(kt):
        k0 = pl.multiple_of(kt * tq, tq)
        key = k0 + krow
        ok = (key <= t_row) & (key > t_row - WINDOW)
        return kw_ref[pl.ds(k0, tq), :], vwt_ref[0, 0, kt], jnp.where(ok, 0.0, NEG)

    back = WINDOW // tq
    for c in range(back + 1):
        @pl.when(jnp.minimum(i, back) == c)
        def _():
            online([win_tile(i - c + j) for j in range(c + 1)], 1)

    o_s = acc_ref[0, 0:DH] / acc_ref[0, DH:DH + 1]
    o_w = acc_ref[1, 0:DH] / acc_ref[1, DH:DH + 1]
    gts = 1.0 / (1.0 + jnp.exp(-gt_ref[0]))
    for g in range(G):
        cs = slice(g * tq, (g + 1) * tq)
        out_t = (gts[3 * g:3 * g + 1, :] * o_c[:, cs] + gts[3 * g + 1:3 * g + 2, :] * o_s[:, cs]
                 + gts[3 * g + 2:3 * g + 3, :] * o_w[:, cs])
        o_ref[:, g * DH:(g + 1) * DH] = out_t.T.astype(bf16)


def _nsa_attention(qt, kc, vct, ks, vst, kw, vwt, gates, bsz, t_len, tq):
    n = bsz * t_len
    nq = t_len // tq
    nchunk = t_len // CMP_STRIDE
    R = NSA_GQA * tq
    seq = pl.BlockSpec((t_len, DH), lambda b, h, i: (b, h))
    seq_t = pl.BlockSpec((1, 1, nq, DH + VT_PAD, tq), lambda b, h, i: (b, h, 0, 0, 0))
    return pl.pallas_call(
        functools.partial(_nsa_body, tq=tq),
        grid=(bsz, NSA_KV_HEADS, nq),
        in_specs=[pl.BlockSpec((1, 1, 1, DH, R), lambda b, h, i: (b, h, i, 0, 0)),
                  pl.BlockSpec((1, 1, nchunk, DH), lambda b, h, i: (b, h, 0, 0)),
                  pl.BlockSpec((1, 1, DH, nchunk), lambda b, h, i: (b, h, 0, 0)),
                  seq, seq_t, seq, seq_t,
                  pl.BlockSpec((1, 3 * NSA_GQA, tq), lambda b, h, i: (h, 0, b * nq + i))],
        out_specs=pl.BlockSpec((tq, NSA_GQA * DH), lambda b, h, i: (b * nq + i, h)),
        out_shape=jax.ShapeDtypeStruct((n, NSA_WIDTH), bf16),
        scratch_shapes=[pltpu.VMEM((2, 1, R), f32), pltpu.VMEM((2, DH + VT_PAD, R), f32),
                        pltpu.VMEM((t_len // SLC_BLOCK, tq), f32)],
        compiler_params=_cparams(3),
        name="nsa_attention",
    )(qt, kc, vct, ks, vst, kw, vwt, gates)


ROW_CH = LANE


def _outproj_body(ya_ref, yb_ref, wa_ref, wb_ref, x_ref, x1_ref):
    acc = x_ref[...] + jnp.dot(ya_ref[...], wa_ref[...], preferred_element_type=f32)
    x1_ref[...] = acc + jnp.dot(yb_ref[...], wb_ref[...], preferred_element_type=f32)


def _outproj(ya, yb, wa, wb, x2, tm=512, tn=1024):
    n, d = x2.shape
    ka = ya.shape[1]
    kb = yb.shape[1]
    return pl.pallas_call(
        _outproj_body,
        grid=(n // tm, d // tn),
        in_specs=[pl.BlockSpec((tm, ka), lambda i, j: (i, 0)),
                  pl.BlockSpec((tm, kb), lambda i, j: (i, 0)),
                  pl.BlockSpec((ka, tn), lambda i, j: (0, j)),
                  pl.BlockSpec((kb, tn), lambda i, j: (0, j)),
                  pl.BlockSpec((tm, tn), lambda i, j: (i, j))],
        out_specs=pl.BlockSpec((tm, tn), lambda i, j: (i, j)),
        out_shape=jax.ShapeDtypeStruct((n, d), f32),
        compiler_params=_cparams(2),
        name="outproj",
    )(ya, yb, wa, wb, x2)


def _router_body(x_ref, g_ref, wr_ref, br_ref, h_ref, ti_ref, tw_ref):
    tm, d = x_ref.shape
    x = x_ref[...]
    hh = x * lax.rsqrt(jnp.mean(x * x, axis=-1, keepdims=True) + EPS) * g_ref[...]
    hi = hh.astype(bf16)
    lo = (hh - hi.astype(f32)).astype(bf16)
    t = jnp.dot(hi, wr_ref[...], preferred_element_type=f32)
    logits = br_ref[...] + (t[:, :LANE] + (t[:, LANE:] + jnp.dot(lo, wr_ref[:, 0:LANE],
                                                                  preferred_element_type=f32)))
    for r0 in range(0, tm, 8):
        blk = jnp.stack([hh[r0:r0 + 8, c * ROW_CH:(c + 1) * ROW_CH] for c in range(d // ROW_CH)], axis=0)
        h_ref[r0:r0 + 8] = pltpu.einshape("crl->rcl", blk)
    lane = lax.broadcasted_iota(i32, (tm, LANE), 1)
    val = jnp.where(lane < N_EXPERTS, logits, -jnp.inf)
    idxs = jnp.zeros((tm, LANE), i32)
    vals = jnp.full((tm, LANE), -jnp.inf, f32)
    for k in range(TOP_K):
        mx = jnp.max(val, axis=-1, keepdims=True)
        first = jnp.min(jnp.where(val == mx, lane, LANE), axis=-1, keepdims=True)
        idxs = jnp.where(lane == k, first, idxs)
        vals = jnp.where(lane == k, mx, vals)
        val = jnp.where(lane == first, -jnp.inf, val)
    e = jnp.exp(vals - jnp.max(vals, axis=-1, keepdims=True))
    tw_ref[...] = e / jnp.sum(e, axis=-1, keepdims=True)
    ti_ref[...] = idxs


def _norm_router(x1, g2, wr, br, tm=256):
    n, d = x1.shape
    return pl.pallas_call(
        _router_body,
        grid=(n // tm,),
        in_specs=[pl.BlockSpec((tm, d), lambda i: (i, 0)),
                  pl.BlockSpec((1, d), lambda i: (0, 0)),
                  pl.BlockSpec((d, 2 * LANE), lambda i: (0, 0)),
                  pl.BlockSpec((1, LANE), lambda i: (0, 0))],
        out_specs=[pl.BlockSpec((tm, d // ROW_CH, ROW_CH), lambda i: (i, 0, 0)),
                   pl.BlockSpec((tm, LANE), lambda i: (i, 0)),
                   pl.BlockSpec((tm, LANE), lambda i: (i, 0))],
        out_shape=[jax.ShapeDtypeStruct((n, d // ROW_CH, ROW_CH), f32),
                   jax.ShapeDtypeStruct((n, LANE), i32),
                   jax.ShapeDtypeStruct((n, LANE), f32)],
        compiler_params=_cparams(1),
        name="norm_router",
    )(x1, g2, wr, br)


MOE_TM = 512
DMA_UNROLL = 8
DMA_THREADS = 2


class _RowGather:
    def __init__(self, src_ref, dst_of, bulk_of, sem, count):
        self.src_ref, self.dst_of, self.bulk_of, self.sem, self.count = src_ref, dst_of, bulk_of, sem, count

    def start(self, idx_of, slot):
        def body(t, c):
            for p in range(DMA_THREADS):
                r = DMA_THREADS * t + p
                pltpu.make_async_copy(self.src_ref.at[idx_of(r)], self.dst_of(slot, r),
                                      self.sem.at[slot]).start(priority=p)
            return c
        lax.fori_loop(0, self.count // DMA_THREADS, body, 0, unroll=DMA_UNROLL // DMA_THREADS)

    def wait(self, slot):
        for src, dst in self.bulk_of(slot):
            pltpu.make_async_copy(src, dst, self.sem.at[slot]).wait()


def _gather_body(tok_ref, s0_ref, end_ref, nu_ref, src_ref, o_ref, buf_ref, sem):
    i = pl.program_id(0)
    tmr = o_ref.shape[0]
    nch = buf_ref.shape[2]
    slot = lax.rem(i, 2)
    n_live = jnp.minimum(nu_ref[0], pl.num_programs(0))

    def tokens_of(tile):
        s0 = s0_ref[tile]
        last = end_ref[tile] - 1
        return lambda r: tok_ref[jnp.minimum(s0 + r, last)]

    rows = _RowGather(src_ref, lambda s, r: buf_ref.at[s, r],
                      lambda s: [(src_ref.at[pl.ds(0, tmr)], buf_ref.at[s])], sem, tmr)

    @pl.when((i == 0) & (n_live > 0))
    def _():
        rows.start(tokens_of(0), 0)

    @pl.when(i + 1 < n_live)
    def _():
        rows.start(tokens_of(i + 1), 1 - slot)

    @pl.when(i < n_live)
    def _():
        rows.wait(slot)
        for r0 in range(0, tmr, 8):
            blk = pltpu.einshape("rcl->crl", buf_ref[slot, r0:r0 + 8])
            for c in range(nch):
                o_ref[r0:r0 + 8, c * ROW_CH:(c + 1) * ROW_CH] = blk[c].astype(o_ref.dtype)

    @pl.when(i >= n_live)
    def _():
        o_ref[...] = jnp.zeros(o_ref.shape, o_ref.dtype)


def _gather_tokens(src3, tok_sorted, blk_s0, blk_end, n_used, tmr):
    _, nch, chw = src3.shape
    nt = blk_s0.shape[0]
    gs = pltpu.PrefetchScalarGridSpec(
        num_scalar_prefetch=4,
        grid=(nt,),
        in_specs=[pl.BlockSpec(memory_space=pl.ANY)],
        out_specs=pl.BlockSpec((tmr, nch * chw), lambda i, *_: (i, 0)),
        scratch_shapes=[pltpu.VMEM((2, tmr, nch, chw), src3.dtype), pltpu.SemaphoreType.DMA((2,))])
    return pl.pallas_call(
        _gather_body, grid_spec=gs,
        out_shape=jax.ShapeDtypeStruct((nt * tmr, nch * chw), bf16),
        compiler_params=_cparams(1),
        name="moe_gather",
    )(tok_sorted, blk_s0, blk_end, n_used, src3)


def _expert_changed(be_ref, i):
    return (i == 0) | (be_ref[i] != be_ref[jnp.maximum(i - 1, 0)])


def _row_cases(nv, tm, compute, o_ref):
    half = tm // 2

    @pl.when(nv > half)
    def _():
        compute(tm)

    @pl.when((nv > 0) & (nv <= half))
    def _():
        compute(half)
        o_ref[half:] = jnp.zeros((tm - half,) + o_ref.shape[1:], o_ref.dtype)

    @pl.when(nv <= 0)
    def _():
        o_ref[...] = jnp.zeros(o_ref.shape, o_ref.dtype)


def _up_body(be_ref, nv_ref, x_ref, wg_ref, wl_ref, bg_ref, bl_ref, o_ref, ws_ref):
    i = pl.program_id(1)
    tn = o_ref.shape[1]
    nv = nv_ref[i]

    @pl.when((nv > 0) & _expert_changed(be_ref, i))
    def _():
        ws_ref[:, 0:tn] = wg_ref[0].astype(bf16)
        ws_ref[:, tn:2 * tn] = wl_ref[0].astype(bf16)

    def compute(m):
        u = jnp.dot(x_ref[0:m], ws_ref[...], preferred_element_type=f32)
        glu = u[:, 0:tn] + bg_ref[0]
        lin = u[:, tn:2 * tn] + bl_ref[0]
        glu = jnp.minimum(glu, SWIGLU_LIMIT)
        lin = jnp.clip(lin, -SWIGLU_LIMIT, SWIGLU_LIMIT)
        act = glu * (1.0 / (1.0 + jnp.exp(-SWIGLU_ALPHA * glu))) * (lin + 1.0)
        o_ref[0:m] = act.astype(bf16)

    _row_cases(nv, o_ref.shape[0], compute, o_ref)


def _moe_up(blk_e, blk_nv, xs, w_up, b_up3, tn=384):
    n_rows, d = xs.shape
    nb = n_rows // MOE_TM
    nj = D_EXPERT // tn
    gs = pltpu.PrefetchScalarGridSpec(
        num_scalar_prefetch=2,
        grid=(nj, nb),
        in_specs=[pl.BlockSpec((MOE_TM, d), lambda j, i, be, nu: (i, 0)),
                  pl.BlockSpec((1, d, tn), lambda j, i, be, nu: (be[i], 0, j)),
                  pl.BlockSpec((1, d, tn), lambda j, i, be, nu: (be[i], 0, nj + j)),
                  pl.BlockSpec((1, 1, tn), lambda j, i, be, nu: (be[i], 0, j)),
                  pl.BlockSpec((1, 1, tn), lambda j, i, be, nu: (be[i], 0, nj + j))],
        out_specs=pl.BlockSpec((MOE_TM, tn), lambda j, i, be, nu: (i, j)),
        scratch_shapes=[pltpu.VMEM((d, 2 * tn), bf16)])
    return pl.pallas_call(
        _up_body, grid_spec=gs,
        out_shape=jax.ShapeDtypeStruct((n_rows, D_EXPERT), bf16),
        compiler_params=_cparams(2),
        name="moe_up",
    )(blk_e, blk_nv, xs, w_up, w_up, b_up3, b_up3)


def _down_body(be_ref, nv_ref, a_ref, w_ref, b_ref, o_ref, ws_ref):
    i = pl.program_id(1)
    nv = nv_ref[i]

    @pl.when((nv > 0) & _expert_changed(be_ref, i))
    def _():
        ws_ref[...] = w_ref[0].astype(bf16)

    def compute(m):
        y = jnp.dot(a_ref[0:m], ws_ref[...], preferred_element_type=f32) + b_ref[0]
        nch = o_ref.shape[1]
        for r0 in range(0, m, 8):
            blk = jnp.stack([y[r0:r0 + 8, c * ROW_CH:(c + 1) * ROW_CH] for c in range(nch)], axis=0)
            o_ref[r0:r0 + 8] = pltpu.einshape("crl->rcl", blk)

    _row_cases(nv, o_ref.shape[0], compute, o_ref)


def _moe_down(blk_e, blk_nv, act, w_down, b_down3, tn=1024):
    n_rows, de = act.shape
    d = w_down.shape[2]
    nb = n_rows // MOE_TM
    nj = d // tn
    gs = pltpu.PrefetchScalarGridSpec(
        num_scalar_prefetch=2,
        grid=(nj, nb),
        in_specs=[pl.BlockSpec((MOE_TM, de), lambda j, i, be, nu: (i, 0)),
                  pl.BlockSpec((1, de, tn), lambda j, i, be, nu: (be[i], 0, j)),
                  pl.BlockSpec((1, 1, tn), lambda j, i, be, nu: (be[i], 0, j))],
        out_specs=pl.BlockSpec((MOE_TM, tn // ROW_CH, ROW_CH), lambda j, i, be, nu: (i, j, 0)),
        scratch_shapes=[pltpu.VMEM((de, tn), bf16)])
    return pl.pallas_call(
        _down_body, grid_spec=gs,
        out_shape=jax.ShapeDtypeStruct((n_rows, d // ROW_CH, ROW_CH), f32),
        compiler_params=_cparams(2),
        name="moe_down",
    )(blk_e, blk_nv, act, w_down, b_down3)


def _combine_body(cur_ref, nxt_ref, w_ref, ys_ref, x_ref, o_ref, buf_ref, tot_ref, sem):
    i = pl.program_id(0)
    nt = pl.num_programs(0)
    tm = o_ref.shape[0]
    nch = buf_ref.shape[3]
    slot = lax.rem(i, 2)
    rows = _RowGather(
        ys_ref, lambda s, a: buf_ref.at[s, a & (TOP_K - 1), lax.shift_right_logical(a, 2)],
        lambda s: [(ys_ref.at[pl.ds(0, tm)], buf_ref.at[s, k]) for k in range(TOP_K)],
        sem, tm * TOP_K)
    cur = lambda a: cur_ref[0, 0, a]

    @pl.when(i == 0)
    def _():
        rows.start(cur, 0)

    @pl.when(i + 1 < nt)
    def _():
        rows.start(lambda a: nxt_ref[0, 0, a], 1 - slot)

    rows.wait(slot)

    def weigh(t, c):
        tot = buf_ref[slot, 0, t] * w_ref[0, 0, TOP_K * t]
        for k in range(1, TOP_K):
            tot = tot + buf_ref[slot, k, t] * w_ref[0, 0, TOP_K * t + k]
        tot_ref[t] = tot
        return c

    lax.fori_loop(0, tm, weigh, 0, unroll=4)
    for r0 in range(0, tm, 8):
        blk = pltpu.einshape("rcl->crl", tot_ref[r0:r0 + 8])
        for c in range(nch):
            cs = slice(c * ROW_CH, (c + 1) * ROW_CH)
            o_ref[r0:r0 + 8, cs] = x_ref[r0:r0 + 8, cs] + blk[c]


def _combine(ys3, pos, top_w, x1, tm=128):
    n, d = x1.shape
    _, nch, chw = ys3.shape
    nt = n // tm
    pos3 = pos.reshape(nt, 1, tm * TOP_K)
    idx_spec = lambda f: pl.BlockSpec((1, 1, tm * TOP_K), f, memory_space=pltpu.SMEM)
    return pl.pallas_call(
        _combine_body,
        grid=(nt,),
        in_specs=[idx_spec(lambda i: (i, 0, 0)),
                  idx_spec(lambda i: (jnp.minimum(i + 1, nt - 1), 0, 0)),
                  idx_spec(lambda i: (i, 0, 0)),
                  pl.BlockSpec(memory_space=pl.ANY),
                  pl.BlockSpec((tm, d), lambda i: (i, 0))],
        out_specs=pl.BlockSpec((tm, d), lambda i: (i, 0)),
        out_shape=jax.ShapeDtypeStruct((n, d), f32),
        scratch_shapes=[pltpu.VMEM((2, TOP_K, tm, nch, chw), f32), pltpu.VMEM((tm, nch, chw), f32),
                        pltpu.SemaphoreType.DMA((2,))],
        compiler_params=_cparams(1),
        name="moe_combine",
    )(pos3, pos3, top_w.reshape(nt, 1, tm * TOP_K), ys3, x1)


def _route(top_idx):
    n = top_idx.shape[0]
    n_assign = n * TOP_K
    e_flat = top_idx.reshape(-1)
    a_iota = jnp.arange(n_assign, dtype=i32)
    _, order = lax.sort((e_flat, a_iota), num_keys=1, is_stable=True)
    _, rank = lax.sort((order, a_iota), num_keys=1)
    experts = jnp.arange(N_EXPERTS, dtype=i32)
    onehot = experts[:, None] == e_flat[None, :]
    counts = jnp.sum(onehot.astype(i32), axis=1)
    padded = (counts + MOE_TM - 1) // MOE_TM * MOE_TM
    start = jnp.cumsum(counts) - counts
    pend = jnp.cumsum(padded)
    shift = pend - padded - start
    pos = rank + jnp.sum(jnp.where(onehot, shift[:, None], 0), axis=0)
    n_blocks = -(-n_assign // MOE_TM) + N_EXPERTS
    n_used = pend[-1] // MOE_TM
    bi = jnp.arange(n_blocks, dtype=i32)
    raw_e = jnp.minimum(jnp.sum((pend[None, :] <= (bi * MOE_TM)[:, None]).astype(i32), axis=1),
                        N_EXPERTS - 1)
    blk_e = jnp.where(bi < n_used, raw_e, raw_e[jnp.maximum(n_used - 1, 0)]).astype(i32)
    blk_s0 = (bi * MOE_TM - shift[blk_e]).astype(i32)
    blk_end = (start[blk_e] + counts[blk_e]).astype(i32)
    blk_nv = jnp.where(bi < n_used, jnp.clip(blk_end - blk_s0, 0, MOE_TM), 0).astype(i32)
    tok_sorted = lax.shift_right_logical(order, 2)
    return (tok_sorted, blk_e, blk_s0, blk_end, blk_nv, n_used.reshape(1).astype(i32),
            pos.astype(i32))


NSA_TQ = 256


def _layer(x, ln1_g, w_in, conv_w, conv_b, dt_bias, a_log, d_skip, ssd_norm_g, q_norm_g,
           k_norm_g, cmp_pe, cmp_w1, cmp_w2, w_out, ln2_g, w_router, b_router, w_up, b_up,
           w_down, b_down):
    bsz, t_len, d = x.shape
    n = bsz * t_len
    x2 = x.reshape(n, d)

    o_dt = SSD_D_INNER + SSD_D_INNER + 2 * SSD_GN
    o_q = o_dt + SSD_HEADS
    o_kv = o_q + NSA_WIDTH
    o_g = o_kv + 6 * NSA_KV_WIDTH
    n_in = o_g + 3 * NSA_HEADS
    w_perm = jnp.concatenate(
        [w_in[:, :o_dt], w_in[:, o_q:o_g], w_in[:, o_dt:o_q], w_in[:, o_g:n_in],
         jnp.zeros((d, NP_PROJ - n_in), w_in.dtype)], axis=1).astype(bf16)

    proj = _inproj(x2, ln1_g.reshape(1, d), w_perm)

    lane_pad = lambda v: jnp.concatenate([v, jnp.zeros((LANE - v.shape[0],), f32)]).reshape(1, LANE)
    y_ssd = _ssd(proj, bsz, t_len, conv_w, conv_b.reshape(1, -1), lane_pad(dt_bias),
                 lane_pad(a_log), jnp.repeat(d_skip, SSD_HEAD_DIM).reshape(1, -1),
                 ssd_norm_g.reshape(1, -1))

    kc, vct = _compress(proj, bsz, t_len, cmp_pe, cmp_w1.astype(bf16), cmp_w2.astype(bf16),
                        k_norm_g[0:1])
    qt, ks, vst, kw, vwt = _qk_prep(proj, q_norm_g.reshape(1, DH), k_norm_g, bsz, t_len, NSA_TQ)
    gates = proj[:, COL_DT + GATE_OFF:COL_DT + GATE_OFF + 3 * NSA_HEADS]
    gates = gates.reshape(n, NSA_KV_HEADS, 3 * NSA_GQA).transpose(1, 2, 0)
    y_nsa = _nsa_attention(qt, kc, vct, ks, vst, kw, vwt, gates, bsz, t_len, NSA_TQ)

    wo = w_out.astype(bf16)
    wr = jnp.concatenate([w_router, jnp.zeros((d, LANE - N_EXPERTS), f32)], axis=1)
    wr_hi = wr.astype(bf16)
    wr = jnp.concatenate([wr_hi, (wr - wr_hi.astype(f32)).astype(bf16)], axis=1)
    br =jnp.concatenate([b_router, jnp.zeros((LANE - N_EXPERTS,), f32)]).reshape(1, LANE)
    x1 = _outproj(y_ssd, y_nsa, wo[:SSD_D_INNER], wo[SSD_D_INNER:], x2)
    h3, ti, tw = _norm_router(x1, ln2_g.reshape(1, d), wr, br)

    tok_sorted, blk_e, blk_s0, blk_end, blk_nv, n_used, pos = _route(ti[:, :TOP_K])
    xs = _gather_tokens(h3, tok_sorted, blk_s0, blk_end, n_used, MOE_TM)
    act = _moe_up(blk_e, blk_nv, xs, w_up, b_up.reshape(N_EXPERTS, 1, -1))
    ys3 = _moe_down(blk_e, blk_nv, act, w_down, b_down.reshape(N_EXPERTS, 1, -1))
    out = _combine(ys3, pos, tw[:, :TOP_K], x1)
    return out.reshape(bsz, t_len, d)


def kernel(x, ln1_g, w_in, conv_w, conv_b, dt_bias, a_log, d_skip, ssd_norm_g, q_norm_g,
           k_norm_g, cmp_pe, cmp_w1, cmp_w2, w_out, ln2_g, w_router, b_router, w_up, b_up,
           w_down, b_down):
    return _layer(x, ln1_g[0], w_in[0], conv_w[0], conv_b[0], dt_bias[0], a_log[0], d_skip[0],
                  ssd_norm_g[0], q_norm_g[0], k_norm_g[0], cmp_pe[0], cmp_w1[0], cmp_w2[0],
                  w_out[0], ln2_g[0], w_router[0], b_router[0], w_up[0], b_up[0], w_down[0],
                  b_down[0])
```

```python
import functools
import math

import jax
import jax.numpy as jnp
from jax import lax
from jax.experimental import pallas as pl
from jax.experimental.pallas import tpu as pltpu

f32 = jnp.float32
bf16 = jnp.bfloat16
i32 = jnp.int32

D_MODEL = 4096
SSD_D_INNER = 2048
SSD_HEAD_DIM = 64
SSD_HEADS = 32
SSD_GROUPS = 8
SSD_HPG = 4
SSD_STATE = 128
SSD_CONV = 4
SSD_CHUNK = 256
SSD_GN = SSD_GROUPS * SSD_STATE
SSD_GW = SSD_HPG * SSD_HEAD_DIM
NSA_HEADS = 16
NSA_KV_HEADS = 4
NSA_GQA = 4
DH = 128
NSA_WIDTH = 2048
NSA_KV_WIDTH = 512
CMP_BLOCK = 32
CMP_STRIDE = 16
CMP_HIDDEN = 256
SLC_BLOCK = 64
SLC_TOPK = 16
WINDOW = 512
N_EXPERTS = 32
TOP_K = 4
D_EXPERT = 1536
SWIGLU_LIMIT = 7.0
SWIGLU_ALPHA = 1.702
EPS = 1e-5
NEG = -1e30
LOG2E = 1.4426950408889634
SLC_GROUP = 4
VT_PAD = 16

COL_Z = 0
COL_X = 2048
COL_B = 4096
COL_C = 5120
COL_Q = 6144
COL_KV = 8192
COL_DT = 11264
GATE_OFF = 32
NP_PROJ = 11520

LANE = 128
VMEM_LIMIT = 56 * 1024 * 1024


def _cparams(n_axes):
    return pltpu.CompilerParams(dimension_semantics=("arbitrary",) * n_axes,
                                vmem_limit_bytes=VMEM_LIMIT)


def _silu(v):
    return v * (1.0 / (1.0 + jnp.exp(-v)))


def _inproj_body(x_ref, g_ref, w_ref, o_ref, h_ref):
    @pl.when(pl.program_id(1) == 0)
    def _():
        x = x_ref[...]
        ms = jnp.mean(x * x, axis=-1, keepdims=True)
        h_ref[...] = (x * lax.rsqrt(ms + EPS) * g_ref[...]).astype(bf16)

    o_ref[...] = jnp.dot(h_ref[...], w_ref[...], preferred_element_type=f32)


def _inproj(x2, g, w, tm=512, tn=768):
    n, d = x2.shape
    npj = w.shape[1]
    return pl.pallas_call(
        _inproj_body,
        grid=(n // tm, npj // tn),
        in_specs=[pl.BlockSpec((tm, d), lambda i, j: (i, 0)),
                  pl.BlockSpec((1, d), lambda i, j: (0, 0)),
                  pl.BlockSpec((d, tn), lambda i, j: (0, j))],
        out_specs=pl.BlockSpec((tm, tn), lambda i, j: (i, j)),
        out_shape=jax.ShapeDtypeStruct((n, npj), f32),
        scratch_shapes=[pltpu.VMEM((tm, d), bf16)],
        compiler_params=_cparams(2),
        name="inproj",
    )(x2, g, w)


def _lane_expand(cols, width):
    L = cols[0].shape[0]
    n = len(cols)
    lane = lax.broadcasted_iota(i32, (L, n * width), 1)
    out = jnp.broadcast_to(cols[n - 1], (L, n * width))
    for j in range(n - 2, -1, -1):
        out = jnp.where(lane < (j + 1) * width, jnp.broadcast_to(cols[j], (L, n * width)), out)
    return out


def _ssd_body(z_ref, x_ref, b_ref, c_ref, dt_ref, cw_ref, cb_ref, dtb_ref, alog_ref,
              dskip_ref, ng_ref, y_ref, buf_ref, xs_ref, bt_ref, cs_ref, h_ref):
    L = SSD_CHUNK
    c_idx = pl.program_id(1)

    @pl.when(c_idx == 0)
    def _():
        buf_ref[0:8, :] = jnp.zeros((8, buf_ref.shape[1]), f32)
        h_ref[...] = jnp.zeros(h_ref.shape, f32)

    buf_ref[8:8 + L, 0:SSD_D_INNER] = x_ref[...]
    buf_ref[8:8 + L, SSD_D_INNER:SSD_D_INNER + SSD_GN] = b_ref[...]
    buf_ref[8:8 + L, SSD_D_INNER + SSD_GN:] = c_ref[...]

    cw = 512
    n_ch = buf_ref.shape[1]
    for c0 in range(0, n_ch, cw):
        acc = jnp.broadcast_to(cb_ref[:, c0:c0 + cw], (L, cw))
        for k in range(SSD_CONV):
            acc = acc + cw_ref[k:k + 1, c0:c0 + cw] * buf_ref[5 + k:5 + k + L, c0:c0 + cw]
        v = _silu(acc)
        if c0 < SSD_D_INNER:
            xs_ref[:, c0:c0 + cw] = v
        elif c0 < SSD_D_INNER + SSD_GN:
            bt_ref[c0 - SSD_D_INNER:c0 - SSD_D_INNER + cw, :] = v.T.astype(bf16)
        else:
            o = c0 - SSD_D_INNER - SSD_GN
            cs_ref[:, o:o + cw] = v.astype(bf16)
    buf_ref[0:8, :] = buf_ref[L:L + 8, :]

    dt = dt_ref[...] + dtb_ref[...]
    dt = jnp.maximum(dt, 0.0) + jnp.log(1.0 + jnp.exp(-jnp.abs(dt)))
    da = dt * -jnp.exp(alog_ref[...])
    r = lax.broadcasted_iota(i32, (L, L), 0)
    s = lax.broadcasted_iota(i32, (L, L), 1)
    tri = (s <= r).astype(f32)
    acs = jnp.dot(tri, da, preferred_element_type=f32,
                  precision=lax.Precision.HIGHEST)
    acs_t = acs.T
    causal = s <= r
    a_last = acs[L - 1:L, :]
    e_acs = jnp.exp(acs)
    e_rem = jnp.exp(a_last - acs)
    e_last = jnp.exp(a_last)
    lane_gw = lax.broadcasted_iota(i32, (L, SSD_GW), 1)

    for g in range(SSD_GROUPS):
        hs = [g * SSD_HPG + j for j in range(SSD_HPG)]
        xg = xs_ref[:, g * SSD_GW:(g + 1) * SSD_GW]
        bgt = bt_ref[g * SSD_STATE:(g + 1) * SSD_STATE, :]
        cg = cs_ref[:, g * SSD_STATE:(g + 1) * SSD_STATE]
        dt_x = _lane_expand([dt[:, h:h + 1] for h in hs], SSD_HEAD_DIM)
        xdt = xg * dt_x
        cbm = jnp.dot(cg, bgt, preferred_element_type=f32)
        y = jnp.zeros((L, SSD_GW), f32)
        for j, h in enumerate(hs):
            seg = jnp.where(causal, acs[:, h:h + 1] - acs_t[h:h + 1, :], NEG)
            w = (cbm * jnp.exp(seg)).astype(bf16)
            band = (lane_gw >= j * SSD_HEAD_DIM) & (lane_gw < (j + 1) * SSD_HEAD_DIM)
            xm = jnp.where(band, xdt, 0.0).astype(bf16)
            y = y + jnp.dot(w, xm, preferred_element_type=f32)
        hprev = h_ref[g]
        y_off = jnp.dot(cg, hprev.astype(bf16), preferred_element_type=f32)
        y = y + y_off * _lane_expand([e_acs[:, h:h + 1] for h in hs], SSD_HEAD_DIM)
        xw = (xdt * _lane_expand([e_rem[:, h:h + 1] for h in hs], SSD_HEAD_DIM)).astype(bf16)
        st = jnp.dot(bgt, xw, preferred_element_type=f32)
        dec = _lane_expand([e_last[:, h:h + 1] for h in hs], SSD_HEAD_DIM)
        h_ref[g] = hprev * dec + st
        y = y + xg * dskip_ref[:, g * SSD_GW:(g + 1) * SSD_GW]
        y = y * _silu(z_ref[:, g * SSD_GW:(g + 1) * SSD_GW])
        y = y * lax.rsqrt(jnp.mean(y * y, axis=-1, keepdims=True) + EPS)
        y_ref[:, g * SSD_GW:(g + 1) * SSD_GW] = (y * ng_ref[:, g * SSD_GW:(g + 1) * SSD_GW]).astype(bf16)


def _ssd(proj, bsz, t_len, conv_w, conv_b, dt_bias, a_log, dskip_x, norm_g):
    L = SSD_CHUNK
    nc = t_len // L
    n = bsz * t_len
    nch = SSD_D_INNER + 2 * SSD_GN
    row = lambda b, c: b * nc + c
    full = lambda shape: pl.BlockSpec(shape, lambda b, c: (0,) * len(shape))
    return pl.pallas_call(
        _ssd_body,
        grid=(bsz, nc),
        in_specs=[pl.BlockSpec((L, SSD_D_INNER), lambda b, c: (row(b, c), COL_Z // SSD_D_INNER)),
                  pl.BlockSpec((L, SSD_D_INNER), lambda b, c: (row(b, c), COL_X // SSD_D_INNER)),
                  pl.BlockSpec((L, SSD_GN), lambda b, c: (row(b, c), COL_B // SSD_GN)),
                  pl.BlockSpec((L, SSD_GN), lambda b, c: (row(b, c), COL_C // SSD_GN)),
                  pl.BlockSpec((L, LANE), lambda b, c: (row(b, c), COL_DT // LANE)),
                  full((SSD_CONV, nch)), full((1, nch)), full((1, LANE)),
                  full((1, LANE)), full((1, SSD_D_INNER)), full((1, SSD_D_INNER))],
        out_specs=pl.BlockSpec((L, SSD_D_INNER), lambda b, c: (row(b, c), 0)),
        out_shape=jax.ShapeDtypeStruct((n, SSD_D_INNER), bf16),
        scratch_shapes=[pltpu.VMEM((L + 8, nch), f32),
                        pltpu.VMEM((L, SSD_D_INNER), f32),
                        pltpu.VMEM((SSD_GN, L), bf16),
                        pltpu.VMEM((L, SSD_GN), bf16),
                        pltpu.VMEM((SSD_GROUPS, SSD_STATE, SSD_GW), f32)],
        compiler_params=_cparams(2),
        name="ssd",
    )(proj, proj, proj, proj, proj, conv_w, conv_b, dt_bias, a_log, dskip_x, norm_g)


def _cmp_body(uk_ref, uv_ref, pe_ref, w1_ref, w2_ref, g_ref, kc_ref, vct_ref):
    nchunk = uk_ref.shape[0]

    def branch(u_ref, kv):
        hid_a = jnp.zeros((nchunk, CMP_HIDDEN), f32)
        hid_b = jnp.zeros((nchunk, CMP_HIDDEN), f32)
        ut = pltpu.einshape("csd->scd", u_ref[...])
        for s in range(CMP_STRIDE):
            us = ut[s]
            hid_a = hid_a + jnp.dot((us + pe_ref[kv, s:s + 1, :]).astype(bf16), w1_ref[kv, s],
                                    preferred_element_type=f32)
            hid_b = hid_b + jnp.dot(
                (us + pe_ref[kv, CMP_STRIDE + s:CMP_STRIDE + s + 1, :]).astype(bf16),
                w1_ref[kv, CMP_STRIDE + s], preferred_element_type=f32)
        hid = hid_a + pltpu.roll(hid_b, nchunk - 1, 0)
        out = jnp.dot(_silu(hid).astype(bf16), w2_ref[kv], preferred_element_type=f32)
        rowi = lax.broadcasted_iota(i32, out.shape, 0)
        return jnp.where(rowi == nchunk - 1, 0.0, out)

    k = branch(uk_ref, 0)
    k = k * lax.rsqrt(jnp.mean(k * k, axis=-1, keepdims=True) + EPS) * g_ref[...]
    kc_ref[0, 0] = k.astype(bf16)
    vct_ref[0, 0] = branch(uv_ref, 1).T.astype(bf16)


def _compress(proj, bsz, t_len, pe, w1, w2, kg):
    nchunk = t_len // CMP_STRIDE
    H = NSA_KV_HEADS
    u3 = proj.reshape(bsz * nchunk, CMP_STRIDE, proj.shape[1])
    useg = lambda kv: pl.BlockSpec((nchunk, CMP_STRIDE, DH),
                                   lambda b, h: (b, 0, COL_KV // DH + kv * H + h))
    full = lambda shape: pl.BlockSpec(shape, lambda b, h: (0,) * len(shape))
    return pl.pallas_call(
        _cmp_body,
        grid=(bsz, H),
        in_specs=[useg(0), useg(1), full((2, CMP_BLOCK, DH)), full((2, CMP_BLOCK, DH, CMP_HIDDEN)),
                  full((2, CMP_HIDDEN, DH)), full((1, DH))],
        out_specs=[pl.BlockSpec((1, 1, nchunk, DH), lambda b, h: (b, h, 0, 0)),
                   pl.BlockSpec((1, 1, DH, nchunk), lambda b, h: (b, h, 0, 0))],
        out_shape=[jax.ShapeDtypeStruct((bsz, H, nchunk, DH), bf16),
                   jax.ShapeDtypeStruct((bsz, H, DH, nchunk), bf16)],
        compiler_params=_cparams(2),
        name="nsa_compress",
    )(u3, u3, pe, w1, w2, kg)


def _qk_prep_body(q_ref, s_ref, w_ref, qg_ref, kg_ref, qt_ref, ks_ref, vst_ref, kw_ref, vwt_ref):
    scale = DH ** -0.5 * LOG2E
    tq = q_ref.shape[0]

    def hnorm(v, g):
        return v * lax.rsqrt(jnp.mean(v * v, axis=-1, keepdims=True) + EPS) * g

    ones_rows = (lax.broadcasted_iota(i32, (VT_PAD, tq), 0) == 0).astype(bf16)
    for h in range(NSA_KV_HEADS):
        for g in range(NSA_GQA):
            sl = slice((h * NSA_GQA + g) * DH, (h * NSA_GQA + g + 1) * DH)
            qn = hnorm(q_ref[:, sl], qg_ref[...]) * scale
            qt_ref[0, h, 0, :, g * tq:(g + 1) * tq] = qn.T.astype(bf16)
        sl = slice(h * DH, (h + 1) * DH)
        sv = slice(NSA_KV_WIDTH + h * DH, NSA_KV_WIDTH + (h + 1) * DH)
        ks_ref[:, sl] = hnorm(s_ref[:, sl], kg_ref[1:2, :]).astype(bf16)
        kw_ref[:, sl] = hnorm(w_ref[:, sl], kg_ref[2:3, :]).astype(bf16)
        vst_ref[0, h, 0, 0:DH] = s_ref[:, sv].T.astype(bf16)
        vwt_ref[0, h, 0, 0:DH] = w_ref[:, sv].T.astype(bf16)
        vst_ref[0, h, 0, DH:DH + VT_PAD] = ones_rows
        vwt_ref[0, h, 0, DH:DH + VT_PAD] = ones_rows


def _qk_prep(proj, qg, kg, bsz, t_len, tq):
    n = proj.shape[0]
    nq = t_len // tq
    H = NSA_KV_HEADS
    w2 = 2 * NSA_KV_WIDTH
    row = lambda b, i: b * nq + i
    vt_spec = pl.BlockSpec((1, H, 1, DH + VT_PAD, tq), lambda b, i: (b, 0, i, 0, 0))
    vt_shape = jax.ShapeDtypeStruct((bsz, H, nq, DH + VT_PAD, tq), bf16)
    return pl.pallas_call(
        _qk_prep_body,
        grid=(bsz, nq),
        in_specs=[pl.BlockSpec((tq, NSA_WIDTH), lambda b, i: (row(b, i), COL_Q // NSA_WIDTH)),
                  pl.BlockSpec((tq, w2), lambda b, i: (row(b, i), (COL_KV + w2) // w2)),
                  pl.BlockSpec((tq, w2), lambda b, i: (row(b, i), (COL_KV + 2 * w2) // w2)),
                  pl.BlockSpec((1, DH), lambda b, i: (0, 0)),
                  pl.BlockSpec((3, DH), lambda b, i: (0, 0))],
        out_specs=[pl.BlockSpec((1, H, 1, DH, NSA_GQA * tq), lambda b, i: (b, 0, i, 0, 0)),
                   pl.BlockSpec((tq, NSA_KV_WIDTH), lambda b, i: (row(b, i), 0)),
                   vt_spec,
                   pl.BlockSpec((tq, NSA_KV_WIDTH), lambda b, i: (row(b, i), 0)),
                   vt_spec],
        out_shape=[jax.ShapeDtypeStruct((bsz, H, nq, DH, NSA_GQA * tq), bf16),
                   jax.ShapeDtypeStruct((n, NSA_KV_WIDTH), bf16), vt_shape,
                   jax.ShapeDtypeStruct((n, NSA_KV_WIDTH), bf16), vt_shape],
        compiler_params=_cparams(2),
        name="nsa_qk_prep",
    )(proj, proj, proj, qg, kg)


def _nsa_body(qt_ref, kc_ref, vct_ref, ks_ref, vst_ref, kw_ref, vwt_ref, gt_ref, o_ref,
              m_ref, acc_ref, sel_ref, *, tq):
    i = pl.program_id(2)
    G = NSA_GQA
    n_cmp = kc_ref.shape[2]
    n_slc = ks_ref.shape[0] // SLC_BLOCK

    qt = qt_ref[0, 0, 0]
    t_row = i * tq + lax.broadcasted_iota(i32, (1, tq), 1)
    t_all = jnp.concatenate([t_row] * G, axis=1)

    s_c = jnp.dot(kc_ref[0, 0], qt, preferred_element_type=f32)
    cend = lax.broadcasted_iota(i32, (n_cmp, 1), 0) * CMP_STRIDE + (CMP_BLOCK - 1)
    m_c = cend <= t_all
    s_c = jnp.where(m_c, s_c, NEG)
    p_c = jnp.where(m_c, jnp.exp2(s_c - jnp.max(s_c, axis=0, keepdims=True)), 0.0)
    p_c = p_c / jnp.maximum(jnp.sum(p_c, axis=0, keepdims=True), 1e-30)
    o_c = jnp.dot(vct_ref[0, 0], p_c.astype(bf16), preferred_element_type=f32)

    imp = p_c[:, 0:tq]
    for g in range(1, G):
        imp = imp + p_c[:, g * tq:(g + 1) * tq]
    per = SLC_BLOCK // CMP_STRIDE
    ni = lax.broadcasted_iota(i32, (n_slc, n_cmp), 0)
    ci = lax.broadcasted_iota(i32, (n_slc, n_cmp), 1)
    fold = ((ci // per == ni).astype(f32) + ((ci + 1) // per == ni).astype(f32))
    imp_b = jnp.dot(fold, imp, preferred_element_type=f32, precision=lax.Precision.HIGHEST)
    jb = lax.broadcasted_iota(i32, (n_slc, tq), 0)
    cur = t_row // SLC_BLOCK
    forced = (jb == 0) | (jb == cur) | (jb == cur - 1)
    val = jnp.where(forced, 1e30, jnp.where(jb <= cur, imp_b, -1.0))
    sel = jnp.zeros((n_slc, tq), f32)
    for _ in range(SLC_TOPK):
        mx = jnp.max(val, axis=0, keepdims=True)
        first = jnp.min(jnp.where(val == mx, jb, n_slc), axis=0, keepdims=True)
        hit = (jb == first) & (mx >= 0.0)
        sel = jnp.where(hit, 1.0, sel)
        val = jnp.where(jb == first, -1.0, val)

    def online(tiles, slot):
        scores = [jnp.dot(k, qt, preferred_element_type=f32) + jnp.concatenate([bias] * G, axis=1)
                  for k, _, bias in tiles]
        m_old = m_ref[slot]
        m_new = m_old
        for s_ in scores:
            m_new = jnp.maximum(m_new, jnp.max(s_, axis=0, keepdims=True))
        acc = jnp.exp2(m_old - m_new) * acc_ref[slot]
        for s_, (_, vt, _) in zip(scores, tiles):
            acc = acc + jnp.dot(vt, jnp.exp2(s_ - m_new).astype(bf16), preferred_element_type=f32)
        acc_ref[slot] = acc
        m_ref[slot] = m_new

    m_ref[...] = jnp.full(m_ref.shape, NEG, f32)
    acc_ref[...] = jnp.zeros(acc_ref.shape, f32)
    krow = lax.broadcasted_iota(i32, (tq, 1), 0)
    kcol = lax.broadcasted_iota(i32, (1, tq), 1)

    sel_ref[...] = jnp.where(sel > 0.5, 0.0, NEG)
    bpt = tq // SLC_BLOCK

    def block_bias(kt):
        rows = [jnp.broadcast_to(sel_ref[pl.ds(kt * bpt + j, 1), :], (SLC_BLOCK, tq))
                for j in range(bpt)]
        return jnp.concatenate(rows, axis=0)

    def slc_tile(kt, causal=False):
        bias = block_bias(kt)
        if causal:
            bias = jnp.where(krow <= kcol, bias, NEG)
        return ks_ref[pl.ds(pl.multiple_of(kt * tq, tq), tq), :], vst_ref[0, 0, kt], bias

    def slc_group(kg, carry):
        online([slc_tile(SLC_GROUP * kg + j) for j in range(SLC_GROUP)], 0)
        return carry

    n_grp = i // SLC_GROUP
    lax.fori_loop(0, n_grp, slc_group, 0)
    for r in range(SLC_GROUP):
        @pl.when(i - n_grp * SLC_GROUP == r)
        def _():
            online([slc_tile(n_grp * SLC_GROUP + j) for j in range(r)] + [slc_tile(i, causal=True)], 0)

    def win_tile(kt):
        k0 = pl.multiple_of(kt * tq, tq)
        key = k0 + krow
        ok = (key <= t_row) & (key > t_row - WINDOW)
        return kw_ref[pl.ds(k0, tq), :], vwt_ref[0, 0, kt], jnp.where(ok, 0.0, NEG)

    back = WINDOW // tq
    for c in range(back + 1):
        @pl.when(jnp.minimum(i, back) == c)
        def _():
            online([win_tile(i - c + j) for j in range(c + 1)], 1)

    o_s = acc_ref[0, 0:DH] / acc_ref[0, DH:DH + 1]
    o_w = acc_ref[1, 0:DH] / acc_ref[1, DH:DH + 1]
    gts = 1.0 / (1.0 + jnp.exp(-gt_ref[0]))
    for g in range(G):
        cs = slice(g * tq, (g + 1) * tq)
        out_t = (gts[3 * g:3 * g + 1, :] * o_c[:, cs] + gts[3 * g + 1:3 * g + 2, :] * o_s[:, cs]
                 + gts[3 * g + 2:3 * g + 3, :] * o_w[:, cs])
        o_ref[:, g * DH:(g + 1) * DH] = out_t.T.astype(bf16)


def _nsa_attention(qt, kc, vct, ks, vst, kw, vwt, gates, bsz, t_len, tq):
    n = bsz * t_len
    nq = t_len // tq
    nchunk = t_len // CMP_STRIDE
    R = NSA_GQA * tq
    seq = pl.BlockSpec((t_len, DH), lambda b, h, i: (b, h))
    seq_t = pl.BlockSpec((1, 1, nq, DH + VT_PAD, tq), lambda b, h, i: (b, h, 0, 0, 0))
    return pl.pallas_call(
        functools.partial(_nsa_body, tq=tq),
        grid=(bsz, NSA_KV_HEADS, nq),
        in_specs=[pl.BlockSpec((1, 1, 1, DH, R), lambda b, h, i: (b, h, i, 0, 0)),
                  pl.BlockSpec((1, 1, nchunk, DH), lambda b, h, i: (b, h, 0, 0)),
                  pl.BlockSpec((1, 1, DH, nchunk), lambda b, h, i: (b, h, 0, 0)),
                  seq, seq_t, seq, seq_t,
                  pl.BlockSpec((1, 3 * NSA_GQA, tq), lambda b, h, i: (h, 0, b * nq + i))],
        out_specs=pl.BlockSpec((tq, NSA_GQA * DH), lambda b, h, i: (b * nq + i, h)),
        out_shape=jax.ShapeDtypeStruct((n, NSA_WIDTH), bf16),
        scratch_shapes=[pltpu.VMEM((2, 1, R), f32), pltpu.VMEM((2, DH + VT_PAD, R), f32),
                        pltpu.VMEM((t_len // SLC_BLOCK, tq), f32)],
        compiler_params=_cparams(3),
        name="nsa_attention",
    )(qt, kc, vct, ks, vst, kw, vwt, gates)


ROW_CH = LANE


def _outproj_body(ya_ref, yb_ref, wa_ref, wb_ref, x_ref, x1_ref):
    acc = x_ref[...] + jnp.dot(ya_ref[...], wa_ref[...], preferred_element_type=f32)
    x1_ref[...] = acc + jnp.dot(yb_ref[...], wb_ref[...], preferred_element_type=f32)


def _outproj(ya, yb, wa, wb, x2, tm=512, tn=1024):
    n, d = x2.shape
    ka = ya.shape[1]
    kb = yb.shape[1]
    return pl.pallas_call(
        _outproj_body,
        grid=(n // tm, d // tn),
        in_specs=[pl.BlockSpec((tm, ka), lambda i, j: (i, 0)),
                  pl.BlockSpec((tm, kb), lambda i, j: (i, 0)),
                  pl.BlockSpec((ka, tn), lambda i, j: (0, j)),
                  pl.BlockSpec((kb, tn), lambda i, j: (0, j)),
                  pl.BlockSpec((tm, tn), lambda i, j: (i, j))],
        out_specs=pl.BlockSpec((tm, tn), lambda i, j: (i, j)),
        out_shape=jax.ShapeDtypeStruct((n, d), f32),
        compiler_params=_cparams(2),
        name="outproj",
    )(ya, yb, wa, wb, x2)


def _router_body(x_ref, g_ref, wr_ref, br_ref, h_ref, ti_ref, tw_ref):
    tm, d = x_ref.shape
    x = x_ref[...]
    hh = x * lax.rsqrt(jnp.mean(x * x, axis=-1, keepdims=True) + EPS) * g_ref[...]
    hi = hh.astype(bf16)
    lo = (hh - hi.astype(f32)).astype(bf16)
    t = jnp.dot(hi, wr_ref[...], preferred_element_type=f32)
    logits = br_ref[...] + (t[:, :LANE] + (t[:, LANE:] + jnp.dot(lo, wr_ref[:, 0:LANE],
                                                                  preferred_element_type=f32)))
    for r0 in range(0, tm, 8):
        blk = jnp.stack([hh[r0:r0 + 8, c * ROW_CH:(c + 1) * ROW_CH] for c in range(d // ROW_CH)], axis=0)
        h_ref[r0:r0 + 8] = pltpu.einshape("crl->rcl", blk)
    lane = lax.broadcasted_iota(i32, (tm, LANE), 1)
    val = jnp.where(lane < N_EXPERTS, logits, -jnp.inf)
    idxs = jnp.zeros((tm, LANE), i32)
    vals = jnp.full((tm, LANE), -jnp.inf, f32)
    for k in range(TOP_K):
        mx = jnp.max(val, axis=-1, keepdims=True)
        first = jnp.min(jnp.where(val == mx, lane, LANE), axis=-1, keepdims=True)
        idxs = jnp.where(lane == k, first, idxs)
        vals = jnp.where(lane == k, mx, vals)
        val = jnp.where(lane == first, -jnp.inf, val)
    e = jnp.exp(vals - jnp.max(vals, axis=-1, keepdims=True))
    tw_ref[...] = e / jnp.sum(e, axis=-1, keepdims=True)
    ti_ref[...] = idxs


def _norm_router(x1, g2, wr, br, tm=256):
    n, d = x1.shape
    return pl.pallas_call(
        _router_body,
        grid=(n // tm,),
        in_specs=[pl.BlockSpec((tm, d), lambda i: (i, 0)),
                  pl.BlockSpec((1, d), lambda i: (0, 0)),
                  pl.BlockSpec((d, 2 * LANE), lambda i: (0, 0)),
                  pl.BlockSpec((1, LANE), lambda i: (0, 0))],
        out_specs=[pl.BlockSpec((tm, d // ROW_CH, ROW_CH), lambda i: (i, 0, 0)),
                   pl.BlockSpec((tm, LANE), lambda i: (i, 0)),
                   pl.BlockSpec((tm, LANE), lambda i: (i, 0))],
        out_shape=[jax.ShapeDtypeStruct((n, d // ROW_CH, ROW_CH), f32),
                   jax.ShapeDtypeStruct((n, LANE), i32),
                   jax.ShapeDtypeStruct((n, LANE), f32)],
        compiler_params=_cparams(1),
        name="norm_router",
    )(x1, g2, wr, br)


MOE_TM = 512
DMA_UNROLL = 8
DMA_THREADS = 2


class _RowGather:
    def __init__(self, src_ref, dst_of, bulk_of, sem, count, chunks=1):
        self.src_ref, self.dst_of, self.bulk_of, self.sem = src_ref, dst_of, bulk_of, sem
        self.per = count // chunks

    def start(self, idx_of, slot, c=0):
        def body(t, carry):
            for p in range(DMA_THREADS):
                r = c * self.per + DMA_THREADS * t + p
                pltpu.make_async_copy(self.src_ref.at[idx_of(r)], self.dst_of(slot, r),
                                      self.sem.at[slot, c]).start(priority=p)
            return carry
        lax.fori_loop(0, self.per // DMA_THREADS, body, 0, unroll=DMA_UNROLL // DMA_THREADS)

    def wait(self, slot, c=0):
        for src, dst in self.bulk_of(slot, c):
            pltpu.make_async_copy(src, dst, self.sem.at[slot, c]).wait()


GATHER_CHUNKS = 8


def _gather_body(tok_ref, s0_ref, end_ref, nu_ref, src_ref, o_ref, buf_ref, sem):
    i = pl.program_id(0)
    tmr = o_ref.shape[0]
    nch = buf_ref.shape[2]
    per = tmr // GATHER_CHUNKS
    slot = lax.rem(i, 2)
    n_live = jnp.minimum(nu_ref[0], pl.num_programs(0))

    def tokens_of(tile):
        s0 = s0_ref[tile]
        last = end_ref[tile] - 1
        return lambda r: tok_ref[jnp.minimum(s0 + r, last)]

    rows = _RowGather(
        src_ref, lambda s, r: buf_ref.at[s, r],
        lambda s, c: [(src_ref.at[pl.ds(0, per)], buf_ref.at[s, pl.ds(c * per, per)])],
        sem, tmr, GATHER_CHUNKS)

    @pl.when((i == 0) & (n_live > 0))
    def _():
        for c in range(GATHER_CHUNKS):
            rows.start(tokens_of(0), 0, c)

    @pl.when(i < n_live)
    def _():
        for c in range(GATHER_CHUNKS):
            @pl.when(i + 1 < n_live)
            def _():
                rows.start(tokens_of(i + 1), 1 - slot, c)

            rows.wait(slot, c)
            for r0 in range(c * per, (c + 1) * per, 8):
                blk = pltpu.einshape("rcl->crl", buf_ref[slot, r0:r0 + 8])
                for k in range(nch):
                    o_ref[r0:r0 + 8, k * ROW_CH:(k + 1) * ROW_CH] = blk[k].astype(o_ref.dtype)

    @pl.when(i >= n_live)
    def _():
        o_ref[...] = jnp.zeros(o_ref.shape, o_ref.dtype)


def _gather_tokens(src3, tok_sorted, blk_s0, blk_end, n_used, tmr):
    _, nch, chw = src3.shape
    nt = blk_s0.shape[0]
    gs = pltpu.PrefetchScalarGridSpec(
        num_scalar_prefetch=4,
        grid=(nt,),
        in_specs=[pl.BlockSpec(memory_space=pl.ANY)],
        out_specs=pl.BlockSpec((tmr, nch * chw), lambda i, *_: (i, 0)),
        scratch_shapes=[pltpu.VMEM((2, tmr, nch, chw), src3.dtype),
                        pltpu.SemaphoreType.DMA((2, GATHER_CHUNKS))])
    return pl.pallas_call(
        _gather_body, grid_spec=gs,
        out_shape=jax.ShapeDtypeStruct((nt * tmr, nch * chw), bf16),
        compiler_params=_cparams(1),
        name="moe_gather",
    )(tok_sorted, blk_s0, blk_end, n_used, src3)


def _expert_changed(be_ref, i):
    return (i == 0) | (be_ref[i] != be_ref[jnp.maximum(i - 1, 0)])


def _row_cases(nv, tm, compute, o_ref):
    half = tm // 2

    @pl.when(nv > half)
    def _():
        compute(tm)

    @pl.when((nv > 0) & (nv <= half))
    def _():
        compute(half)
        o_ref[half:] = jnp.zeros((tm - half,) + o_ref.shape[1:], o_ref.dtype)

    @pl.when(nv <= 0)
    def _():
        o_ref[...] = jnp.zeros(o_ref.shape, o_ref.dtype)


def _up_body(be_ref, nv_ref, x_ref, wg_ref, wl_ref, bg_ref, bl_ref, o_ref, ws_ref):
    i = pl.program_id(1)
    tn = o_ref.shape[1]
    nv = nv_ref[i]

    @pl.when((nv > 0) & _expert_changed(be_ref, i))
    def _():
        ws_ref[:, 0:tn] = wg_ref[0].astype(bf16)
        ws_ref[:, tn:2 * tn] = wl_ref[0].astype(bf16)

    def compute(m):
        u = jnp.dot(x_ref[0:m], ws_ref[...], preferred_element_type=f32)
        glu = u[:, 0:tn] + bg_ref[0]
        lin = u[:, tn:2 * tn] + bl_ref[0]
        glu = jnp.minimum(glu, SWIGLU_LIMIT)
        lin = jnp.clip(lin, -SWIGLU_LIMIT, SWIGLU_LIMIT)
        act = glu * (1.0 / (1.0 + jnp.exp(-SWIGLU_ALPHA * glu))) * (lin + 1.0)
        o_ref[0:m] = act.astype(bf16)

    _row_cases(nv, o_ref.shape[0], compute, o_ref)


def _moe_up(blk_e, blk_nv, xs, w_up, b_up3, tn=384):
    n_rows, d = xs.shape
    nb = n_rows // MOE_TM
    nj = D_EXPERT // tn
    gs = pltpu.PrefetchScalarGridSpec(
        num_scalar_prefetch=2,
        grid=(nj, nb),
        in_specs=[pl.BlockSpec((MOE_TM, d), lambda j, i, be, nu: (i, 0)),
                  pl.BlockSpec((1, d, tn), lambda j, i, be, nu: (be[i], 0, j)),
                  pl.BlockSpec((1, d, tn), lambda j, i, be, nu: (be[i], 0, nj + j)),
                  pl.BlockSpec((1, 1, tn), lambda j, i, be, nu: (be[i], 0, j)),
                  pl.BlockSpec((1, 1, tn), lambda j, i, be, nu: (be[i], 0, nj + j))],
        out_specs=pl.BlockSpec((MOE_TM, tn), lambda j, i, be, nu: (i, j)),
        scratch_shapes=[pltpu.VMEM((d, 2 * tn), bf16)])
    return pl.pallas_call(
        _up_body, grid_spec=gs,
        out_shape=jax.ShapeDtypeStruct((n_rows, D_EXPERT), bf16),
        compiler_params=_cparams(2),
        name="moe_up",
    )(blk_e, blk_nv, xs, w_up, w_up, b_up3, b_up3)


def _down_body(be_ref, nv_ref, a_ref, w_ref, b_ref, o_ref, ws_ref):
    i = pl.program_id(1)
    nv = nv_ref[i]

    @pl.when((nv > 0) & _expert_changed(be_ref, i))
    def _():
        ws_ref[...] = w_ref[0].astype(bf16)

    def compute(m):
        y = jnp.dot(a_ref[0:m], ws_ref[...], preferred_element_type=f32) + b_ref[0]
        nch = o_ref.shape[1]
        for r0 in range(0, m, 8):
            blk = jnp.stack([y[r0:r0 + 8, c * ROW_CH:(c + 1) * ROW_CH] for c in range(nch)], axis=0)
            o_ref[r0:r0 + 8] = pltpu.einshape("crl->rcl", blk)

    _row_cases(nv, o_ref.shape[0], compute, o_ref)


def _moe_down(blk_e, blk_nv, act, w_down, b_down3, tn=1024):
    n_rows, de = act.shape
    d = w_down.shape[2]
    nb = n_rows // MOE_TM
    nj = d // tn
    gs = pltpu.PrefetchScalarGridSpec(
        num_scalar_prefetch=2,
        grid=(nj, nb),
        in_specs=[pl.BlockSpec((MOE_TM, de), lambda j, i, be, nu: (i, 0)),
                  pl.BlockSpec((1, de, tn), lambda j, i, be, nu: (be[i], 0, j)),
                  pl.BlockSpec((1, 1, tn), lambda j, i, be, nu: (be[i], 0, j))],
        out_specs=pl.BlockSpec((MOE_TM, tn // ROW_CH, ROW_CH), lambda j, i, be, nu: (i, j, 0)),
        scratch_shapes=[pltpu.VMEM((de, tn), bf16)])
    return pl.pallas_call(
        _down_body, grid_spec=gs,
        out_shape=jax.ShapeDtypeStruct((n_rows, d // ROW_CH, ROW_CH), f32),
        compiler_params=_cparams(2),
        name="moe_down",
    )(blk_e, blk_nv, act, w_down, b_down3)


def _combine_body(cur_ref, nxt_ref, w_ref, ys_ref, x_ref, o_ref, buf_ref, tot_ref, sem):
    i = pl.program_id(0)
    nt = pl.num_programs(0)
    tm = o_ref.shape[0]
    nch = buf_ref.shape[3]
    slot = lax.rem(i, 2)
    rows = _RowGather(
        ys_ref, lambda s, a: buf_ref.at[s, a & (TOP_K - 1), lax.shift_right_logical(a, 2)],
        lambda s, c: [(ys_ref.at[pl.ds(0, tm)], buf_ref.at[s, k]) for k in range(TOP_K)],
        sem, tm * TOP_K)
    cur = lambda a: cur_ref[0, 0, a]

    @pl.when(i == 0)
    def _():
        rows.start(cur, 0)

    @pl.when(i + 1 < nt)
    def _():
        rows.start(lambda a: nxt_ref[0, 0, a], 1 - slot)

    rows.wait(slot)

    def weigh(t, c):
        tot = buf_ref[slot, 0, t] * w_ref[0, 0, TOP_K * t]
        for k in range(1, TOP_K):
            tot = tot + buf_ref[slot, k, t] * w_ref[0, 0, TOP_K * t + k]
        tot_ref[t] = tot
        return c

    lax.fori_loop(0, tm, weigh, 0, unroll=4)
    for r0 in range(0, tm, 8):
        blk = pltpu.einshape("rcl->crl", tot_ref[r0:r0 + 8])
        for c in range(nch):
            cs = slice(c * ROW_CH, (c + 1) * ROW_CH)
            o_ref[r0:r0 + 8, cs] = x_ref[r0:r0 + 8, cs] + blk[c]


def _combine(ys3, pos, top_w, x1, tm=128):
    n, d = x1.shape
    _, nch, chw = ys3.shape
    nt = n // tm
    pos3 = pos.reshape(nt, 1, tm * TOP_K)
    idx_spec = lambda f: pl.BlockSpec((1, 1, tm * TOP_K), f, memory_space=pltpu.SMEM)
    return pl.pallas_call(
        _combine_body,
        grid=(nt,),
        in_specs=[idx_spec(lambda i: (i, 0, 0)),
                  idx_spec(lambda i: (jnp.minimum(i + 1, nt - 1), 0, 0)),
                  idx_spec(lambda i: (i, 0, 0)),
                  pl.BlockSpec(memory_space=pl.ANY),
                  pl.BlockSpec((tm, d), lambda i: (i, 0))],
        out_specs=pl.BlockSpec((tm, d), lambda i: (i, 0)),
        out_shape=jax.ShapeDtypeStruct((n, d), f32),
        scratch_shapes=[pltpu.VMEM((2, TOP_K, tm, nch, chw), f32), pltpu.VMEM((tm, nch, chw), f32),
                        pltpu.SemaphoreType.DMA((2, 1))],
        compiler_params=_cparams(1),
        name="moe_combine",
    )(pos3, pos3, top_w.reshape(nt, 1, tm * TOP_K), ys3, x1)


def _route(top_idx):
    n = top_idx.shape[0]
    n_assign = n * TOP_K
    e_flat = top_idx.reshape(-1)
    a_iota = jnp.arange(n_assign, dtype=i32)
    _, order = lax.sort((e_flat, a_iota), num_keys=1, is_stable=True)
    _, rank = lax.sort((order, a_iota), num_keys=1)
    experts = jnp.arange(N_EXPERTS, dtype=i32)
    onehot = experts[:, None] == e_flat[None, :]
    counts = jnp.sum(onehot.astype(i32), axis=1)
    padded = (counts + MOE_TM - 1) // MOE_TM * MOE_TM
    start = jnp.cumsum(counts) - counts
    pend = jnp.cumsum(padded)
    shift = pend - padded - start
    pos = rank + jnp.sum(jnp.where(onehot, shift[:, None], 0), axis=0)
    n_blocks = -(-n_assign // MOE_TM) + N_EXPERTS
    n_used = pend[-1] // MOE_TM
    bi = jnp.arange(n_blocks, dtype=i32)
    raw_e = jnp.minimum(jnp.sum((pend[None, :] <= (bi * MOE_TM)[:, None]).astype(i32), axis=1),
                        N_EXPERTS - 1)
    blk_e = jnp.where(bi < n_used, raw_e, raw_e[jnp.maximum(n_used - 1, 0)]).astype(i32)
    blk_s0 = (bi * MOE_TM - shift[blk_e]).astype(i32)
    blk_end = (start[blk_e] + counts[blk_e]).astype(i32)
    blk_nv = jnp.where(bi < n_used, jnp.clip(blk_end - blk_s0, 0, MOE_TM), 0).astype(i32)
    tok_sorted = lax.shift_right_logical(order, 2)
    return (tok_sorted, blk_e, blk_s0, blk_end, blk_nv, n_used.reshape(1).astype(i32),
            pos.astype(i32))


NSA_TQ = 256


def _layer(x, ln1_g, w_in, conv_w, conv_b, dt_bias, a_log, d_skip, ssd_norm_g, q_norm_g,
           k_norm_g, cmp_pe, cmp_w1, cmp_w2, w_out, ln2_g, w_router, b_router, w_up, b_up,
           w_down, b_down):
    bsz, t_len, d = x.shape
    n = bsz * t_len
    x2 = x.reshape(n, d)

    o_dt = SSD_D_INNER + SSD_D_INNER + 2 * SSD_GN
    o_q = o_dt + SSD_HEADS
    o_kv = o_q + NSA_WIDTH
    o_g = o_kv + 6 * NSA_KV_WIDTH
    n_in = o_g + 3 * NSA_HEADS
    w_perm = jnp.concatenate(
        [w_in[:, :o_dt], w_in[:, o_q:o_g], w_in[:, o_dt:o_q], w_in[:, o_g:n_in],
         jnp.zeros((d, NP_PROJ - n_in), w_in.dtype)], axis=1).astype(bf16)

    proj = _inproj(x2, ln1_g.reshape(1, d), w_perm)

    lane_pad = lambda v: jnp.concatenate([v, jnp.zeros((LANE - v.shape[0],), f32)]).reshape(1, LANE)
    y_ssd = _ssd(proj, bsz, t_len, conv_w, conv_b.reshape(1, -1), lane_pad(dt_bias),
                 lane_pad(a_log), jnp.repeat(d_skip, SSD_HEAD_DIM).reshape(1, -1),
                 ssd_norm_g.reshape(1, -1))

    kc, vct = _compress(proj, bsz, t_len, cmp_pe, cmp_w1.astype(bf16), cmp_w2.astype(bf16),
                        k_norm_g[0:1])
    qt, ks, vst, kw, vwt = _qk_prep(proj, q_norm_g.reshape(1, DH), k_norm_g, bsz, t_len, NSA_TQ)
    gates = proj[:, COL_DT + GATE_OFF:COL_DT + GATE_OFF + 3 * NSA_HEADS]
    gates = gates.reshape(n, NSA_KV_HEADS, 3 * NSA_GQA).transpose(1, 2, 0)
    y_nsa = _nsa_attention(qt, kc, vct, ks, vst, kw, vwt, gates, bsz, t_len, NSA_TQ)

    wo = w_out.astype(bf16)
    wr = jnp.concatenate([w_router, jnp.zeros((d, LANE - N_EXPERTS), f32)], axis=1)
    wr_hi = wr.astype(bf16)
    wr = jnp.concatenate([wr_hi, (wr - wr_hi.astype(f32)).astype(bf16)], axis=1)
    br =jnp.concatenate([b_router, jnp.zeros((LANE - N_EXPERTS,), f32)]).reshape(1, LANE)
    x1 = _outproj(y_ssd, y_nsa, wo[:SSD_D_INNER], wo[SSD_D_INNER:], x2)
    h3, ti, tw = _norm_router(x1, ln2_g.reshape(1, d), wr, br)

    tok_sorted, blk_e, blk_s0, blk_end, blk_nv, n_used, pos = _route(ti[:, :TOP_K])
    xs = _gather_tokens(h3, tok_sorted, blk_s0, blk_end, n_used, MOE_TM)
    act = _moe_up(blk_e, blk_nv, xs, w_up, b_up.reshape(N_EXPERTS, 1, -1))
    ys3 = _moe_down(blk_e, blk_nv, act, w_down, b_down.reshape(N_EXPERTS, 1, -1))
    out = _combine(ys3, pos, tw[:, :TOP_K], x1)
    return out.reshape(bsz, t_len, d)


def kernel(x, ln1_g, w_in, conv_w, conv_b, dt_bias, a_log, d_skip, ssd_norm_g, q_norm_g,
           k_norm_g, cmp_pe, cmp_w1, cmp_w2, w_out, ln2_g, w_router, b_router, w_up, b_up,
           w_down, b_down):
    return _layer(x, ln1_g[0], w_in[0], conv_w[0], conv_b[0], dt_bias[0], a_log[0], d_skip[0],
                  ssd_norm_g[0], q_norm_g[0], k_norm_g[0], cmp_pe[0], cmp_w1[0], cmp_w2[0],
                  w_out[0], ln2_g[0], w_router[0], b_router[0], w_up[0], b_up[0], w_down[0],
                  b_down[0])
```

```python
import functools
import math

import jax
import jax.numpy as jnp
from jax import lax
from jax.experimental import pallas as pl
from jax.experimental.pallas import tpu as pltpu

f32 = jnp.float32
bf16 = jnp.bfloat16
i32 = jnp.int32

D_MODEL = 4096
SSD_D_INNER = 2048
SSD_HEAD_DIM = 64
SSD_HEADS = 32
SSD_GROUPS = 8
SSD_HPG = 4
SSD_STATE = 128
SSD_CONV = 4
SSD_CHUNK = 256
SSD_GN = SSD_GROUPS * SSD_STATE
SSD_GW = SSD_HPG * SSD_HEAD_DIM
NSA_HEADS = 16
NSA_KV_HEADS = 4
NSA_GQA = 4
DH = 128
NSA_WIDTH = 2048
NSA_KV_WIDTH = 512
CMP_BLOCK = 32
CMP_STRIDE = 16
CMP_HIDDEN = 256
SLC_BLOCK = 64
SLC_TOPK = 16
WINDOW = 512
N_EXPERTS = 32
TOP_K = 4
D_EXPERT = 1536
SWIGLU_LIMIT = 7.0
SWIGLU_ALPHA = 1.702
EPS = 1e-5
NEG = -1e30
LOG2E = 1.4426950408889634
SLC_GROUP = 4
VT_PAD = 16

COL_Z = 0
COL_X = 2048
COL_B = 4096
COL_C = 5120
COL_Q = 6144
COL_KV = 8192
COL_DT = 11264
GATE_OFF = 32
NP_PROJ = 11520

LANE = 128
VMEM_LIMIT = 56 * 1024 * 1024


def _cparams(n_axes):
    return pltpu.CompilerParams(dimension_semantics=("arbitrary",) * n_axes,
                                vmem_limit_bytes=VMEM_LIMIT)


def _silu(v):
    return v * (1.0 / (1.0 + jnp.exp(-v)))


def _inproj_body(x_ref, g_ref, w_ref, o_ref, h_ref):
    @pl.when(pl.program_id(1) == 0)
    def _():
        x = x_ref[...]
        ms = jnp.mean(x * x, axis=-1, keepdims=True)
        h_ref[...] = (x * lax.rsqrt(ms + EPS) * g_ref[...]).astype(bf16)

    o_ref[...] = jnp.dot(h_ref[...], w_ref[...], preferred_element_type=f32)


def _inproj(x2, g, w, tm=512, tn=768):
    n, d = x2.shape
    npj = w.shape[1]
    return pl.pallas_call(
        _inproj_body,
        grid=(n // tm, npj // tn),
        in_specs=[pl.BlockSpec((tm, d), lambda i, j: (i, 0)),
                  pl.BlockSpec((1, d), lambda i, j: (0, 0)),
                  pl.BlockSpec((d, tn), lambda i, j: (0, j))],
        out_specs=pl.BlockSpec((tm, tn), lambda i, j: (i, j)),
        out_shape=jax.ShapeDtypeStruct((n, npj), f32),
        scratch_shapes=[pltpu.VMEM((tm, d), bf16)],
        compiler_params=_cparams(2),
        name="inproj",
    )(x2, g, w)


def _lane_expand(cols, width):
    L = cols[0].shape[0]
    n = len(cols)
    lane = lax.broadcasted_iota(i32, (L, n * width), 1)
    out = jnp.broadcast_to(cols[n - 1], (L, n * width))
    for j in range(n - 2, -1, -1):
        out = jnp.where(lane < (j + 1) * width, jnp.broadcast_to(cols[j], (L, n * width)), out)
    return out


def _ssd_body(z_ref, x_ref, b_ref, c_ref, dt_ref, cw_ref, cb_ref, dtb_ref, alog_ref,
              dskip_ref, ng_ref, y_ref, buf_ref, xs_ref, bt_ref, cs_ref, h_ref):
    L = SSD_CHUNK
    c_idx = pl.program_id(1)

    @pl.when(c_idx == 0)
    def _():
        buf_ref[0:8, :] = jnp.zeros((8, buf_ref.shape[1]), f32)
        h_ref[...] = jnp.zeros(h_ref.shape, f32)

    buf_ref[8:8 + L, 0:SSD_D_INNER] = x_ref[...]
    buf_ref[8:8 + L, SSD_D_INNER:SSD_D_INNER + SSD_GN] = b_ref[...]
    buf_ref[8:8 + L, SSD_D_INNER + SSD_GN:] = c_ref[...]

    cw = 512
    n_ch = buf_ref.shape[1]
    for c0 in range(0, n_ch, cw):
        acc = jnp.broadcast_to(cb_ref[:, c0:c0 + cw], (L, cw))
        for k in range(SSD_CONV):
            acc = acc + cw_ref[k:k + 1, c0:c0 + cw] * buf_ref[5 + k:5 + k + L, c0:c0 + cw]
        v = _silu(acc)
        if c0 < SSD_D_INNER:
            xs_ref[:, c0:c0 + cw] = v
        elif c0 < SSD_D_INNER + SSD_GN:
            bt_ref[c0 - SSD_D_INNER:c0 - SSD_D_INNER + cw, :] = v.T.astype(bf16)
        else:
            o = c0 - SSD_D_INNER - SSD_GN
            cs_ref[:, o:o + cw] = v.astype(bf16)
    buf_ref[0:8, :] = buf_ref[L:L + 8, :]

    dt = dt_ref[...] + dtb_ref[...]
    dt = jnp.maximum(dt, 0.0) + jnp.log(1.0 + jnp.exp(-jnp.abs(dt)))
    da = dt * -jnp.exp(alog_ref[...])
    r = lax.broadcasted_iota(i32, (L, L), 0)
    s = lax.broadcasted_iota(i32, (L, L), 1)
    tri = (s <= r).astype(f32)
    acs = jnp.dot(tri, da, preferred_element_type=f32,
                  precision=lax.Precision.HIGHEST)
    acs_t = acs.T
    causal = s <= r
    a_last = acs[L - 1:L, :]
    e_acs = jnp.exp(acs)
    e_rem = jnp.exp(a_last - acs)
    e_last = jnp.exp(a_last)
    lane_gw = lax.broadcasted_iota(i32, (L, SSD_GW), 1)

    for g in range(SSD_GROUPS):
        hs = [g * SSD_HPG + j for j in range(SSD_HPG)]
        xg = xs_ref[:, g * SSD_GW:(g + 1) * SSD_GW]
        bgt = bt_ref[g * SSD_STATE:(g + 1) * SSD_STATE, :]
        cg = cs_ref[:, g * SSD_STATE:(g + 1) * SSD_STATE]
        dt_x = _lane_expand([dt[:, h:h + 1] for h in hs], SSD_HEAD_DIM)
        xdt = xg * dt_x
        cbm = jnp.dot(cg, bgt, preferred_element_type=f32)
        y = jnp.zeros((L, SSD_GW), f32)
        for j, h in enumerate(hs):
            seg = jnp.where(causal, acs[:, h:h + 1] - acs_t[h:h + 1, :], NEG)
            w = (cbm * jnp.exp(seg)).astype(bf16)
            band = (lane_gw >= j * SSD_HEAD_DIM) & (lane_gw < (j + 1) * SSD_HEAD_DIM)
            xm = jnp.where(band, xdt, 0.0).astype(bf16)
            y = y + jnp.dot(w, xm, preferred_element_type=f32)
        hprev = h_ref[g]
        y_off = jnp.dot(cg, hprev.astype(bf16), preferred_element_type=f32)
        y = y + y_off * _lane_expand([e_acs[:, h:h + 1] for h in hs], SSD_HEAD_DIM)
        xw = (xdt * _lane_expand([e_rem[:, h:h + 1] for h in hs], SSD_HEAD_DIM)).astype(bf16)
        st = jnp.dot(bgt, xw, preferred_element_type=f32)
        dec = _lane_expand([e_last[:, h:h + 1] for h in hs], SSD_HEAD_DIM)
        h_ref[g] = hprev * dec + st
        y = y + xg * dskip_ref[:, g * SSD_GW:(g + 1) * SSD_GW]
        y = y * _silu(z_ref[:, g * SSD_GW:(g + 1) * SSD_GW])
        y = y * lax.rsqrt(jnp.mean(y * y, axis=-1, keepdims=True) + EPS)
        y_ref[:, g * SSD_GW:(g + 1) * SSD_GW] = (y * ng_ref[:, g * SSD_GW:(g + 1) * SSD_GW]).astype(bf16)


def _ssd(proj, bsz, t_len, conv_w, conv_b, dt_bias, a_log, dskip_x, norm_g):
    L = SSD_CHUNK
    nc = t_len // L
    n = bsz * t_len
    nch = SSD_D_INNER + 2 * SSD_GN
    row = lambda b, c: b * nc + c
    full = lambda shape: pl.BlockSpec(shape, lambda b, c: (0,) * len(shape))
    return pl.pallas_call(
        _ssd_body,
        grid=(bsz, nc),
        in_specs=[pl.BlockSpec((L, SSD_D_INNER), lambda b, c: (row(b, c), COL_Z // SSD_D_INNER)),
                  pl.BlockSpec((L, SSD_D_INNER), lambda b, c: (row(b, c), COL_X // SSD_D_INNER)),
                  pl.BlockSpec((L, SSD_GN), lambda b, c: (row(b, c), COL_B // SSD_GN)),
                  pl.BlockSpec((L, SSD_GN), lambda b, c: (row(b, c), COL_C // SSD_GN)),
                  pl.BlockSpec((L, LANE), lambda b, c: (row(b, c), COL_DT // LANE)),
                  full((SSD_CONV, nch)), full((1, nch)), full((1, LANE)),
                  full((1, LANE)), full((1, SSD_D_INNER)), full((1, SSD_D_INNER))],
        out_specs=pl.BlockSpec((L, SSD_D_INNER), lambda b, c: (row(b, c), 0)),
        out_shape=jax.ShapeDtypeStruct((n, SSD_D_INNER), bf16),
        scratch_shapes=[pltpu.VMEM((L + 8, nch), f32),
                        pltpu.VMEM((L, SSD_D_INNER), f32),
                        pltpu.VMEM((SSD_GN, L), bf16),
                        pltpu.VMEM((L, SSD_GN), bf16),
                        pltpu.VMEM((SSD_GROUPS, SSD_STATE, SSD_GW), f32)],
        compiler_params=_cparams(2),
        name="ssd",
    )(proj, proj, proj, proj, proj, conv_w, conv_b, dt_bias, a_log, dskip_x, norm_g)


def _cmp_body(uk_ref, uv_ref, pe_ref, w1_ref, w2_ref, g_ref, kc_ref, vct_ref):
    nchunk = uk_ref.shape[0]

    def branch(u_ref, kv):
        hid_a = jnp.zeros((nchunk, CMP_HIDDEN), f32)
        hid_b = jnp.zeros((nchunk, CMP_HIDDEN), f32)
        ut = pltpu.einshape("csd->scd", u_ref[...])
        for s in range(CMP_STRIDE):
            us = ut[s]
            hid_a = hid_a + jnp.dot((us + pe_ref[kv, s:s + 1, :]).astype(bf16), w1_ref[kv, s],
                                    preferred_element_type=f32)
            hid_b = hid_b + jnp.dot(
                (us + pe_ref[kv, CMP_STRIDE + s:CMP_STRIDE + s + 1, :]).astype(bf16),
                w1_ref[kv, CMP_STRIDE + s], preferred_element_type=f32)
        hid = hid_a + pltpu.roll(hid_b, nchunk - 1, 0)
        out = jnp.dot(_silu(hid).astype(bf16), w2_ref[kv], preferred_element_type=f32)
        rowi = lax.broadcasted_iota(i32, out.shape, 0)
        return jnp.where(rowi == nchunk - 1, 0.0, out)

    k = branch(uk_ref, 0)
    k = k * lax.rsqrt(jnp.mean(k * k, axis=-1, keepdims=True) + EPS) * g_ref[...]
    kc_ref[0, 0] = k.astype(bf16)
    vct_ref[0, 0] = branch(uv_ref, 1).T.astype(bf16)


def _compress(proj, bsz, t_len, pe, w1, w2, kg):
    nchunk = t_len // CMP_STRIDE
    H = NSA_KV_HEADS
    u3 = proj.reshape(bsz * nchunk, CMP_STRIDE, proj.shape[1])
    useg = lambda kv: pl.BlockSpec((nchunk, CMP_STRIDE, DH),
                                   lambda b, h: (b, 0, COL_KV // DH + kv * H + h))
    full = lambda shape: pl.BlockSpec(shape, lambda b, h: (0,) * len(shape))
    return pl.pallas_call(
        _cmp_body,
        grid=(bsz, H),
        in_specs=[useg(0), useg(1), full((2, CMP_BLOCK, DH)), full((2, CMP_BLOCK, DH, CMP_HIDDEN)),
                  full((2, CMP_HIDDEN, DH)), full((1, DH))],
        out_specs=[pl.BlockSpec((1, 1, nchunk, DH), lambda b, h: (b, h, 0, 0)),
                   pl.BlockSpec((1, 1, DH, nchunk), lambda b, h: (b, h, 0, 0))],
        out_shape=[jax.ShapeDtypeStruct((bsz, H, nchunk, DH), bf16),
                   jax.ShapeDtypeStruct((bsz, H, DH, nchunk), bf16)],
        compiler_params=_cparams(2),
        name="nsa_compress",
    )(u3, u3, pe, w1, w2, kg)


def _qk_prep_body(q_ref, s_ref, w_ref, qg_ref, kg_ref, qt_ref, ks_ref, vst_ref, kw_ref, vwt_ref):
    scale = DH ** -0.5 * LOG2E
    tq = q_ref.shape[0]

    def hnorm(v, g):
        return v * lax.rsqrt(jnp.mean(v * v, axis=-1, keepdims=True) + EPS) * g

    ones_rows = (lax.broadcasted_iota(i32, (VT_PAD, tq), 0) == 0).astype(bf16)
    for h in range(NSA_KV_HEADS):
        for g in range(NSA_GQA):
            sl = slice((h * NSA_GQA + g) * DH, (h * NSA_GQA + g + 1) * DH)
            qn = hnorm(q_ref[:, sl], qg_ref[...]) * scale
            qt_ref[0, h, 0, :, g * tq:(g + 1) * tq] = qn.T.astype(bf16)
        sl = slice(h * DH, (h + 1) * DH)
        sv = slice(NSA_KV_WIDTH + h * DH, NSA_KV_WIDTH + (h + 1) * DH)
        ks_ref[:, sl] = hnorm(s_ref[:, sl], kg_ref[1:2, :]).astype(bf16)
        kw_ref[:, sl] = hnorm(w_ref[:, sl], kg_ref[2:3, :]).astype(bf16)
        vst_ref[0, h, 0, 0:DH] = s_ref[:, sv].T.astype(bf16)
        vwt_ref[0, h, 0, 0:DH] = w_ref[:, sv].T.astype(bf16)
        vst_ref[0, h, 0, DH:DH + VT_PAD] = ones_rows
        vwt_ref[0, h, 0, DH:DH + VT_PAD] = ones_rows


def _qk_prep(proj, qg, kg, bsz, t_len, tq):
    n = proj.shape[0]
    nq = t_len // tq
    H = NSA_KV_HEADS
    w2 = 2 * NSA_KV_WIDTH
    row = lambda b, i: b * nq + i
    vt_spec = pl.BlockSpec((1, H, 1, DH + VT_PAD, tq), lambda b, i: (b, 0, i, 0, 0))
    vt_shape = jax.ShapeDtypeStruct((bsz, H, nq, DH + VT_PAD, tq), bf16)
    return pl.pallas_call(
        _qk_prep_body,
        grid=(bsz, nq),
        in_specs=[pl.BlockSpec((tq, NSA_WIDTH), lambda b, i: (row(b, i), COL_Q // NSA_WIDTH)),
                  pl.BlockSpec((tq, w2), lambda b, i: (row(b, i), (COL_KV + w2) // w2)),
                  pl.BlockSpec((tq, w2), lambda b, i: (row(b, i), (COL_KV + 2 * w2) // w2)),
                  pl.BlockSpec((1, DH), lambda b, i: (0, 0)),
                  pl.BlockSpec((3, DH), lambda b, i: (0, 0))],
        out_specs=[pl.BlockSpec((1, H, 1, DH, NSA_GQA * tq), lambda b, i: (b, 0, i, 0, 0)),
                   pl.BlockSpec((tq, NSA_KV_WIDTH), lambda b, i: (row(b, i), 0)),
                   vt_spec,
                   pl.BlockSpec((tq, NSA_KV_WIDTH), lambda b, i: (row(b, i), 0)),
                   vt_spec],
        out_shape=[jax.ShapeDtypeStruct((bsz, H, nq, DH, NSA_GQA * tq), bf16),
                   jax.ShapeDtypeStruct((n, NSA_KV_WIDTH), bf16), vt_shape,
                   jax.ShapeDtypeStruct((n, NSA_KV_WIDTH), bf16), vt_shape],
        compiler_params=_cparams(2),
        name="nsa_qk_prep",
    )(proj, proj, proj, qg, kg)


def _nsa_body(qt_ref, kc_ref, vct_ref, ks_ref, vst_ref, kw_ref, vwt_ref, gt_ref, o_ref,
              m_ref, acc_ref, sel_ref, *, tq):
    i = pl.program_id(2)
    G = NSA_GQA
    n_cmp = kc_ref.shape[2]
    n_slc = ks_ref.shape[0] // SLC_BLOCK

    qt = qt_ref[0, 0, 0]
    t_row = i * tq + lax.broadcasted_iota(i32, (1, tq), 1)
    t_all = jnp.concatenate([t_row] * G, axis=1)

    s_c = jnp.dot(kc_ref[0, 0], qt, preferred_element_type=f32)
    cend = lax.broadcasted_iota(i32, (n_cmp, 1), 0) * CMP_STRIDE + (CMP_BLOCK - 1)
    m_c = cend <= t_all
    s_c = jnp.where(m_c, s_c, NEG)
    p_c = jnp.where(m_c, jnp.exp2(s_c - jnp.max(s_c, axis=0, keepdims=True)), 0.0)
    p_c = p_c / jnp.maximum(jnp.sum(p_c, axis=0, keepdims=True), 1e-30)
    o_c = jnp.dot(vct_ref[0, 0], p_c.astype(bf16), preferred_element_type=f32)

    imp = p_c[:, 0:tq]
    for g in range(1, G):
        imp = imp + p_c[:, g * tq:(g + 1) * tq]
    per = SLC_BLOCK // CMP_STRIDE
    ni = lax.broadcasted_iota(i32, (n_slc, n_cmp), 0)
    ci = lax.broadcasted_iota(i32, (n_slc, n_cmp), 1)
    fold = ((ci // per == ni).astype(f32) + ((ci + 1) // per == ni).astype(f32))
    imp_b = jnp.dot(fold, imp, preferred_element_type=f32, precision=lax.Precision.HIGHEST)
    jb = lax.broadcasted_iota(i32, (n_slc, tq), 0)
    cur = t_row // SLC_BLOCK
    forced = (jb == 0) | (jb == cur) | (jb == cur - 1)
    val = jnp.where(forced, 1e30, jnp.where(jb <= cur, imp_b, -1.0))
    sel = jnp.zeros((n_slc, tq), f32)
    for _ in range(SLC_TOPK):
        mx = jnp.max(val, axis=0, keepdims=True)
        first = jnp.min(jnp.where(val == mx, jb, n_slc), axis=0, keepdims=True)
        hit = (jb == first) & (mx >= 0.0)
        sel = jnp.where(hit, 1.0, sel)
        val = jnp.where(jb == first, -1.0, val)

    def online(tiles, slot):
        scores = [jnp.dot(k, qt, preferred_element_type=f32) + jnp.concatenate([bias] * G, axis=1)
                  for k, _, bias in tiles]
        m_old = m_ref[slot]
        m_new = m_old
        for s_ in scores:
            m_new = jnp.maximum(m_new, jnp.max(s_, axis=0, keepdims=True))
        acc = jnp.exp2(m_old - m_new) * acc_ref[slot]
        for s_, (_, vt, _) in zip(scores, tiles):
            acc = acc + jnp.dot(vt, jnp.exp2(s_ - m_new).astype(bf16), preferred_element_type=f32)
        acc_ref[slot] = acc
        m_ref[slot] = m_new

    m_ref[...] = jnp.full(m_ref.shape, NEG, f32)
    acc_ref[...] = jnp.zeros(acc_ref.shape, f32)
    krow = lax.broadcasted_iota(i32, (tq, 1), 0)
    kcol = lax.broadcasted_iota(i32, (1, tq), 1)

    sel_ref[...] = jnp.where(sel > 0.5, 0.0, NEG)
    bpt = tq // SLC_BLOCK

    def block_bias(kt):
        rows = [jnp.broadcast_to(sel_ref[pl.ds(kt * bpt + j, 1), :], (SLC_BLOCK, tq))
                for j in range(bpt)]
        return jnp.concatenate(rows, axis=0)

    def slc_tile(kt, causal=False):
        bias = block_bias(kt)
        if causal:
            bias = jnp.where(krow <= kcol, bias, NEG)
        return ks_ref[pl.ds(pl.multiple_of(kt * tq, tq), tq), :], vst_ref[0, 0, kt], bias

    def slc_group(kg, carry):
        online([slc_tile(SLC_GROUP * kg + j) for j in range(SLC_GROUP)], 0)
        return carry

    n_grp = i // SLC_GROUP
    lax.fori_loop(0, n_grp, slc_group, 0)
    for r in range(SLC_GROUP):
        @pl.when(i - n_grp * SLC_GROUP == r)
        def _():
            online([slc_tile(n_grp * SLC_GROUP + j) for j in range(r)] + [slc_tile(i, causal=True)], 0)

    def win_tile(kt):
        k0 = pl.multiple_of(kt * tq, tq)
        key = k0 + krow
        ok = (key <= t_row) & (key > t_row - WINDOW)
        return kw_ref[pl.ds(k0, tq), :], vwt_ref[0, 0, kt], jnp.where(ok, 0.0, NEG)

    back = WINDOW // tq
    for c in range(back + 1):
        @pl.when(jnp.minimum(i, back) == c)
        def _():
            online([win_tile(i - c + j) for j in range(c + 1)], 1)

    o_s = acc_ref[0, 0:DH] / acc_ref[0, DH:DH + 1]
    o_w = acc_ref[1, 0:DH] / acc_ref[1, DH:DH + 1]
    gts = 1.0 / (1.0 + jnp.exp(-gt_ref[0]))
    for g in range(G):
        cs = slice(g * tq, (g + 1) * tq)
        out_t = (gts[3 * g:3 * g + 1, :] * o_c[:, cs] + gts[3 * g + 1:3 * g + 2, :] * o_s[:, cs]
                 + gts[3 * g + 2:3 * g + 3, :] * o_w[:, cs])
        o_ref[:, g * DH:(g + 1) * DH] = out_t.T.astype(bf16)


def _nsa_attention(qt, kc, vct, ks, vst, kw, vwt, gates, bsz, t_len, tq):
    n = bsz * t_len
    nq = t_len // tq
    nchunk = t_len // CMP_STRIDE
    R = NSA_GQA * tq
    seq = pl.BlockSpec((t_len, DH), lambda b, h, i: (b, h))
    seq_t = pl.BlockSpec((1, 1, nq, DH + VT_PAD, tq), lambda b, h, i: (b, h, 0, 0, 0))
    return pl.pallas_call(
        functools.partial(_nsa_body, tq=tq),
        grid=(bsz, NSA_KV_HEADS, nq),
        in_specs=[pl.BlockSpec((1, 1, 1, DH, R), lambda b, h, i: (b, h, i, 0, 0)),
                  pl.BlockSpec((1, 1, nchunk, DH), lambda b, h, i: (b, h, 0, 0)),
                  pl.BlockSpec((1, 1, DH, nchunk), lambda b, h, i: (b, h, 0, 0)),
                  seq, seq_t, seq, seq_t,
                  pl.BlockSpec((1, 3 * NSA_GQA, tq), lambda b, h, i: (h, 0, b * nq + i))],
        out_specs=pl.BlockSpec((tq, NSA_GQA * DH), lambda b, h, i: (b * nq + i, h)),
        out_shape=jax.ShapeDtypeStruct((n, NSA_WIDTH), bf16),
        scratch_shapes=[pltpu.VMEM((2, 1, R), f32), pltpu.VMEM((2, DH + VT_PAD, R), f32),
                        pltpu.VMEM((t_len // SLC_BLOCK, tq), f32)],
        compiler_params=_cparams(3),
        name="nsa_attention",
    )(qt, kc, vct, ks, vst, kw, vwt, gates)


ROW_CH = LANE


def _outproj_body(ya_ref, yb_ref, wa_ref, wb_ref, x_ref, x1_ref):
    acc = x_ref[...] + jnp.dot(ya_ref[...], wa_ref[...], preferred_element_type=f32)
    x1_ref[...] = acc + jnp.dot(yb_ref[...], wb_ref[...], preferred_element_type=f32)


def _outproj(ya, yb, wa, wb, x2, tm=512, tn=1024):
    n, d = x2.shape
    ka = ya.shape[1]
    kb = yb.shape[1]
    return pl.pallas_call(
        _outproj_body,
        grid=(n // tm, d // tn),
        in_specs=[pl.BlockSpec((tm, ka), lambda i, j: (i, 0)),
                  pl.BlockSpec((tm, kb), lambda i, j: (i, 0)),
                  pl.BlockSpec((ka, tn), lambda i, j: (0, j)),
                  pl.BlockSpec((kb, tn), lambda i, j: (0, j)),
                  pl.BlockSpec((tm, tn), lambda i, j: (i, j))],
        out_specs=pl.BlockSpec((tm, tn), lambda i, j: (i, j)),
        out_shape=jax.ShapeDtypeStruct((n, d), f32),
        compiler_params=_cparams(2),
        name="outproj",
    )(ya, yb, wa, wb, x2)


def _router_body(x_ref, g_ref, wr_ref, br_ref, h_ref, ti_ref, tw_ref):
    tm, d = x_ref.shape
    x = x_ref[...]
    hh = x * lax.rsqrt(jnp.mean(x * x, axis=-1, keepdims=True) + EPS) * g_ref[...]
    hi = hh.astype(bf16)
    lo = (hh - hi.astype(f32)).astype(bf16)
    t = jnp.dot(hi, wr_ref[...], preferred_element_type=f32)
    logits = br_ref[...] + (t[:, :LANE] + (t[:, LANE:] + jnp.dot(lo, wr_ref[:, 0:LANE],
                                                                  preferred_element_type=f32)))
    for r0 in range(0, tm, 8):
        blk = jnp.stack([hh[r0:r0 + 8, c * ROW_CH:(c + 1) * ROW_CH] for c in range(d // ROW_CH)], axis=0)
        h_ref[r0:r0 + 8] = pltpu.einshape("crl->rcl", blk)
    lane = lax.broadcasted_iota(i32, (tm, LANE), 1)
    val = jnp.where(lane < N_EXPERTS, logits, -jnp.inf)
    idxs = jnp.zeros((tm, LANE), i32)
    vals = jnp.full((tm, LANE), -jnp.inf, f32)
    for k in range(TOP_K):
        mx = jnp.max(val, axis=-1, keepdims=True)
        first = jnp.min(jnp.where(val == mx, lane, LANE), axis=-1, keepdims=True)
        idxs = jnp.where(lane == k, first, idxs)
        vals = jnp.where(lane == k, mx, vals)
        val = jnp.where(lane == first, -jnp.inf, val)
    e = jnp.exp(vals - jnp.max(vals, axis=-1, keepdims=True))
    tw_ref[...] = e / jnp.sum(e, axis=-1, keepdims=True)
    ti_ref[...] = idxs


def _norm_router(x1, g2, wr, br, tm=256):
    n, d = x1.shape
    return pl.pallas_call(
        _router_body,
        grid=(n // tm,),
        in_specs=[pl.BlockSpec((tm, d), lambda i: (i, 0)),
                  pl.BlockSpec((1, d), lambda i: (0, 0)),
                  pl.BlockSpec((d, 2 * LANE), lambda i: (0, 0)),
                  pl.BlockSpec((1, LANE), lambda i: (0, 0))],
        out_specs=[pl.BlockSpec((tm, d // ROW_CH, ROW_CH), lambda i: (i, 0, 0)),
                   pl.BlockSpec((tm, LANE), lambda i: (i, 0)),
                   pl.BlockSpec((tm, LANE), lambda i: (i, 0))],
        out_shape=[jax.ShapeDtypeStruct((n, d // ROW_CH, ROW_CH), f32),
                   jax.ShapeDtypeStruct((n, LANE), i32),
                   jax.ShapeDtypeStruct((n, LANE), f32)],
        compiler_params=_cparams(1),
        name="norm_router",
    )(x1, g2, wr, br)


MOE_TM = 512
DMA_UNROLL = 8
DMA_THREADS = 2


class _RowGather:
    def __init__(self, src_ref, dst_of, bulk_of, sem, count, chunks=1):
        self.src_ref, self.dst_of, self.bulk_of, self.sem = src_ref, dst_of, bulk_of, sem
        self.per = count // chunks

    def start(self, idx_of, slot, c=0):
        def body(t, carry):
            for p in range(DMA_THREADS):
                r = c * self.per + DMA_THREADS * t + p
                pltpu.make_async_copy(self.src_ref.at[idx_of(r)], self.dst_of(slot, r),
                                      self.sem.at[slot, c]).start(priority=p)
            return carry
        lax.fori_loop(0, self.per // DMA_THREADS, body, 0, unroll=DMA_UNROLL // DMA_THREADS)

    def wait(self, slot, c=0):
        for src, dst in self.bulk_of(slot, c):
            pltpu.make_async_copy(src, dst, self.sem.at[slot, c]).wait()


GATHER_CHUNKS = 8


def _gather_body(tok_ref, s0_ref, end_ref, nu_ref, src_ref, o_ref, buf_ref, sem):
    i = pl.program_id(0)
    tmr = o_ref.shape[0]
    nch = buf_ref.shape[2]
    per = tmr // GATHER_CHUNKS
    slot = lax.rem(i, 2)
    n_live = jnp.minimum(nu_ref[0], pl.num_programs(0))

    def tokens_of(tile):
        s0 = s0_ref[tile]
        last = end_ref[tile] - 1
        return lambda r: tok_ref[jnp.minimum(s0 + r, last)]

    rows = _RowGather(
        src_ref, lambda s, r: buf_ref.at[s, r],
        lambda s, c: [(src_ref.at[pl.ds(0, per)], buf_ref.at[s, pl.ds(c * per, per)])],
        sem, tmr, GATHER_CHUNKS)

    @pl.when((i == 0) & (n_live > 0))
    def _():
        for c in range(GATHER_CHUNKS):
            rows.start(tokens_of(0), 0, c)

    @pl.when(i < n_live)
    def _():
        for c in range(GATHER_CHUNKS):
            @pl.when(i + 1 < n_live)
            def _():
                rows.start(tokens_of(i + 1), 1 - slot, c)

            rows.wait(slot, c)
            for r0 in range(c * per, (c + 1) * per, 8):
                blk = pltpu.einshape("rcl->crl", buf_ref[slot, r0:r0 + 8])
                for k in range(nch):
                    o_ref[r0:r0 + 8, k * ROW_CH:(k + 1) * ROW_CH] = blk[k].astype(o_ref.dtype)

    @pl.when(i >= n_live)
    def _():
        o_ref[...] = jnp.zeros(o_ref.shape, o_ref.dtype)


def _gather_tokens(src3, tok_sorted, blk_s0, blk_end, n_used, tmr):
    _, nch, chw = src3.shape
    nt = blk_s0.shape[0]
    gs = pltpu.PrefetchScalarGridSpec(
        num_scalar_prefetch=4,
        grid=(nt,),
        in_specs=[pl.BlockSpec(memory_space=pl.ANY)],
        out_specs=pl.BlockSpec((tmr, nch * chw), lambda i, *_: (i, 0)),
        scratch_shapes=[pltpu.VMEM((2, tmr, nch, chw), src3.dtype),
                        pltpu.SemaphoreType.DMA((2, GATHER_CHUNKS))])
    return pl.pallas_call(
        _gather_body, grid_spec=gs,
        out_shape=jax.ShapeDtypeStruct((nt * tmr, nch * chw), bf16),
        compiler_params=_cparams(1),
        name="moe_gather",
    )(tok_sorted, blk_s0, blk_end, n_used, src3)


def _expert_changed(be_ref, i):
    return (i == 0) | (be_ref[i] != be_ref[jnp.maximum(i - 1, 0)])


def _row_cases(nv, tm, compute, o_ref):
    half = tm // 2

    @pl.when(nv > half)
    def _():
        compute(tm)

    @pl.when((nv > 0) & (nv <= half))
    def _():
        compute(half)
        o_ref[half:] = jnp.zeros((tm - half,) + o_ref.shape[1:], o_ref.dtype)

    @pl.when(nv <= 0)
    def _():
        o_ref[...] = jnp.zeros(o_ref.shape, o_ref.dtype)


def _up_body(be_ref, nv_ref, x_ref, wg_ref, wl_ref, bg_ref, bl_ref, o_ref, ws_ref):
    i = pl.program_id(1)
    tn = o_ref.shape[1]
    nv = nv_ref[i]

    @pl.when((nv > 0) & _expert_changed(be_ref, i))
    def _():
        ws_ref[:, 0:tn] = wg_ref[0].astype(bf16)
        ws_ref[:, tn:2 * tn] = wl_ref[0].astype(bf16)

    def compute(m):
        u = jnp.dot(x_ref[0:m], ws_ref[...], preferred_element_type=f32)
        glu = u[:, 0:tn] + bg_ref[0]
        lin = u[:, tn:2 * tn] + bl_ref[0]
        glu = jnp.minimum(glu, SWIGLU_LIMIT)
        lin = jnp.clip(lin, -SWIGLU_LIMIT, SWIGLU_LIMIT)
        act = glu * (1.0 / (1.0 + jnp.exp(-SWIGLU_ALPHA * glu))) * (lin + 1.0)
        o_ref[0:m] = act.astype(bf16)

    _row_cases(nv, o_ref.shape[0], compute, o_ref)


def _moe_up(blk_e, blk_nv, xs, w_up, b_up3, tn=384):
    n_rows, d = xs.shape
    nb = n_rows // MOE_TM
    nj = D_EXPERT // tn
    gs = pltpu.PrefetchScalarGridSpec(
        num_scalar_prefetch=2,
        grid=(nj, nb),
        in_specs=[pl.BlockSpec((MOE_TM, d), lambda j, i, be, nu: (i, 0)),
                  pl.BlockSpec((1, d, tn), lambda j, i, be, nu: (be[i], 0, j)),
                  pl.BlockSpec((1, d, tn), lambda j, i, be, nu: (be[i], 0, nj + j)),
                  pl.BlockSpec((1, 1, tn), lambda j, i, be, nu: (be[i], 0, j)),
                  pl.BlockSpec((1, 1, tn), lambda j, i, be, nu: (be[i], 0, nj + j))],
        out_specs=pl.BlockSpec((MOE_TM, tn), lambda j, i, be, nu: (i, j)),
        scratch_shapes=[pltpu.VMEM((d, 2 * tn), bf16)])
    return pl.pallas_call(
        _up_body, grid_spec=gs,
        out_shape=jax.ShapeDtypeStruct((n_rows, D_EXPERT), bf16),
        compiler_params=_cparams(2),
        name="moe_up",
    )(blk_e, blk_nv, xs, w_up, w_up, b_up3, b_up3)


def _down_body(be_ref, nv_ref, a_ref, w_ref, b_ref, o_ref, ws_ref):
    i = pl.program_id(1)
    nv = nv_ref[i]

    @pl.when((nv > 0) & _expert_changed(be_ref, i))
    def _():
        ws_ref[...] = w_ref[0].astype(bf16)

    def compute(m):
        y = jnp.dot(a_ref[0:m], ws_ref[...], preferred_element_type=f32) + b_ref[0]
        nch = o_ref.shape[1]
        for r0 in range(0, m, 8):
            blk = jnp.stack([y[r0:r0 + 8, c * ROW_CH:(c + 1) * ROW_CH] for c in range(nch)], axis=0)
            o_ref[r0:r0 + 8] = pltpu.einshape("crl->rcl", blk)

    _row_cases(nv, o_ref.shape[0], compute, o_ref)


def _moe_down(blk_e, blk_nv, act, w_down, b_down3, tn=2048):
    n_rows, de = act.shape
    d = w_down.shape[2]
    nb = n_rows // MOE_TM
    nj = d // tn
    gs = pltpu.PrefetchScalarGridSpec(
        num_scalar_prefetch=2,
        grid=(nj, nb),
        in_specs=[pl.BlockSpec((MOE_TM, de), lambda j, i, be, nu: (i, 0)),
                  pl.BlockSpec((1, de, tn), lambda j, i, be, nu: (be[i], 0, j)),
                  pl.BlockSpec((1, 1, tn), lambda j, i, be, nu: (be[i], 0, j))],
        out_specs=pl.BlockSpec((MOE_TM, tn // ROW_CH, ROW_CH), lambda j, i, be, nu: (i, j, 0)),
        scratch_shapes=[pltpu.VMEM((de, tn), bf16)])
    return pl.pallas_call(
        _down_body, grid_spec=gs,
        out_shape=jax.ShapeDtypeStruct((n_rows, d // ROW_CH, ROW_CH), f32),
        compiler_params=_cparams(2),
        name="moe_down",
    )(blk_e, blk_nv, act, w_down, b_down3)


def _combine_body(cur_ref, nxt_ref, w_ref, ys_ref, x_ref, o_ref, buf_ref, tot_ref, sem):
    i = pl.program_id(0)
    nt = pl.num_programs(0)
    tm = o_ref.shape[0]
    nch = buf_ref.shape[3]
    slot = lax.rem(i, 2)
    rows = _RowGather(
        ys_ref, lambda s, a: buf_ref.at[s, a & (TOP_K - 1), lax.shift_right_logical(a, 2)],
        lambda s, c: [(ys_ref.at[pl.ds(0, tm)], buf_ref.at[s, k]) for k in range(TOP_K)],
        sem, tm * TOP_K)
    cur = lambda a: cur_ref[0, 0, a]

    @pl.when(i == 0)
    def _():
        rows.start(cur, 0)

    @pl.when(i + 1 < nt)
    def _():
        rows.start(lambda a: nxt_ref[0, 0, a], 1 - slot)

    rows.wait(slot)

    def weigh(t, c):
        tot = buf_ref[slot, 0, t] * w_ref[0, 0, TOP_K * t]
        for k in range(1, TOP_K):
            tot = tot + buf_ref[slot, k, t] * w_ref[0, 0, TOP_K * t + k]
        tot_ref[t] = tot
        return c

    lax.fori_loop(0, tm, weigh, 0, unroll=4)
    for r0 in range(0, tm, 8):
        blk = pltpu.einshape("rcl->crl", tot_ref[r0:r0 + 8])
        for c in range(nch):
            cs = slice(c * ROW_CH, (c + 1) * ROW_CH)
            o_ref[r0:r0 + 8, cs] = x_ref[r0:r0 + 8, cs] + blk[c]


def _combine(ys3, pos, top_w, x1, tm=128):
    n, d = x1.shape
    _, nch, chw = ys3.shape
    nt = n // tm
    pos3 = pos.reshape(nt, 1, tm * TOP_K)
    idx_spec = lambda f: pl.BlockSpec((1, 1, tm * TOP_K), f, memory_space=pltpu.SMEM)
    return pl.pallas_call(
        _combine_body,
        grid=(nt,),
        in_specs=[idx_spec(lambda i: (i, 0, 0)),
                  idx_spec(lambda i: (jnp.minimum(i + 1, nt - 1), 0, 0)),
                  idx_spec(lambda i: (i, 0, 0)),
                  pl.BlockSpec(memory_space=pl.ANY),
                  pl.BlockSpec((tm, d), lambda i: (i, 0))],
        out_specs=pl.BlockSpec((tm, d), lambda i: (i, 0)),
        out_shape=jax.ShapeDtypeStruct((n, d), f32),
        scratch_shapes=[pltpu.VMEM((2, TOP_K, tm, nch, chw), f32), pltpu.VMEM((tm, nch, chw), f32),
                        pltpu.SemaphoreType.DMA((2, 1))],
        compiler_params=_cparams(1),
        name="moe_combine",
    )(pos3, pos3, top_w.reshape(nt, 1, tm * TOP_K), ys3, x1)


def _route(top_idx):
    n = top_idx.shape[0]
    n_assign = n * TOP_K
    e_flat = top_idx.reshape(-1)
    a_iota = jnp.arange(n_assign, dtype=i32)
    _, order = lax.sort((e_flat, a_iota), num_keys=1, is_stable=True)
    _, rank = lax.sort((order, a_iota), num_keys=1)
    experts = jnp.arange(N_EXPERTS, dtype=i32)
    onehot = experts[:, None] == e_flat[None, :]
    counts = jnp.sum(onehot.astype(i32), axis=1)
    padded = (counts + MOE_TM - 1) // MOE_TM * MOE_TM
    start = jnp.cumsum(counts) - counts
    pend = jnp.cumsum(padded)
    shift = pend - padded - start
    pos = rank + jnp.sum(jnp.where(onehot, shift[:, None], 0), axis=0)
    n_blocks = -(-n_assign // MOE_TM) + N_EXPERTS
    n_used = pend[-1] // MOE_TM
    bi = jnp.arange(n_blocks, dtype=i32)
    raw_e = jnp.minimum(jnp.sum((pend[None, :] <= (bi * MOE_TM)[:, None]).astype(i32), axis=1),
                        N_EXPERTS - 1)
    blk_e = jnp.where(bi < n_used, raw_e, raw_e[jnp.maximum(n_used - 1, 0)]).astype(i32)
    blk_s0 = (bi * MOE_TM - shift[blk_e]).astype(i32)
    blk_end = (start[blk_e] + counts[blk_e]).astype(i32)
    blk_nv = jnp.where(bi < n_used, jnp.clip(blk_end - blk_s0, 0, MOE_TM), 0).astype(i32)
    tok_sorted = lax.shift_right_logical(order, 2)
    return (tok_sorted, blk_e, blk_s0, blk_end, blk_nv, n_used.reshape(1).astype(i32),
            pos.astype(i32))


NSA_TQ = 256


def _layer(x, ln1_g, w_in, conv_w, conv_b, dt_bias, a_log, d_skip, ssd_norm_g, q_norm_g,
           k_norm_g, cmp_pe, cmp_w1, cmp_w2, w_out, ln2_g, w_router, b_router, w_up, b_up,
           w_down, b_down):
    bsz, t_len, d = x.shape
    n = bsz * t_len
    x2 = x.reshape(n, d)

    o_dt = SSD_D_INNER + SSD_D_INNER + 2 * SSD_GN
    o_q = o_dt + SSD_HEADS
    o_kv = o_q + NSA_WIDTH
    o_g = o_kv + 6 * NSA_KV_WIDTH
    n_in = o_g + 3 * NSA_HEADS
    w_perm = jnp.concatenate(
        [w_in[:, :o_dt], w_in[:, o_q:o_g], w_in[:, o_dt:o_q], w_in[:, o_g:n_in],
         jnp.zeros((d, NP_PROJ - n_in), w_in.dtype)], axis=1).astype(bf16)

    proj = _inproj(x2, ln1_g.reshape(1, d), w_perm)

    lane_pad = lambda v: jnp.concatenate([v, jnp.zeros((LANE - v.shape[0],), f32)]).reshape(1, LANE)
    y_ssd = _ssd(proj, bsz, t_len, conv_w, conv_b.reshape(1, -1), lane_pad(dt_bias),
                 lane_pad(a_log), jnp.repeat(d_skip, SSD_HEAD_DIM).reshape(1, -1),
                 ssd_norm_g.reshape(1, -1))

    kc, vct = _compress(proj, bsz, t_len, cmp_pe, cmp_w1.astype(bf16), cmp_w2.astype(bf16),
                        k_norm_g[0:1])
    qt, ks, vst, kw, vwt = _qk_prep(proj, q_norm_g.reshape(1, DH), k_norm_g, bsz, t_len, NSA_TQ)
    gates = proj[:, COL_DT + GATE_OFF:COL_DT + GATE_OFF + 3 * NSA_HEADS]
    gates = gates.reshape(n, NSA_KV_HEADS, 3 * NSA_GQA).transpose(1, 2, 0)
    y_nsa = _nsa_attention(qt, kc, vct, ks, vst, kw, vwt, gates, bsz, t_len, NSA_TQ)

    wo = w_out.astype(bf16)
    wr = jnp.concatenate([w_router, jnp.zeros((d, LANE - N_EXPERTS), f32)], axis=1)
    wr_hi = wr.astype(bf16)
    wr = jnp.concatenate([wr_hi, (wr - wr_hi.astype(f32)).astype(bf16)], axis=1)
    br =jnp.concatenate([b_router, jnp.zeros((LANE - N_EXPERTS,), f32)]).reshape(1, LANE)
    x1 = _outproj(y_ssd, y_nsa, wo[:SSD_D_INNER], wo[SSD_D_INNER:], x2)
    h3, ti, tw = _norm_router(x1, ln2_g.reshape(1, d), wr, br)

    tok_sorted, blk_e, blk_s0, blk_end, blk_nv, n_used, pos = _route(ti[:, :TOP_K])
    xs = _gather_tokens(h3, tok_sorted, blk_s0, blk_end, n_used, MOE_TM)
    act = _moe_up(blk_e, blk_nv, xs, w_up, b_up.reshape(N_EXPERTS, 1, -1))
    ys3 = _moe_down(blk_e, blk_nv, act, w_down, b_down.reshape(N_EXPERTS, 1, -1))
    out = _combine(ys3, pos, tw[:, :TOP_K], x1)
    return out.reshape(bsz, t_len, d)


def kernel(x, ln1_g, w_in, conv_w, conv_b, dt_bias, a_log, d_skip, ssd_norm_g, q_norm_g,
           k_norm_g, cmp_pe, cmp_w1, cmp_w2, w_out, ln2_g, w_router, b_router, w_up, b_up,
           w_down, b_down):
    return _layer(x, ln1_g[0], w_in[0], conv_w[0], conv_b[0], dt_bias[0], a_log[0], d_skip[0],
                  ssd_norm_g[0], q_norm_g[0], k_norm_g[0], cmp_pe[0], cmp_w1[0], cmp_w2[0],
                  w_out[0], ln2_g[0], w_router[0], b_router[0], w_up[0], b_up[0], w_down[0],
                  b_down[0])
```

```python
import functools
import math

import jax
import jax.numpy as jnp
from jax import lax
from jax.experimental import pallas as pl
from jax.experimental.pallas import tpu as pltpu

f32 = jnp.float32
bf16 = jnp.bfloat16
i32 = jnp.int32

D_MODEL = 4096
SSD_D_INNER = 2048
SSD_HEAD_DIM = 64
SSD_HEADS = 32
SSD_GROUPS = 8
SSD_HPG = 4
SSD_STATE = 128
SSD_CONV = 4
SSD_CHUNK = 256
SSD_GN = SSD_GROUPS * SSD_STATE
SSD_GW = SSD_HPG * SSD_HEAD_DIM
NSA_HEADS = 16
NSA_KV_HEADS = 4
NSA_GQA = 4
DH = 128
NSA_WIDTH = 2048
NSA_KV_WIDTH = 512
CMP_BLOCK = 32
CMP_STRIDE = 16
CMP_HIDDEN = 256
SLC_BLOCK = 64
SLC_TOPK = 16
WINDOW = 512
N_EXPERTS = 32
TOP_K = 4
D_EXPERT = 1536
SWIGLU_LIMIT = 7.0
SWIGLU_ALPHA = 1.702
EPS = 1e-5
NEG = -1e30
LOG2E = 1.4426950408889634
SLC_GROUP = 4
VT_PAD = 16

COL_Z = 0
COL_X = 2048
COL_B = 4096
COL_C = 5120
COL_Q = 6144
COL_KV = 8192
COL_DT = 11264
GATE_OFF = 32
NP_PROJ = 11520

LANE = 128
VMEM_LIMIT = 56 * 1024 * 1024


def _cparams(n_axes):
    return pltpu.CompilerParams(dimension_semantics=("arbitrary",) * n_axes,
                                vmem_limit_bytes=VMEM_LIMIT)


def _silu(v):
    return v * (1.0 / (1.0 + jnp.exp(-v)))


def _inproj_body(x_ref, g_ref, w_ref, o_ref, h_ref):
    @pl.when(pl.program_id(1) == 0)
    def _():
        x = x_ref[...]
        ms = jnp.mean(x * x, axis=-1, keepdims=True)
        h_ref[...] = (x * lax.rsqrt(ms + EPS) * g_ref[...]).astype(bf16)

    o_ref[...] = jnp.dot(h_ref[...], w_ref[...], preferred_element_type=f32)


def _inproj(x2, g, w, tm=512, tn=768):
    n, d = x2.shape
    npj = w.shape[1]
    return pl.pallas_call(
        _inproj_body,
        grid=(n // tm, npj // tn),
        in_specs=[pl.BlockSpec((tm, d), lambda i, j: (i, 0)),
                  pl.BlockSpec((1, d), lambda i, j: (0, 0)),
                  pl.BlockSpec((d, tn), lambda i, j: (0, j))],
        out_specs=pl.BlockSpec((tm, tn), lambda i, j: (i, j)),
        out_shape=jax.ShapeDtypeStruct((n, npj), f32),
        scratch_shapes=[pltpu.VMEM((tm, d), bf16)],
        compiler_params=_cparams(2),
        name="inproj",
    )(x2, g, w)


def _lane_expand(cols, width):
    L = cols[0].shape[0]
    n = len(cols)
    lane = lax.broadcasted_iota(i32, (L, n * width), 1)
    out = jnp.broadcast_to(cols[n - 1], (L, n * width))
    for j in range(n - 2, -1, -1):
        out = jnp.where(lane < (j + 1) * width, jnp.broadcast_to(cols[j], (L, n * width)), out)
    return out


def _ssd_body(z_ref, x_ref, b_ref, c_ref, dt_ref, cw_ref, cb_ref, dtb_ref, alog_ref,
              dskip_ref, ng_ref, y_ref, buf_ref, xs_ref, bt_ref, cs_ref, h_ref):
    L = SSD_CHUNK
    c_idx = pl.program_id(1)

    @pl.when(c_idx == 0)
    def _():
        buf_ref[0:8, :] = jnp.zeros((8, buf_ref.shape[1]), f32)
        h_ref[...] = jnp.zeros(h_ref.shape, f32)

    buf_ref[8:8 + L, 0:SSD_D_INNER] = x_ref[...]
    buf_ref[8:8 + L, SSD_D_INNER:SSD_D_INNER + SSD_GN] = b_ref[...]
    buf_ref[8:8 + L, SSD_D_INNER + SSD_GN:] = c_ref[...]

    cw = 512
    n_ch = buf_ref.shape[1]
    for c0 in range(0, n_ch, cw):
        acc = jnp.broadcast_to(cb_ref[:, c0:c0 + cw], (L, cw))
        for k in range(SSD_CONV):
            acc = acc + cw_ref[k:k + 1, c0:c0 + cw] * buf_ref[5 + k:5 + k + L, c0:c0 + cw]
        v = _silu(acc)
        if c0 < SSD_D_INNER:
            xs_ref[:, c0:c0 + cw] = v
        elif c0 < SSD_D_INNER + SSD_GN:
            bt_ref[c0 - SSD_D_INNER:c0 - SSD_D_INNER + cw, :] = v.T.astype(bf16)
        else:
            o = c0 - SSD_D_INNER - SSD_GN
            cs_ref[:, o:o + cw] = v.astype(bf16)
    buf_ref[0:8, :] = buf_ref[L:L + 8, :]

    dt = dt_ref[...] + dtb_ref[...]
    dt = jnp.maximum(dt, 0.0) + jnp.log(1.0 + jnp.exp(-jnp.abs(dt)))
    da = dt * -jnp.exp(alog_ref[...])
    r = lax.broadcasted_iota(i32, (L, L), 0)
    s = lax.broadcasted_iota(i32, (L, L), 1)
    tri = (s <= r).astype(f32)
    acs = jnp.dot(tri, da, preferred_element_type=f32,
                  precision=lax.Precision.HIGHEST)
    acs_t = acs.T
    causal = s <= r
    a_last = acs[L - 1:L, :]
    e_acs = jnp.exp(acs)
    e_rem = jnp.exp(a_last - acs)
    e_last = jnp.exp(a_last)
    lane_gw = lax.broadcasted_iota(i32, (L, SSD_GW), 1)

    for g in range(SSD_GROUPS):
        hs = [g * SSD_HPG + j for j in range(SSD_HPG)]
        xg = xs_ref[:, g * SSD_GW:(g + 1) * SSD_GW]
        bgt = bt_ref[g * SSD_STATE:(g + 1) * SSD_STATE, :]
        cg = cs_ref[:, g * SSD_STATE:(g + 1) * SSD_STATE]
        dt_x = _lane_expand([dt[:, h:h + 1] for h in hs], SSD_HEAD_DIM)
        xdt = xg * dt_x
        cbm = jnp.dot(cg, bgt, preferred_element_type=f32)
        y = jnp.zeros((L, SSD_GW), f32)
        for j, h in enumerate(hs):
            seg = jnp.where(causal, acs[:, h:h + 1] - acs_t[h:h + 1, :], NEG)
            w = (cbm * jnp.exp(seg)).astype(bf16)
            band = (lane_gw >= j * SSD_HEAD_DIM) & (lane_gw < (j + 1) * SSD_HEAD_DIM)
            xm = jnp.where(band, xdt, 0.0).astype(bf16)
            y = y + jnp.dot(w, xm, preferred_element_type=f32)
        hprev = h_ref[g]
        y_off = jnp.dot(cg, hprev.astype(bf16), preferred_element_type=f32)
        y = y + y_off * _lane_expand([e_acs[:, h:h + 1] for h in hs], SSD_HEAD_DIM)
        xw = (xdt * _lane_expand([e_rem[:, h:h + 1] for h in hs], SSD_HEAD_DIM)).astype(bf16)
        st = jnp.dot(bgt, xw, preferred_element_type=f32)
        dec = _lane_expand([e_last[:, h:h + 1] for h in hs], SSD_HEAD_DIM)
        h_ref[g] = hprev * dec + st
        y = y + xg * dskip_ref[:, g * SSD_GW:(g + 1) * SSD_GW]
        y = y * _silu(z_ref[:, g * SSD_GW:(g + 1) * SSD_GW])
        y = y * lax.rsqrt(jnp.mean(y * y, axis=-1, keepdims=True) + EPS)
        y_ref[:, g * SSD_GW:(g + 1) * SSD_GW] = (y * ng_ref[:, g * SSD_GW:(g + 1) * SSD_GW]).astype(bf16)


def _ssd(proj, bsz, t_len, conv_w, conv_b, dt_bias, a_log, dskip_x, norm_g):
    L = SSD_CHUNK
    nc = t_len // L
    n = bsz * t_len
    nch = SSD_D_INNER + 2 * SSD_GN
    row = lambda b, c: b * nc + c
    full = lambda shape: pl.BlockSpec(shape, lambda b, c: (0,) * len(shape))
    return pl.pallas_call(
        _ssd_body,
        grid=(bsz, nc),
        in_specs=[pl.BlockSpec((L, SSD_D_INNER), lambda b, c: (row(b, c), COL_Z // SSD_D_INNER)),
                  pl.BlockSpec((L, SSD_D_INNER), lambda b, c: (row(b, c), COL_X // SSD_D_INNER)),
                  pl.BlockSpec((L, SSD_GN), lambda b, c: (row(b, c), COL_B // SSD_GN)),
                  pl.BlockSpec((L, SSD_GN), lambda b, c: (row(b, c), COL_C // SSD_GN)),
                  pl.BlockSpec((L, LANE), lambda b, c: (row(b, c), COL_DT // LANE)),
                  full((SSD_CONV, nch)), full((1, nch)), full((1, LANE)),
                  full((1, LANE)), full((1, SSD_D_INNER)), full((1, SSD_D_INNER))],
        out_specs=pl.BlockSpec((L, SSD_D_INNER), lambda b, c: (row(b, c), 0)),
        out_shape=jax.ShapeDtypeStruct((n, SSD_D_INNER), bf16),
        scratch_shapes=[pltpu.VMEM((L + 8, nch), f32),
                        pltpu.VMEM((L, SSD_D_INNER), f32),
                        pltpu.VMEM((SSD_GN, L), bf16),
                        pltpu.VMEM((L, SSD_GN), bf16),
                        pltpu.VMEM((SSD_GROUPS, SSD_STATE, SSD_GW), f32)],
        compiler_params=_cparams(2),
        name="ssd",
    )(proj, proj, proj, proj, proj, conv_w, conv_b, dt_bias, a_log, dskip_x, norm_g)


def _cmp_body(uk_ref, uv_ref, pe_ref, w1_ref, w2_ref, g_ref, kc_ref, vct_ref):
    nchunk = uk_ref.shape[0]

    def branch(u_ref, kv):
        hid_a = jnp.zeros((nchunk, CMP_HIDDEN), f32)
        hid_b = jnp.zeros((nchunk, CMP_HIDDEN), f32)
        ut = pltpu.einshape("csd->scd", u_ref[...])
        for s in range(CMP_STRIDE):
            us = ut[s]
            hid_a = hid_a + jnp.dot((us + pe_ref[kv, s:s + 1, :]).astype(bf16), w1_ref[kv, s],
                                    preferred_element_type=f32)
            hid_b = hid_b + jnp.dot(
                (us + pe_ref[kv, CMP_STRIDE + s:CMP_STRIDE + s + 1, :]).astype(bf16),
                w1_ref[kv, CMP_STRIDE + s], preferred_element_type=f32)
        hid = hid_a + pltpu.roll(hid_b, nchunk - 1, 0)
        out = jnp.dot(_silu(hid).astype(bf16), w2_ref[kv], preferred_element_type=f32)
        rowi = lax.broadcasted_iota(i32, out.shape, 0)
        return jnp.where(rowi == nchunk - 1, 0.0, out)

    k = branch(uk_ref, 0)
    k = k * lax.rsqrt(jnp.mean(k * k, axis=-1, keepdims=True) + EPS) * g_ref[...]
    kc_ref[0, 0] = k.astype(bf16)
    vct_ref[0, 0] = branch(uv_ref, 1).T.astype(bf16)


def _compress(proj, bsz, t_len, pe, w1, w2, kg):
    nchunk = t_len // CMP_STRIDE
    H = NSA_KV_HEADS
    u3 = proj.reshape(bsz * nchunk, CMP_STRIDE, proj.shape[1])
    useg = lambda kv: pl.BlockSpec((nchunk, CMP_STRIDE, DH),
                                   lambda b, h: (b, 0, COL_KV // DH + kv * H + h))
    full = lambda shape: pl.BlockSpec(shape, lambda b, h: (0,) * len(shape))
    return pl.pallas_call(
        _cmp_body,
        grid=(bsz, H),
        in_specs=[useg(0), useg(1), full((2, CMP_BLOCK, DH)), full((2, CMP_BLOCK, DH, CMP_HIDDEN)),
                  full((2, CMP_HIDDEN, DH)), full((1, DH))],
        out_specs=[pl.BlockSpec((1, 1, nchunk, DH), lambda b, h: (b, h, 0, 0)),
                   pl.BlockSpec((1, 1, DH, nchunk), lambda b, h: (b, h, 0, 0))],
        out_shape=[jax.ShapeDtypeStruct((bsz, H, nchunk, DH), bf16),
                   jax.ShapeDtypeStruct((bsz, H, DH, nchunk), bf16)],
        compiler_params=_cparams(2),
        name="nsa_compress",
    )(u3, u3, pe, w1, w2, kg)


def _qk_prep_body(q_ref, s_ref, w_ref, qg_ref, kg_ref, qt_ref, ks_ref, vst_ref, kw_ref, vwt_ref):
    scale = DH ** -0.5 * LOG2E
    tq = q_ref.shape[0]

    def hnorm(v, g):
        return v * lax.rsqrt(jnp.mean(v * v, axis=-1, keepdims=True) + EPS) * g

    ones_rows = (lax.broadcasted_iota(i32, (VT_PAD, tq), 0) == 0).astype(bf16)
    for h in range(NSA_KV_HEADS):
        for g in range(NSA_GQA):
            sl = slice((h * NSA_GQA + g) * DH, (h * NSA_GQA + g + 1) * DH)
            qn = hnorm(q_ref[:, sl], qg_ref[...]) * scale
            qt_ref[0, h, 0, :, g * tq:(g + 1) * tq] = qn.T.astype(bf16)
        sl = slice(h * DH, (h + 1) * DH)
        sv = slice(NSA_KV_WIDTH + h * DH, NSA_KV_WIDTH + (h + 1) * DH)
        ks_ref[:, sl] = hnorm(s_ref[:, sl], kg_ref[1:2, :]).astype(bf16)
        kw_ref[:, sl] = hnorm(w_ref[:, sl], kg_ref[2:3, :]).astype(bf16)
        vst_ref[0, h, 0, 0:DH] = s_ref[:, sv].T.astype(bf16)
        vwt_ref[0, h, 0, 0:DH] = w_ref[:, sv].T.astype(bf16)
        vst_ref[0, h, 0, DH:DH + VT_PAD] = ones_rows
        vwt_ref[0, h, 0, DH:DH + VT_PAD] = ones_rows


def _qk_prep(proj, qg, kg, bsz, t_len, tq):
    n = proj.shape[0]
    nq = t_len // tq
    H = NSA_KV_HEADS
    w2 = 2 * NSA_KV_WIDTH
    row = lambda b, i: b * nq + i
    vt_spec = pl.BlockSpec((1, H, 1, DH + VT_PAD, tq), lambda b, i: (b, 0, i, 0, 0))
    vt_shape = jax.ShapeDtypeStruct((bsz, H, nq, DH + VT_PAD, tq), bf16)
    return pl.pallas_call(
        _qk_prep_body,
        grid=(bsz, nq),
        in_specs=[pl.BlockSpec((tq, NSA_WIDTH), lambda b, i: (row(b, i), COL_Q // NSA_WIDTH)),
                  pl.BlockSpec((tq, w2), lambda b, i: (row(b, i), (COL_KV + w2) // w2)),
                  pl.BlockSpec((tq, w2), lambda b, i: (row(b, i), (COL_KV + 2 * w2) // w2)),
                  pl.BlockSpec((1, DH), lambda b, i: (0, 0)),
                  pl.BlockSpec((3, DH), lambda b, i: (0, 0))],
        out_specs=[pl.BlockSpec((1, H, 1, DH, NSA_GQA * tq), lambda b, i: (b, 0, i, 0, 0)),
                   pl.BlockSpec((tq, NSA_KV_WIDTH), lambda b, i: (row(b, i), 0)),
                   vt_spec,
                   pl.BlockSpec((tq, NSA_KV_WIDTH), lambda b, i: (row(b, i), 0)),
                   vt_spec],
        out_shape=[jax.ShapeDtypeStruct((bsz, H, nq, DH, NSA_GQA * tq), bf16),
                   jax.ShapeDtypeStruct((n, NSA_KV_WIDTH), bf16), vt_shape,
                   jax.ShapeDtypeStruct((n, NSA_KV_WIDTH), bf16), vt_shape],
        compiler_params=_cparams(2),
        name="nsa_qk_prep",
    )(proj, proj, proj, qg, kg)


def _nsa_body(qt_ref, kc_ref, vct_ref, ks_ref, vst_ref, kw_ref, vwt_ref, gt_ref, o_ref,
              m_ref, acc_ref, sel_ref, *, tq):
    i = pl.program_id(2)
    G = NSA_GQA
    n_cmp = kc_ref.shape[2]
    n_slc = ks_ref.shape[0] // SLC_BLOCK

    qt = qt_ref[0, 0, 0]
    t_row = i * tq + lax.broadcasted_iota(i32, (1, tq), 1)
    t_all = jnp.concatenate([t_row] * G, axis=1)

    s_c = jnp.dot(kc_ref[0, 0], qt, preferred_element_type=f32)
    cend = lax.broadcasted_iota(i32, (n_cmp, 1), 0) * CMP_STRIDE + (CMP_BLOCK - 1)
    m_c = cend <= t_all
    s_c = jnp.where(m_c, s_c, NEG)
    p_c = jnp.where(m_c, jnp.exp2(s_c - jnp.max(s_c, axis=0, keepdims=True)), 0.0)
    p_c = p_c / jnp.maximum(jnp.sum(p_c, axis=0, keepdims=True), 1e-30)
    o_c = jnp.dot(vct_ref[0, 0], p_c.astype(bf16), preferred_element_type=f32)

    imp = p_c[:, 0:tq]
    for g in range(1, G):
        imp = imp + p_c[:, g * tq:(g + 1) * tq]
    per = SLC_BLOCK // CMP_STRIDE
    ni = lax.broadcasted_iota(i32, (n_slc, n_cmp), 0)
    ci = lax.broadcasted_iota(i32, (n_slc, n_cmp), 1)
    fold = ((ci // per == ni).astype(f32) + ((ci + 1) // per == ni).astype(f32))
    imp_b = jnp.dot(fold, imp, preferred_element_type=f32, precision=lax.Precision.HIGHEST)
    jb = lax.broadcasted_iota(i32, (n_slc, tq), 0)
    cur = t_row // SLC_BLOCK
    forced = (jb == 0) | (jb == cur) | (jb == cur - 1)
    val = jnp.where(forced, 1e30, jnp.where(jb <= cur, imp_b, -1.0))
    sel = jnp.zeros((n_slc, tq), f32)
    for _ in range(SLC_TOPK):
        mx = jnp.max(val, axis=0, keepdims=True)
        first = jnp.min(jnp.where(val == mx, jb, n_slc), axis=0, keepdims=True)
        hit = (jb == first) & (mx >= 0.0)
        sel = jnp.where(hit, 1.0, sel)
        val = jnp.where(jb == first, -1.0, val)

    def online(tiles, slot):
        scores = [jnp.dot(k, qt, preferred_element_type=f32) + jnp.concatenate([bias] * G, axis=1)
                  for k, _, bias in tiles]
        m_old = m_ref[slot]
        m_new = m_old
        for s_ in scores:
            m_new = jnp.maximum(m_new, jnp.max(s_, axis=0, keepdims=True))
        acc = jnp.exp2(m_old - m_new) * acc_ref[slot]
        for s_, (_, vt, _) in zip(scores, tiles):
            acc = acc + jnp.dot(vt, jnp.exp2(s_ - m_new).astype(bf16), preferred_element_type=f32)
        acc_ref[slot] = acc
        m_ref[slot] = m_new

    m_ref[...] = jnp.full(m_ref.shape, NEG, f32)
    acc_ref[...] = jnp.zeros(acc_ref.shape, f32)
    krow = lax.broadcasted_iota(i32, (tq, 1), 0)
    kcol = lax.broadcasted_iota(i32, (1, tq), 1)

    sel_ref[...] = jnp.where(sel > 0.5, 0.0, NEG)
    bpt = tq // SLC_BLOCK

    def block_bias(kt):
        rows = [jnp.broadcast_to(sel_ref[pl.ds(kt * bpt + j, 1), :], (SLC_BLOCK, tq))
                for j in range(bpt)]
        return jnp.concatenate(rows, axis=0)

    def slc_tile(kt, causal=False):
        bias = block_bias(kt)
        if causal:
            bias = jnp.where(krow <= kcol, bias, NEG)
        return ks_ref[pl.ds(pl.multiple_of(kt * tq, tq), tq), :], vst_ref[0, 0, kt], bias

    def slc_group(kg, carry):
        online([slc_tile(SLC_GROUP * kg + j) for j in range(SLC_GROUP)], 0)
        return carry

    n_grp = i // SLC_GROUP
    lax.fori_loop(0, n_grp, slc_group, 0)
    for r in range(SLC_GROUP):
        @pl.when(i - n_grp * SLC_GROUP == r)
        def _():
            online([slc_tile(n_grp * SLC_GROUP + j) for j in range(r)] + [slc_tile(i, causal=True)], 0)

    def win_tile(kt):
        k0 = pl.multiple_of(kt * tq, tq)
        key = k0 + krow
        ok = (key <= t_row) & (key > t_row - WINDOW)
        return kw_ref[pl.ds(k0, tq), :], vwt_ref[0, 0, kt], jnp.where(ok, 0.0, NEG)

    back = WINDOW // tq
    for c in range(back + 1):
        @pl.when(jnp.minimum(i, back) == c)
        def _():
            online([win_tile(i - c + j) for j in range(c + 1)], 1)

    o_s = acc_ref[0, 0:DH] / acc_ref[0, DH:DH + 1]
    o_w = acc_ref[1, 0:DH] / acc_ref[1, DH:DH + 1]
    gts = 1.0 / (1.0 + jnp.exp(-gt_ref[0]))
    for g in range(G):
        cs = slice(g * tq, (g + 1) * tq)
        out_t = (gts[3 * g:3 * g + 1, :] * o_c[:, cs] + gts[3 * g + 1:3 * g + 2, :] * o_s[:, cs]
                 + gts[3 * g + 2:3 * g + 3, :] * o_w[:, cs])
        o_ref[:, g * DH:(g + 1) * DH] = out_t.T.astype(bf16)


def _nsa_attention(qt, kc, vct, ks, vst, kw, vwt, gates, bsz, t_len, tq):
    n = bsz * t_len
    nq = t_len // tq
    nchunk = t_len // CMP_STRIDE
    R = NSA_GQA * tq
    seq = pl.BlockSpec((t_len, DH), lambda b, h, i: (b, h))
    seq_t = pl.BlockSpec((1, 1, nq, DH + VT_PAD, tq), lambda b, h, i: (b, h, 0, 0, 0))
    return pl.pallas_call(
        functools.partial(_nsa_body, tq=tq),
        grid=(bsz, NSA_KV_HEADS, nq),
        in_specs=[pl.BlockSpec((1, 1, 1, DH, R), lambda b, h, i: (b, h, i, 0, 0)),
                  pl.BlockSpec((1, 1, nchunk, DH), lambda b, h, i: (b, h, 0, 0)),
                  pl.BlockSpec((1, 1, DH, nchunk), lambda b, h, i: (b, h, 0, 0)),
                  seq, seq_t, seq, seq_t,
                  pl.BlockSpec((1, 3 * NSA_GQA, tq), lambda b, h, i: (h, 0, b * nq + i))],
        out_specs=pl.BlockSpec((tq, NSA_GQA * DH), lambda b, h, i: (b * nq + i, h)),
        out_shape=jax.ShapeDtypeStruct((n, NSA_WIDTH), bf16),
        scratch_shapes=[pltpu.VMEM((2, 1, R), f32), pltpu.VMEM((2, DH + VT_PAD, R), f32),
                        pltpu.VMEM((t_len // SLC_BLOCK, tq), f32)],
        compiler_params=_cparams(3),
        name="nsa_attention",
    )(qt, kc, vct, ks, vst, kw, vwt, gates)


ROW_CH = LANE


def _outproj_body(ya_ref, yb_ref, wa_ref, wb_ref, x_ref, x1_ref):
    acc = x_ref[...] + jnp.dot(ya_ref[...], wa_ref[...], preferred_element_type=f32)
    x1_ref[...] = acc + jnp.dot(yb_ref[...], wb_ref[...], preferred_element_type=f32)


def _outproj(ya, yb, wa, wb, x2, tm=512, tn=1024):
    n, d = x2.shape
    ka = ya.shape[1]
    kb = yb.shape[1]
    return pl.pallas_call(
        _outproj_body,
        grid=(n // tm, d // tn),
        in_specs=[pl.BlockSpec((tm, ka), lambda i, j: (i, 0)),
                  pl.BlockSpec((tm, kb), lambda i, j: (i, 0)),
                  pl.BlockSpec((ka, tn), lambda i, j: (0, j)),
                  pl.BlockSpec((kb, tn), lambda i, j: (0, j)),
                  pl.BlockSpec((tm, tn), lambda i, j: (i, j))],
        out_specs=pl.BlockSpec((tm, tn), lambda i, j: (i, j)),
        out_shape=jax.ShapeDtypeStruct((n, d), f32),
        compiler_params=_cparams(2),
        name="outproj",
    )(ya, yb, wa, wb, x2)


def _router_body(x_ref, g_ref, wr_ref, br_ref, h_ref, ti_ref, tw_ref):
    tm, d = x_ref.shape
    x = x_ref[...]
    hh = x * lax.rsqrt(jnp.mean(x * x, axis=-1, keepdims=True) + EPS) * g_ref[...]
    hi = hh.astype(bf16)
    lo = (hh - hi.astype(f32)).astype(bf16)
    t = jnp.dot(hi, wr_ref[...], preferred_element_type=f32)
    logits = br_ref[...] + (t[:, :LANE] + (t[:, LANE:] + jnp.dot(lo, wr_ref[:, 0:LANE],
                                                                  preferred_element_type=f32)))
    for r0 in range(0, tm, 8):
        blk = jnp.stack([hh[r0:r0 + 8, c * ROW_CH:(c + 1) * ROW_CH] for c in range(d // ROW_CH)], axis=0)
        h_ref[r0:r0 + 8] = pltpu.einshape("crl->rcl", blk)
    lane = lax.broadcasted_iota(i32, (tm, LANE), 1)
    val = jnp.where(lane < N_EXPERTS, logits, -jnp.inf)
    idxs = jnp.zeros((tm, LANE), i32)
    vals = jnp.full((tm, LANE), -jnp.inf, f32)
    for k in range(TOP_K):
        mx = jnp.max(val, axis=-1, keepdims=True)
        first = jnp.min(jnp.where(val == mx, lane, LANE), axis=-1, keepdims=True)
        idxs = jnp.where(lane == k, first, idxs)
        vals = jnp.where(lane == k, mx, vals)
        val = jnp.where(lane == first, -jnp.inf, val)
    e = jnp.exp(vals - jnp.max(vals, axis=-1, keepdims=True))
    tw_ref[...] = e / jnp.sum(e, axis=-1, keepdims=True)
    ti_ref[...] = idxs


def _norm_router(x1, g2, wr, br, tm=256):
    n, d = x1.shape
    return pl.pallas_call(
        _router_body,
        grid=(n // tm,),
        in_specs=[pl.BlockSpec((tm, d), lambda i: (i, 0)),
                  pl.BlockSpec((1, d), lambda i: (0, 0)),
                  pl.BlockSpec((d, 2 * LANE), lambda i: (0, 0)),
                  pl.BlockSpec((1, LANE), lambda i: (0, 0))],
        out_specs=[pl.BlockSpec((tm, d // ROW_CH, ROW_CH), lambda i: (i, 0, 0)),
                   pl.BlockSpec((tm, LANE), lambda i: (i, 0)),
                   pl.BlockSpec((tm, LANE), lambda i: (i, 0))],
        out_shape=[jax.ShapeDtypeStruct((n, d // ROW_CH, ROW_CH), f32),
                   jax.ShapeDtypeStruct((n, LANE), i32),
                   jax.ShapeDtypeStruct((n, LANE), f32)],
        compiler_params=_cparams(1),
        name="norm_router",
    )(x1, g2, wr, br)


MOE_TM = 512
DMA_UNROLL = 8
DMA_THREADS = 2


class _RowGather:
    def __init__(self, src_ref, dst_of, bulk_of, sem, count, chunks=1):
        self.src_ref, self.dst_of, self.bulk_of, self.sem = src_ref, dst_of, bulk_of, sem
        self.per = count // chunks

    def start(self, idx_of, slot, c=0):
        def body(t, carry):
            for p in range(DMA_THREADS):
                r = c * self.per + DMA_THREADS * t + p
                pltpu.make_async_copy(self.src_ref.at[idx_of(r)], self.dst_of(slot, r),
                                      self.sem.at[slot, c]).start(priority=p)
            return carry
        lax.fori_loop(0, self.per // DMA_THREADS, body, 0, unroll=DMA_UNROLL // DMA_THREADS)

    def wait(self, slot, c=0):
        for src, dst in self.bulk_of(slot, c):
            pltpu.make_async_copy(src, dst, self.sem.at[slot, c]).wait()


GATHER_CHUNKS = 8


def _gather_body(tok_ref, s0_ref, end_ref, nu_ref, src_ref, o_ref, buf_ref, sem):
    i = pl.program_id(0)
    tmr = o_ref.shape[0]
    nch = buf_ref.shape[2]
    per = tmr // GATHER_CHUNKS
    slot = lax.rem(i, 2)
    n_live = jnp.minimum(nu_ref[0], pl.num_programs(0))

    def tokens_of(tile):
        s0 = s0_ref[tile]
        last = end_ref[tile] - 1
        return lambda r: tok_ref[jnp.minimum(s0 + r, last)]

    rows = _RowGather(
        src_ref, lambda s, r: buf_ref.at[s, r],
        lambda s, c: [(src_ref.at[pl.ds(0, per)], buf_ref.at[s, pl.ds(c * per, per)])],
        sem, tmr, GATHER_CHUNKS)

    @pl.when((i == 0) & (n_live > 0))
    def _():
        for c in range(GATHER_CHUNKS):
            rows.start(tokens_of(0), 0, c)

    @pl.when(i < n_live)
    def _():
        for c in range(GATHER_CHUNKS):
            @pl.when(i + 1 < n_live)
            def _():
                rows.start(tokens_of(i + 1), 1 - slot, c)

            rows.wait(slot, c)
            for r0 in range(c * per, (c + 1) * per, 8):
                blk = pltpu.einshape("rcl->crl", buf_ref[slot, r0:r0 + 8])
                for k in range(nch):
                    o_ref[r0:r0 + 8, k * ROW_CH:(k + 1) * ROW_CH] = blk[k].astype(o_ref.dtype)

    @pl.when(i >= n_live)
    def _():
        o_ref[...] = jnp.zeros(o_ref.shape, o_ref.dtype)


def _gather_tokens(src3, tok_sorted, blk_s0, blk_end, n_used, tmr):
    _, nch, chw = src3.shape
    nt = blk_s0.shape[0]
    gs = pltpu.PrefetchScalarGridSpec(
        num_scalar_prefetch=4,
        grid=(nt,),
        in_specs=[pl.BlockSpec(memory_space=pl.ANY)],
        out_specs=pl.BlockSpec((tmr, nch * chw), lambda i, *_: (i, 0)),
        scratch_shapes=[pltpu.VMEM((2, tmr, nch, chw), src3.dtype),
                        pltpu.SemaphoreType.DMA((2, GATHER_CHUNKS))])
    return pl.pallas_call(
        _gather_body, grid_spec=gs,
        out_shape=jax.ShapeDtypeStruct((nt * tmr, nch * chw), bf16),
        compiler_params=_cparams(1),
        name="moe_gather",
    )(tok_sorted, blk_s0, blk_end, n_used, src3)


def _expert_changed(be_ref, i):
    return (i == 0) | (be_ref[i] != be_ref[jnp.maximum(i - 1, 0)])


def _row_cases(nv, tm, compute, o_ref):
    half = tm // 2

    @pl.when(nv > half)
    def _():
        compute(tm)

    @pl.when((nv > 0) & (nv <= half))
    def _():
        compute(half)
        o_ref[half:] = jnp.zeros((tm - half,) + o_ref.shape[1:], o_ref.dtype)

    @pl.when(nv <= 0)
    def _():
        o_ref[...] = jnp.zeros(o_ref.shape, o_ref.dtype)


def _up_body(be_ref, nv_ref, x_ref, wg_ref, wl_ref, bg_ref, bl_ref, o_ref, ws_ref):
    i = pl.program_id(1)
    tn = o_ref.shape[1]
    nv = nv_ref[i]

    @pl.when((nv > 0) & _expert_changed(be_ref, i))
    def _():
        ws_ref[:, 0:tn] = wg_ref[0].astype(bf16)
        ws_ref[:, tn:2 * tn] = wl_ref[0].astype(bf16)

    def compute(m):
        u = jnp.dot(x_ref[0:m], ws_ref[...], preferred_element_type=f32)
        glu = u[:, 0:tn] + bg_ref[0]
        lin = u[:, tn:2 * tn] + bl_ref[0]
        glu = jnp.minimum(glu, SWIGLU_LIMIT)
        lin = jnp.clip(lin, -SWIGLU_LIMIT, SWIGLU_LIMIT)
        act = glu * (1.0 / (1.0 + jnp.exp(-SWIGLU_ALPHA * glu))) * (lin + 1.0)
        o_ref[0:m] = act.astype(bf16)

    _row_cases(nv, o_ref.shape[0], compute, o_ref)


def _moe_up(blk_e, blk_nv, xs, w_up, b_up3, tn=512):
    n_rows, d = xs.shape
    nb = n_rows // MOE_TM
    nj = D_EXPERT // tn
    gs = pltpu.PrefetchScalarGridSpec(
        num_scalar_prefetch=2,
        grid=(nj, nb),
        in_specs=[pl.BlockSpec((MOE_TM, d), lambda j, i, be, nu: (i, 0)),
                  pl.BlockSpec((1, d, tn), lambda j, i, be, nu: (be[i], 0, j)),
                  pl.BlockSpec((1, d, tn), lambda j, i, be, nu: (be[i], 0, nj + j)),
                  pl.BlockSpec((1, 1, tn), lambda j, i, be, nu: (be[i], 0, j)),
                  pl.BlockSpec((1, 1, tn), lambda j, i, be, nu: (be[i], 0, nj + j))],
        out_specs=pl.BlockSpec((MOE_TM, tn), lambda j, i, be, nu: (i, j)),
        scratch_shapes=[pltpu.VMEM((d, 2 * tn), bf16)])
    return pl.pallas_call(
        _up_body, grid_spec=gs,
        out_shape=jax.ShapeDtypeStruct((n_rows, D_EXPERT), bf16),
        compiler_params=_cparams(2),
        name="moe_up",
    )(blk_e, blk_nv, xs, w_up, w_up, b_up3, b_up3)


def _down_body(be_ref, nv_ref, a_ref, w_ref, b_ref, o_ref, ws_ref):
    i = pl.program_id(1)
    nv = nv_ref[i]

    @pl.when((nv > 0) & _expert_changed(be_ref, i))
    def _():
        ws_ref[...] = w_ref[0].astype(bf16)

    def compute(m):
        y = jnp.dot(a_ref[0:m], ws_ref[...], preferred_element_type=f32) + b_ref[0]
        nch = o_ref.shape[1]
        for r0 in range(0, m, 8):
            blk = jnp.stack([y[r0:r0 + 8, c * ROW_CH:(c + 1) * ROW_CH] for c in range(nch)], axis=0)
            o_ref[r0:r0 + 8] = pltpu.einshape("crl->rcl", blk)

    _row_cases(nv, o_ref.shape[0], compute, o_ref)


def _moe_down(blk_e, blk_nv, act, w_down, b_down3, tn=2048):
    n_rows, de = act.shape
    d = w_down.shape[2]
    nb = n_rows // MOE_TM
    nj = d // tn
    gs = pltpu.PrefetchScalarGridSpec(
        num_scalar_prefetch=2,
        grid=(nj, nb),
        in_specs=[pl.BlockSpec((MOE_TM, de), lambda j, i, be, nu: (i, 0)),
                  pl.BlockSpec((1, de, tn), lambda j, i, be, nu: (be[i], 0, j)),
                  pl.BlockSpec((1, 1, tn), lambda j, i, be, nu: (be[i], 0, j))],
        out_specs=pl.BlockSpec((MOE_TM, tn // ROW_CH, ROW_CH), lambda j, i, be, nu: (i, j, 0)),
        scratch_shapes=[pltpu.VMEM((de, tn), bf16)])
    return pl.pallas_call(
        _down_body, grid_spec=gs,
        out_shape=jax.ShapeDtypeStruct((n_rows, d // ROW_CH, ROW_CH), f32),
        compiler_params=_cparams(2),
        name="moe_down",
    )(blk_e, blk_nv, act, w_down, b_down3)


def _combine_body(cur_ref, nxt_ref, w_ref, ys_ref, x_ref, o_ref, buf_ref, tot_ref, sem):
    i = pl.program_id(0)
    nt = pl.num_programs(0)
    tm = o_ref.shape[0]
    nch = buf_ref.shape[3]
    slot = lax.rem(i, 2)
    rows = _RowGather(
        ys_ref, lambda s, a: buf_ref.at[s, a & (TOP_K - 1), lax.shift_right_logical(a, 2)],
        lambda s, c: [(ys_ref.at[pl.ds(0, tm)], buf_ref.at[s, k]) for k in range(TOP_K)],
        sem, tm * TOP_K)
    cur = lambda a: cur_ref[0, 0, a]

    @pl.when(i == 0)
    def _():
        rows.start(cur, 0)

    @pl.when(i + 1 < nt)
    def _():
        rows.start(lambda a: nxt_ref[0, 0, a], 1 - slot)

    rows.wait(slot)

    def weigh(t, c):
        tot = buf_ref[slot, 0, t] * w_ref[0, 0, TOP_K * t]
        for k in range(1, TOP_K):
            tot = tot + buf_ref[slot, k, t] * w_ref[0, 0, TOP_K * t + k]
        tot_ref[t] = tot
        return c

    lax.fori_loop(0, tm, weigh, 0, unroll=4)
    for r0 in range(0, tm, 8):
        blk = pltpu.einshape("rcl->crl", tot_ref[r0:r0 + 8])
        for c in range(nch):
            cs = slice(c * ROW_CH, (c + 1) * ROW_CH)
            o_ref[r0:r0 + 8, cs] = x_ref[r0:r0 + 8, cs] + blk[c]


def _combine(ys3, pos, top_w, x1, tm=128):
    n, d = x1.shape
    _, nch, chw = ys3.shape
    nt = n // tm
    pos3 = pos.reshape(nt, 1, tm * TOP_K)
    idx_spec = lambda f: pl.BlockSpec((1, 1, tm * TOP_K), f, memory_space=pltpu.SMEM)
    return pl.pallas_call(
        _combine_body,
        grid=(nt,),
        in_specs=[idx_spec(lambda i: (i, 0, 0)),
                  idx_spec(lambda i: (jnp.minimum(i + 1, nt - 1), 0, 0)),
                  idx_spec(lambda i: (i, 0, 0)),
                  pl.BlockSpec(memory_space=pl.ANY),
                  pl.BlockSpec((tm, d), lambda i: (i, 0))],
        out_specs=pl.BlockSpec((tm, d), lambda i: (i, 0)),
        out_shape=jax.ShapeDtypeStruct((n, d), f32),
        scratch_shapes=[pltpu.VMEM((2, TOP_K, tm, nch, chw), f32), pltpu.VMEM((tm, nch, chw), f32),
                        pltpu.SemaphoreType.DMA((2, 1))],
        compiler_params=_cparams(1),
        name="moe_combine",
    )(pos3, pos3, top_w.reshape(nt, 1, tm * TOP_K), ys3, x1)


def _route(top_idx):
    n = top_idx.shape[0]
    n_assign = n * TOP_K
    e_flat = top_idx.reshape(-1)
    a_iota = jnp.arange(n_assign, dtype=i32)
    _, order = lax.sort((e_flat, a_iota), num_keys=1, is_stable=True)
    _, rank = lax.sort((order, a_iota), num_keys=1)
    experts = jnp.arange(N_EXPERTS, dtype=i32)
    onehot = experts[:, None] == e_flat[None, :]
    counts = jnp.sum(onehot.astype(i32), axis=1)
    padded = (counts + MOE_TM - 1) // MOE_TM * MOE_TM
    start = jnp.cumsum(counts) - counts
    pend = jnp.cumsum(padded)
    shift = pend - padded - start
    pos = rank + jnp.sum(jnp.where(onehot, shift[:, None], 0), axis=0)
    n_blocks = -(-n_assign // MOE_TM) + N_EXPERTS
    n_used = pend[-1] // MOE_TM
    bi = jnp.arange(n_blocks, dtype=i32)
    raw_e = jnp.minimum(jnp.sum((pend[None, :] <= (bi * MOE_TM)[:, None]).astype(i32), axis=1),
                        N_EXPERTS - 1)
    blk_e = jnp.where(bi < n_used, raw_e, raw_e[jnp.maximum(n_used - 1, 0)]).astype(i32)
    blk_s0 = (bi * MOE_TM - shift[blk_e]).astype(i32)
    blk_end = (start[blk_e] + counts[blk_e]).astype(i32)
    blk_nv = jnp.where(bi < n_used, jnp.clip(blk_end - blk_s0, 0, MOE_TM), 0).astype(i32)
    tok_sorted = lax.shift_right_logical(order, 2)
    return (tok_sorted, blk_e, blk_s0, blk_end, blk_nv, n_used.reshape(1).astype(i32),
            pos.astype(i32))


NSA_TQ = 256


def _layer(x, ln1_g, w_in, conv_w, conv_b, dt_bias, a_log, d_skip, ssd_norm_g, q_norm_g,
           k_norm_g, cmp_pe, cmp_w1, cmp_w2, w_out, ln2_g, w_router, b_router, w_up, b_up,
           w_down, b_down):
    bsz, t_len, d = x.shape
    n = bsz * t_len
    x2 = x.reshape(n, d)

    o_dt = SSD_D_INNER + SSD_D_INNER + 2 * SSD_GN
    o_q = o_dt + SSD_HEADS
    o_kv = o_q + NSA_WIDTH
    o_g = o_kv + 6 * NSA_KV_WIDTH
    n_in = o_g + 3 * NSA_HEADS
    w_perm = jnp.concatenate(
        [w_in[:, :o_dt], w_in[:, o_q:o_g], w_in[:, o_dt:o_q], w_in[:, o_g:n_in],
         jnp.zeros((d, NP_PROJ - n_in), w_in.dtype)], axis=1).astype(bf16)

    proj = _inproj(x2, ln1_g.reshape(1, d), w_perm)

    lane_pad = lambda v: jnp.concatenate([v, jnp.zeros((LANE - v.shape[0],), f32)]).reshape(1, LANE)
    y_ssd = _ssd(proj, bsz, t_len, conv_w, conv_b.reshape(1, -1), lane_pad(dt_bias),
                 lane_pad(a_log), jnp.repeat(d_skip, SSD_HEAD_DIM).reshape(1, -1),
                 ssd_norm_g.reshape(1, -1))

    kc, vct = _compress(proj, bsz, t_len, cmp_pe, cmp_w1.astype(bf16), cmp_w2.astype(bf16),
                        k_norm_g[0:1])
    qt, ks, vst, kw, vwt = _qk_prep(proj, q_norm_g.reshape(1, DH), k_norm_g, bsz, t_len, NSA_TQ)
    gates = proj[:, COL_DT + GATE_OFF:COL_DT + GATE_OFF + 3 * NSA_HEADS]
    gates = gates.reshape(n, NSA_KV_HEADS, 3 * NSA_GQA).transpose(1, 2, 0)
    y_nsa = _nsa_attention(qt, kc, vct, ks, vst, kw, vwt, gates, bsz, t_len, NSA_TQ)

    wo = w_out.astype(bf16)
    wr = jnp.concatenate([w_router, jnp.zeros((d, LANE - N_EXPERTS), f32)], axis=1)
    wr_hi = wr.astype(bf16)
    wr = jnp.concatenate([wr_hi, (wr - wr_hi.astype(f32)).astype(bf16)], axis=1)
    br =jnp.concatenate([b_router, jnp.zeros((LANE - N_EXPERTS,), f32)]).reshape(1, LANE)
    x1 = _outproj(y_ssd, y_nsa, wo[:SSD_D_INNER], wo[SSD_D_INNER:], x2)
    h3, ti, tw = _norm_router(x1, ln2_g.reshape(1, d), wr, br)

    tok_sorted, blk_e, blk_s0, blk_end, blk_nv, n_used, pos = _route(ti[:, :TOP_K])
    xs = _gather_tokens(h3, tok_sorted, blk_s0, blk_end, n_used, MOE_TM)
    act = _moe_up(blk_e, blk_nv, xs, w_up, b_up.reshape(N_EXPERTS, 1, -1))
    ys3 = _moe_down(blk_e, blk_nv, act, w_down, b_down.reshape(N_EXPERTS, 1, -1))
    out = _combine(ys3, pos, tw[:, :TOP_K], x1)
    return out.reshape(bsz, t_len, d)


def kernel(x, ln1_g, w_in, conv_w, conv_b, dt_bias, a_log, d_skip, ssd_norm_g, q_norm_g,
           k_norm_g, cmp_pe, cmp_w1, cmp_w2, w_out, ln2_g, w_router, b_router, w_up, b_up,
           w_down, b_down):
    return _layer(x, ln1_g[0], w_in[0], conv_w[0], conv_b[0], dt_bias[0], a_log[0], d_skip[0],
                  ssd_norm_g[0], q_norm_g[0], k_norm_g[0], cmp_pe[0], cmp_w1[0], cmp_w2[0],
                  w_out[0], ln2_g[0], w_router[0], b_router[0], w_up[0], b_up[0], w_down[0],
                  b_down[0])
```
